```python
import functools
import jax, jax.numpy as jnp
from jax import lax
import numpy as np

D_MODEL = 1024
BATCH = 4
SEQ = 8192
DEPTH = 4

CTX_LEN = 256
GRID_W = 64
N_MIXERS = 3
D_FF = 2816
N_MOD = 9
NORM_EPS = 1e-6
CHUNK = 128
RW_HEAD = 64
RW_HEADS = D_MODEL // RW_HEAD
RW_DECAY_LORA = 64
RW_AAA_LORA = 64
RW_GATE_LORA = 128
RW_LNX_EPS = 64e-5
RT_HEADS = 4
RT_DK = D_MODEL // RT_HEADS
RT_DV = 2 * D_MODEL // RT_HEADS
ROPE_BASE = 10000.0
ML_INNER = 2 * D_MODEL
ML_HEADS = 4
ML_DH = ML_INNER // ML_HEADS
ML_BLOCK = 4
ML_CONV = 3
ML_NORM_EPS = 1e-5
N_A = (DEPTH + 2) // 3
N_B = (DEPTH + 1) // 3
N_C = DEPTH // 3

kernel_name = "hybrid_rwkv7_retention_mlstm_macaron_adaln"

F32 = jnp.float32


def rmsnorm(x, g):
    x32 = x.astype(F32)
    y = x32 * lax.rsqrt(jnp.mean(x32 * x32, axis=-1, keepdims=True) + NORM_EPS)
    return y.astype(x.dtype) * g


def head_layernorm(x, g, b, eps):
    x32 = x.astype(F32)
    mu = jnp.mean(x32, axis=-1, keepdims=True)
    xc = x32 - mu
    y = (xc * lax.rsqrt(jnp.mean(xc * xc, axis=-1, keepdims=True) + eps)).astype(x.dtype) * g
    return y if b is None else y + b


def head_rms(x):
    x32 = x.astype(F32)
    return x32 * lax.rsqrt(jnp.mean(x32 * x32, axis=-1, keepdims=True) + NORM_EPS)


def modulate(x, g, shift, scale):
    return rmsnorm(x, g) * (1.0 + scale) + shift


def swiglu(h, w13, w2):
    a, b = jnp.split(h @ w13, 2, axis=-1)
    return (jax.nn.silu(a) * b) @ w2


def to_bhtd(x, n_heads):
    B, T, C = x.shape
    return jnp.swapaxes(x.reshape(B, T, n_heads, C // n_heads), 1, 2).astype(F32)


def from_bhtd(y):
    return jnp.swapaxes(y, 1, 2)


def to_chunks(a):
    B, H, T = a.shape[:3]
    a = a.reshape((B, H, T // CHUNK, CHUNK) + a.shape[3:])
    return jnp.moveaxis(a, 2, 0)


def from_chunks(a):
    a = jnp.moveaxis(a, 0, 2)
    return a.reshape(a.shape[:2] + (-1,) + a.shape[4:])


def run_bidirectional(scan_f, scan_b, ctx_f, ctx_b, lat_f, lat_b, state0):
    flip = lambda arrs: tuple(jnp.flip(a, axis=2) for a in arrs)
    yc_f, s_f = scan_f(ctx_f, state0)
    yc_b, s_b = scan_b(flip(ctx_b), state0)
    yl_f, _ = scan_f(lat_f, s_f)
    yl_b, _ = scan_b(flip(lat_b), s_b)
    return yc_f + jnp.flip(yc_b, axis=2), yl_f + jnp.flip(yl_b, axis=2)


def centred_neighbour_mean(x):
    xp = jnp.pad(x, ((0, 0), (1, 1), (0, 0)))
    return 0.5 * (xp[:, :-2] + xp[:, 2:])


def l2_normalize_heads(x, n_heads):
    B, T, C = x.shape
    xh = x.reshape(B, T, n_heads, C // n_heads).astype(F32)
    xh = xh / jnp.maximum(jnp.sqrt(jnp.sum(xh * xh, axis=-1, keepdims=True)), 1e-12)
    return xh.reshape(B, T, C).astype(x.dtype)


def rwkv7_scan(inputs, S0):
    def step(S, xt):
        r_t, w_t, k_t, v_t, a_t, b_t = xt
        sa = jnp.einsum('bhvk,bhk->bhv', S, a_t)
        S = S * w_t[:, :, None, :] + sa[..., None] * b_t[:, :, None, :] + v_t[..., None] * k_t[:, :, None, :]
        return S, jnp.einsum('bhvk,bhk->bhv', S, r_t)
    xs = tuple(jnp.moveaxis(t, 2, 0) for t in inputs)
    S, ys = lax.scan(step, S0, xs)
    return jnp.moveaxis(ys, 0, 2), S


def rwkv7_mixer(hc, hl, ctx_out, mu, w_rkv, w0, w1, w2, a0, a1, a2, g1, g2, k_k, k_a, r_k,
                lnx_g, lnx_b, w_o):
    def branches(h):
        xx = centred_neighbour_mean(h) - h
        xr, xw, xk, xv, xa, xg = (h + xx * mu[i] for i in range(6))
        r = xr @ w_rkv[0]
        k = xk @ w_rkv[1]
        v = xv @ w_rkv[2]
        g = jax.nn.sigmoid(xg @ g1) @ g2
        kk = l2_normalize_heads(k * k_k, RW_HEADS)
        per_dir = []
        for d in range(2):
            wl = (w0[d] + jnp.tanh(xw @ w1[d]) @ w2[d]).astype(F32)
            decay = jnp.exp(-jnp.exp(-jax.nn.softplus(-wl) - 0.5))
            a = jax.nn.sigmoid(a0[d] + (xa @ a1[d]) @ a2[d])
            kd = k * (1.0 + (a - 1.0) * k_a)
            per_dir.append((kd, tuple(to_bhtd(t, RW_HEADS) for t in (r, decay, kd, v, -kk, kk * a))))
        return per_dir, r, v, g

    def readout(y, per_dir, r, v, g):
        B, T, _ = r.shape
        yh = head_layernorm(from_bhtd(y), lnx_g.reshape(RW_HEADS, RW_HEAD),
                            lnx_b.reshape(RW_HEADS, RW_HEAD), RW_LNX_EPS)
        rh = r.reshape(B, T, RW_HEADS, RW_HEAD)
        vh = v.reshape(B, T, RW_HEADS, RW_HEAD)
        bonus = sum(jnp.sum(rh * kd.reshape(B, T, RW_HEADS, RW_HEAD) * r_k, axis=-1, keepdims=True)
                    for kd, _ in per_dir) * vh
        return ((yh + bonus).reshape(B, T, D_MODEL) * g) @ w_o

    pc, rc, vc, gc = branches(hc)
    pl, rl, vl, gl = branches(hl)
    S0 = jnp.zeros((hl.shape[0], RW_HEADS, RW_HEAD, RW_HEAD), F32)
    yc, yl = run_bidirectional(rwkv7_scan, rwkv7_scan, pc[0][1], pc[1][1], pl[0][1], pl[1][1], S0)
    oc = readout(yc, pc, rc, vc, gc) if ctx_out else None
    return oc, readout(yl, pl, rl, vl, gl)


def axial_rope(x, rows, cols):
    half = x.shape[-1] // 2
    quarter = half // 2
    freqs = ROPE_BASE ** (-jnp.arange(quarter, dtype=F32) / quarter)

    def rot(xp, pos):
        ang = pos.astype(F32)[:, None] * freqs
        cos = jnp.cos(ang)[None, :, None, :]
        sin = jnp.sin(ang)[None, :, None, :]
        x1, x2 = jnp.split(xp, 2, axis=-1)
        return jnp.concatenate([x1 * cos - x2 * sin, x1 * sin + x2 * cos], axis=-1)
    return jnp.concatenate([rot(x[..., :half], rows), rot(x[..., half:], cols)], axis=-1)


def retention_scan(inputs, R0, log_gamma):
    idx = jnp.arange(CHUNK, dtype=F32)
    lg1 = log_gamma.astype(F32)[:, None]
    diff = idx[:, None] - idx[None, :]
    intra = jnp.where(diff >= 0, jnp.exp(jnp.maximum(diff, 0.0) * lg1[:, :, None]), 0.0)
    q_dec = jnp.exp((idx + 1.0) * lg1)[..., None]
    k_dec = jnp.exp((CHUNK - 1.0 - idx) * lg1)[..., None]
    chunk_dec = jnp.exp(CHUNK * lg1)[..., None]

    def step(R, blk):
        qc, kc, vc = blk
        s = jnp.einsum('bhik,bhjk->bhij', qc, kc) * intra
        o = jnp.einsum('bhij,bhjv->bhiv', s, vc) + jnp.einsum('bhik,bhkv->bhiv', qc * q_dec, R)
        R = chunk_dec * R + jnp.einsum('bhjk,bhjv->bhkv', kc * k_dec, vc)
        return R, o
    R, o = lax.scan(step, R0, tuple(to_chunks(a) for a in inputs))
    return from_chunks(o), R


def retention_mixer(hc, hl, ctx_out, rows, cols, w_in, log_gamma, w_o):
    def proj(h, pos):
        B, T, _ = h.shape
        q, k, v, g = jnp.split(h @ w_in, [D_MODEL, 2 * D_MODEL, 4 * D_MODEL], axis=-1)
        q = q.reshape(B, T, RT_HEADS, RT_DK)
        k = k.reshape(B, T, RT_HEADS, RT_DK) * RT_DK ** -0.5
        if pos is not None:
            q = axial_rope(q, *pos)
            k = axial_rope(k, *pos)
        ins = (jnp.swapaxes(q, 1, 2).astype(F32), jnp.swapaxes(k, 1, 2).astype(F32), to_bhtd(v, RT_HEADS))
        return ins, g

    def readout(o, g):
        B, T, _ = g.shape
        return (jax.nn.silu(g) * head_rms(from_bhtd(o)).reshape(B, T, RT_HEADS * RT_DV)) @ w_o

    lg = -jnp.abs(log_gamma.astype(F32))
    scan_f = functools.partial(retention_scan, log_gamma=lg[0])
    scan_b = functools.partial(retention_scan, log_gamma=lg[1])
    ic, gc = proj(hc, None)
    il, gl = proj(hl, (rows, cols))
    R0 = jnp.zeros((hl.shape[0], RT_HEADS, RT_DK, RT_DV), F32)
    oc, ol = run_bidirectional(scan_f, scan_b, ic, ic, il, il, R0)
    return (readout(oc, gc) if ctx_out else None), readout(ol, gl)


def depthwise_conv(x, w, b):
    y = lax.conv_general_dilated(x, w[:, None, :].astype(x.dtype), window_strides=(1,),
                                 padding=[(ML_CONV // 2, ML_CONV // 2)],
                                 dimension_numbers=('NWC', 'WIO', 'NWC'),
                                 feature_group_count=x.shape[-1])
    return y + b


def headwise(x, w):
    B, T, C = x.shape
    return jnp.einsum('btgi,gio->btgo', x.reshape(B, T, C // ML_BLOCK, ML_BLOCK), w).reshape(B, T, C)


def mlstm_scan(inputs, state0):
    causal = jnp.tril(jnp.ones((CHUNK, CHUNK), dtype=bool))

    def step(state, blk):
        C, n, m = state
        qc, kc, vc, igc, lfc = blk
        b = jnp.cumsum(lfc, axis=-1)
        logd = jnp.where(causal, b[..., :, None] - b[..., None, :] + igc[..., None, :], -jnp.inf)
        m_inter = b + m[..., None]
        m_t = jnp.maximum(m_inter, jnp.max(logd, axis=-1))
        s = jnp.einsum('bhik,bhjk->bhij', qc, kc) * jnp.exp(logd - m_t[..., None])
        inter = jnp.exp(m_inter - m_t)
        num = jnp.einsum('bhij,bhjv->bhiv', s, vc) + inter[..., None] * jnp.einsum('bhvk,bhik->bhiv', C, qc)
        den = jnp.sum(s, axis=-1) + inter * jnp.einsum('bhk,bhik->bhi', n, qc)
        h = num / jnp.maximum(jnp.abs(den), jnp.exp(-m_t))[..., None]
        gj = b[..., -1:] - b + igc
        m_end = b[..., -1] + m
        m_new = jnp.maximum(m_end, jnp.max(gj, axis=-1))
        carry = jnp.exp(m_end - m_new)
        wj = jnp.exp(gj - m_new[..., None])
        C = carry[..., None, None] * C + jnp.einsum('bhj,bhjv,bhjk->bhvk', wj, vc, kc)
        n = carry[..., None] * n + jnp.einsum('bhj,bhjk->bhk', wj, kc)
        return (C, n, m_new), h
    state, h = lax.scan(step, state0, tuple(to_chunks(a) for a in inputs))
    return from_chunks(h), state


def mlstm_mixer(hc, hl, ctx_out, w_in, conv_w, conv_b, w_q, w_k, w_v, w_if, b_if, skip, norm_g, w_o):
    def proj(h):
        xm, z = jnp.split(h @ w_in, 2, axis=-1)
        xconv = jax.nn.silu(depthwise_conv(xm, conv_w, conv_b))
        q = headwise(xconv, w_q)
        k = headwise(xconv, w_k)
        v = headwise(xm, w_v)
        gin = jnp.concatenate([q, k, v], axis=-1)
        qh = to_bhtd(q, ML_HEADS)
        kh = to_bhtd(k, ML_HEADS) * ML_DH ** -0.5
        vh = to_bhtd(v, ML_HEADS)
        dirs = []
        for d in range(2):
            pre = jnp.swapaxes((gin @ w_if[d] + b_if[d]).astype(F32), 1, 2)
            dirs.append((qh, kh, vh, pre[:, :ML_HEADS], jax.nn.log_sigmoid(pre[:, ML_HEADS:])))
        return dirs, xconv, z

    def readout(y, xconv, z):
        B, T, _ = z.shape
        yh = head_layernorm(from_bhtd(y), norm_g.reshape(ML_HEADS, ML_DH), None, ML_NORM_EPS)
        return ((yh.reshape(B, T, ML_INNER) + skip * xconv) * jax.nn.silu(z)) @ w_o

    dc, xcc, zc = proj(hc)
    dl, xcl, zl = proj(hl)
    B = hl.shape[0]
    st0 = (jnp.zeros((B, ML_HEADS, ML_DH, ML_DH), F32), jnp.zeros((B, ML_HEADS, ML_DH), F32),
           jnp.zeros((B, ML_HEADS), F32))
    yc, yl = run_bidirectional(mlstm_scan, mlstm_scan, dc[0], dc[1], dl[0], dl[1], st0)
    return (readout(yc, xcc, zc) if ctx_out else None), readout(yl, xcl, zl)


def macaron_layer(x, xc, mod_l, mod_c, norm_g, w13, w2, mixer, last):
    ml = jnp.split(mod_l, N_MOD, axis=-1)
    mc = jnp.split(mod_c, N_MOD, axis=-1)

    def half_ffn(h, m, s):
        f = s // 2
        return h + 0.5 * m[3 * s + 2] * swiglu(modulate(h, norm_g[s], m[3 * s], m[3 * s + 1]), w13[f], w2[f])

    x = half_ffn(x, ml, 0)
    xc = half_ffn(xc, mc, 0)
    oc, ol = mixer(modulate(xc, norm_g[1], mc[3], mc[4]), modulate(x, norm_g[1], ml[3], ml[4]), not last)
    x = half_ffn(x + ml[5] * ol, ml, 2)
    if not last:
        xc = half_ffn(xc + mc[5] * oc, mc, 2)
    return x, xc


def setup_inputs(seed: int = 0) -> dict:
    key = jax.random.key(seed)
    ks = iter(jax.random.split(key, 64))
    nrm = lambda shape, scale: scale * jax.random.normal(next(ks), shape, F32)
    D = D_MODEL
    x = nrm((BATCH, SEQ, D), 1.0)
    c = nrm((BATCH, D), 1.0)
    ctx = nrm((BATCH, CTX_LEN, D), 1.0)
    c_ctx = nrm((D,), 1.0)
    mod_w = nrm((DEPTH, D, N_MOD * D), 0.5 * D ** -0.5)
    mod_b = nrm((DEPTH, N_MOD * D), 0.02)
    norm_g = 1.0 + nrm((DEPTH, 3, D), 0.02)
    ffn_w13 = nrm((DEPTH, 2, D, 2 * D_FF), D ** -0.5)
    ffn_w2 = nrm((DEPTH, 2, D_FF, D), D_FF ** -0.5)
    final_g = 1.0 + nrm((D,), 0.02)
    rw_mu = jax.random.uniform(next(ks), (N_A, 6, D), F32, 0.0, 1.0)
    rw_wrkv = nrm((N_A, 3, D, D), D ** -0.5)
    ratio = jnp.linspace(0.0, 1.0, D, dtype=F32) ** 0.9
    rw_w0 = -6.0 + 5.0 * ratio + nrm((N_A, 2, D), 0.3)
    rw_w1 = nrm((N_A, 2, D, RW_DECAY_LORA), D ** -0.5)
    rw_w2 = nrm((N_A, 2, RW_DECAY_LORA, D), 0.1 * RW_DECAY_LORA ** -0.5)
    rw_a0 = nrm((N_A, 2, D), 0.1)
    rw_a1 = nrm((N_A, 2, D, RW_AAA_LORA), D ** -0.5)
    rw_a2 = nrm((N_A, 2, RW_AAA_LORA, D), 0.1 * RW_AAA_LORA ** -0.5)
    rw_g1 = nrm((N_A, D, RW_GATE_LORA), D ** -0.5)
    rw_g2 = nrm((N_A, RW_GATE_LORA, D), RW_GATE_LORA ** -0.5)
    rw_kk = 0.85 + nrm((N_A, D), 0.02)
    rw_ka = 1.0 + nrm((N_A, D), 0.02)
    rw_rk = nrm((N_A, RW_HEADS, RW_HEAD), 0.1)
    rw_lnx_g = 1.0 + nrm((N_A, D), 0.02)
    rw_lnx_b = nrm((N_A, D), 0.02)
    rw_wo = nrm((N_A, D, D), D ** -0.5)
    rt_win = nrm((N_B, D, 6 * D), D ** -0.5)
    base_lg = jnp.log(1.0 - 2.0 ** (-5.0 - jnp.arange(RT_HEADS, dtype=F32)))
    rt_log_gamma = base_lg * (1.0 + nrm((N_B, 2, RT_HEADS), 0.05))
    rt_wo = nrm((N_B, 2 * D, D), (2 * D) ** -0.5)
    ml_win = nrm((N_C, D, 2 * ML_INNER), D ** -0.5)
    ml_conv_w = nrm((N_C, ML_CONV, ML_INNER), ML_CONV ** -0.5)
    ml_conv_b = nrm((N_C, ML_INNER), 0.02)
    ml_wq = nrm((N_C, ML_INNER // ML_BLOCK, ML_BLOCK, ML_BLOCK), ML_BLOCK ** -0.5)
    ml_wk = nrm((N_C, ML_INNER // ML_BLOCK, ML_BLOCK, ML_BLOCK), ML_BLOCK ** -0.5)
    ml_wv = nrm((N_C, ML_INNER // ML_BLOCK, ML_BLOCK, ML_BLOCK), ML_BLOCK ** -0.5)
    ml_wif = nrm((N_C, 2, 3 * ML_INNER, 2 * ML_HEADS), 0.5 * (3 * ML_INNER) ** -0.5)
    fbias = jnp.linspace(3.0, 6.0, ML_HEADS, dtype=F32)
    ml_bif = jnp.concatenate([nrm((N_C, 2, ML_HEADS), 0.1), fbias + nrm((N_C, 2, ML_HEADS), 0.1)], axis=-1)
    ml_skip = 1.0 + nrm((N_C, ML_INNER), 0.02)
    ml_norm_g = 1.0 + nrm((N_C, ML_INNER), 0.02)
    ml_wo = nrm((N_C, ML_INNER, D), ML_INNER ** -0.5)
    return {"x": x, "c": c, "ctx": ctx, "c_ctx": c_ctx,
            "mod_w": mod_w, "mod_b": mod_b, "norm_g": norm_g, "ffn_w13": ffn_w13, "ffn_w2": ffn_w2,
            "final_g": final_g,
            "rw_mu": rw_mu, "rw_wrkv": rw_wrkv, "rw_w0": rw_w0, "rw_w1": rw_w1, "rw_w2": rw_w2,
            "rw_a0": rw_a0, "rw_a1": rw_a1, "rw_a2": rw_a2, "rw_g1": rw_g1, "rw_g2": rw_g2,
            "rw_kk": rw_kk, "rw_ka": rw_ka, "rw_rk": rw_rk, "rw_lnx_g": rw_lnx_g, "rw_lnx_b": rw_lnx_b,
            "rw_wo": rw_wo,
            "rt_win": rt_win, "rt_log_gamma": rt_log_gamma, "rt_wo": rt_wo,
            "ml_win": ml_win, "ml_conv_w": ml_conv_w, "ml_conv_b": ml_conv_b, "ml_wq": ml_wq,
            "ml_wk": ml_wk, "ml_wv": ml_wv, "ml_wif": ml_wif, "ml_bif": ml_bif, "ml_skip": ml_skip,
            "ml_norm_g": ml_norm_g, "ml_wo": ml_wo}


def reference(x, c, ctx, c_ctx, mod_w, mod_b, norm_g, ffn_w13, ffn_w2, final_g,
              rw_mu, rw_wrkv, rw_w0, rw_w1, rw_w2, rw_a0, rw_a1, rw_a2, rw_g1, rw_g2,
              rw_kk, rw_ka, rw_rk, rw_lnx_g, rw_lnx_b, rw_wo,
              rt_win, rt_log_gamma, rt_wo,
              ml_win, ml_conv_w, ml_conv_b, ml_wq, ml_wk, ml_wv, ml_wif, ml_bif, ml_skip,
              ml_norm_g, ml_wo):
    T = x.shape[1]
    ROWS = T // GRID_W
    rows = jnp.repeat(jnp.arange(ROWS, dtype=jnp.int32), GRID_W)
    cols = jnp.tile(jnp.arange(GRID_W, dtype=jnp.int32), ROWS)
    silu_c = jax.nn.silu(c)[:, None, :]
    silu_cc = jax.nn.silu(c_ctx)[None, None, :]
    xc = ctx
    for i in range(DEPTH):
        kind, s = i % N_MIXERS, i // N_MIXERS
        mod_l = silu_c @ mod_w[i] + mod_b[i]
        mod_c = silu_cc @ mod_w[i] + mod_b[i]
        if kind == 0:
            mixer = functools.partial(
                rwkv7_mixer, mu=rw_mu[s], w_rkv=rw_wrkv[s], w0=rw_w0[s], w1=rw_w1[s], w2=rw_w2[s],
                a0=rw_a0[s], a1=rw_a1[s], a2=rw_a2[s], g1=rw_g1[s], g2=rw_g2[s], k_k=rw_kk[s],
                k_a=rw_ka[s], r_k=rw_rk[s], lnx_g=rw_lnx_g[s], lnx_b=rw_lnx_b[s], w_o=rw_wo[s])
        elif kind == 1:
            mixer = functools.partial(
                retention_mixer, rows=rows, cols=cols, w_in=rt_win[s], log_gamma=rt_log_gamma[s],
                w_o=rt_wo[s])
        else:
            mixer = functools.partial(
                mlstm_mixer, w_in=ml_win[s], conv_w=ml_conv_w[s], conv_b=ml_conv_b[s], w_q=ml_wq[s],
                w_k=ml_wk[s], w_v=ml_wv[s], w_if=ml_wif[s], b_if=ml_bif[s], skip=ml_skip[s],
                norm_g=ml_norm_g[s], w_o=ml_wo[s])
        x, xc = macaron_layer(x, xc, mod_l, mod_c, norm_g[i], ffn_w13[i], ffn_w2[i], mixer,
                              i == DEPTH - 1)
    return rmsnorm(x, final_g)
```

```python
import functools
import math

import jax
import jax.numpy as jnp
from jax import lax
from jax.experimental import pallas as pl
from jax.experimental.pallas import tpu as pltpu

F32 = jnp.float32
BF16 = jnp.bfloat16
HI = lax.Precision.HIGHEST

LANE = 128
SUB = 8
VMEM_LIMIT = 56 * 1024 * 1024

D = 1024
D_FF = 2816
N_MOD = 9
NORM_EPS = 1e-6
GRID_W = 64
RW_HEAD = 64
RW_LNX_EPS = 64e-5
RW_CHUNK = 64
RW_PAIRS = D // LANE
RT_HEADS = 4
RT_DK = D // RT_HEADS
RT_DV = 2 * D // RT_HEADS
ROPE_BASE = 10000.0
RT_CHUNK = 128
ML_INNER = 2 * D
ML_HEADS = 4
ML_DH = ML_INNER // ML_HEADS
ML_BLOCK = 4
ML_NORM_EPS = 1e-5
ML_CHUNK = 128
ML_TILES = ML_INNER // LANE

ROW_TILE = 256
FFN_ROW_TILE = 512
FFN_COL_TILE = D_FF // 2

NT = (((1,), (1,)), ((), ()))
TN = (((0,), (0,)), ((), ()))


def _params(*sem):
    return pltpu.CompilerParams(dimension_semantics=sem, vmem_limit_bytes=VMEM_LIMIT)


def _sigmoid(x):
    return 1.0 / (1.0 + jnp.exp(-x))


def _silu(x):
    return x * _sigmoid(x)


def _rms(x, eps):
    return x * lax.rsqrt(jnp.mean(x * x, axis=-1, keepdims=True) + eps)


def _modnorm(x, g, shift, scale):
    return _rms(x, NORM_EPS) * g * (1.0 + scale) + shift


def _mm(a, b):
    return jnp.dot(a.astype(BF16), b, preferred_element_type=F32)


def _iota(shape, dim):
    return lax.broadcasted_iota(jnp.int32, shape, dim)


def _mod_body(c_ref, w_ref, b_ref, o_ref):
    s = _silu(c_ref[...])
    o_ref[...] = jnp.dot(s, w_ref[...], precision=HI, preferred_element_type=F32) + b_ref[...]


def _mod_call(cvec, mod_w, mod_b):
    depth, _, n = mod_w.shape
    tn = D
    return pl.pallas_call(
        _mod_body,
        grid=(depth, n // tn),
        in_specs=[pl.BlockSpec((SUB, D), lambda l, j: (0, 0)),
                  pl.BlockSpec((None, D, tn), lambda l, j: (l, 0, j)),
                  pl.BlockSpec((None, 1, tn), lambda l, j: (l, 0, j))],
        out_specs=pl.BlockSpec((None, SUB, tn), lambda l, j: (l, 0, j)),
        out_shape=jax.ShapeDtypeStruct((depth, SUB, n), F32),
        compiler_params=_params("parallel", "parallel"),
        name="adaln_mod",
    )(cvec, mod_w, mod_b.reshape(depth, 1, n))


def _ffn_body(x_ref, mod_ref, ng_ref, w1_ref, w3_ref, w2_ref, *rest, sub, final):
    if final:
        fg_ref, o_ref, y_scr, acc_scr = rest
    else:
        o_ref, y_scr, acc_scr = rest
    j = pl.program_id(2)

    @pl.when(j == 0)
    def _():
        y = _modnorm(x_ref[...], ng_ref[sub:sub + 1], mod_ref[3 * sub:3 * sub + 1],
                     mod_ref[3 * sub + 1:3 * sub + 2])
        y_scr[...] = y.astype(BF16)
        acc_scr[...] = jnp.zeros_like(acc_scr)

    y = y_scr[...]
    a = jnp.dot(y, w1_ref[...], preferred_element_type=F32)
    b = jnp.dot(y, w3_ref[...], preferred_element_type=F32)
    acc_scr[...] += _mm(_silu(a) * b, w2_ref[...])

    @pl.when(j == pl.num_programs(2) - 1)
    def _():
        out = x_ref[...] + 0.5 * mod_ref[3 * sub + 2:3 * sub + 3] * acc_scr[...]
        if final:
            out = _rms(out, NORM_EPS) * fg_ref[...]
        o_ref[...] = out


def _ffn_call(x, mod, ng, w13, w2, sub, final_g=None):
    b, t, _ = x.shape
    tm = min(FFN_ROW_TILE, t)
    nf = D_FF // FFN_COL_TILE
    in_specs = [pl.BlockSpec((None, tm, D), lambda bi, i, j: (bi, i, 0)),
                pl.BlockSpec((None, N_MOD, D), lambda bi, i, j: (bi, 0, 0)),
                pl.BlockSpec((3, D), lambda bi, i, j: (0, 0)),
                pl.BlockSpec((D, FFN_COL_TILE), lambda bi, i, j: (0, j)),
                pl.BlockSpec((D, FFN_COL_TILE), lambda bi, i, j: (0, j + nf)),
                pl.BlockSpec((FFN_COL_TILE, D), lambda bi, i, j: (j, 0))]
    args = [x, mod, ng, w13, w13, w2]
    if final_g is not None:
        in_specs.append(pl.BlockSpec((1, D), lambda bi, i, j: (0, 0)))
        args.append(final_g.reshape(1, D))
    return pl.pallas_call(
        functools.partial(_ffn_body, sub=sub, final=final_g is not None),
        grid=(b, t // tm, nf),
        in_specs=in_specs,
        out_specs=pl.BlockSpec((None, tm, D), lambda bi, i, j: (bi, i, 0)),
        out_shape=jax.ShapeDtypeStruct((b, t, D), F32),
        scratch_shapes=[pltpu.VMEM((tm, D), BF16), pltpu.VMEM((tm, D), F32)],
        compiler_params=_params("parallel", "parallel", "arbitrary"),
        name="half_ffn",
    )(*args)


def _row_specs(t, tm):
    nb = tm // SUB
    last = t // SUB - 1
    return [pl.BlockSpec((None, tm, D), lambda bi, i: (bi, i, 0)),
            pl.BlockSpec((None, SUB, D), lambda bi, i: (bi, jnp.maximum(i * nb - 1, 0), 0)),
            pl.BlockSpec((None, SUB, D), lambda bi, i: (bi, jnp.minimum((i + 1) * nb, last), 0))]


def _full(shape):
    nd = len(shape)
    return pl.BlockSpec(shape, lambda bi, i: (0,) * nd)


def _tile(width, tm):
    return pl.BlockSpec((None, tm, width), lambda bi, i: (bi, i, 0))


def _tile2(width, tm):
    return pl.BlockSpec((2, None, tm, width), lambda bi, i: (0, bi, i, 0))


_MOD_SPEC = pl.BlockSpec((None, N_MOD, D), lambda bi, i: (bi, 0, 0))


def _chunk_index(d, c, nc):
    return c + d * (nc - 1 - 2 * c)


def _group_ones():
    r = jnp.right_shift(_iota((LANE, LANE), 0), 6)
    c = jnp.right_shift(_iota((LANE, LANE), 1), 6)
    return (r == c).astype(BF16)


def _head_sum64(x, ones):
    parts = [_mm(x[:, t * LANE:(t + 1) * LANE], ones) for t in range(x.shape[1] // LANE)]
    return jnp.concatenate(parts, axis=-1)


def _rw_proj_body(x_ref, xp_ref, xn_ref, mod_ref, ng_ref, vec_ref, wrkv_ref, wd_ref, w2_ref, a2_ref, g2_ref,
                  r_ref, v_ref, g_ref, kk_ref, lw_ref, kd_ref, b_ref):
    i = pl.program_id(1)
    tm = x_ref.shape[0]
    g, shift, scale = ng_ref[1:2], mod_ref[3:4], mod_ref[4:5]
    h = _modnorm(x_ref[...], g, shift, scale)
    hp = jnp.where(i > 0, _modnorm(xp_ref[SUB - 1:SUB], g, shift, scale), 0.0)
    hn = jnp.where(i < pl.num_programs(1) - 1, _modnorm(xn_ref[0:1], g, shift, scale), 0.0)
    row = _iota((tm, 1), 0)
    h_prev = jnp.where(row == 0, hp, pltpu.roll(h, 1, axis=0))
    h_next = jnp.where(row == tm - 1, hn, pltpu.roll(h, tm - 1, axis=0))
    xx = 0.5 * (h_prev + h_next) - h

    def mix(n):
        return (h + xx * vec_ref[n:n + 1]).astype(BF16)

    r = jnp.dot(mix(0), wrkv_ref[0], preferred_element_type=F32)
    k = jnp.dot(mix(2), wrkv_ref[1], preferred_element_type=F32)
    v = jnp.dot(mix(3), wrkv_ref[2], preferred_element_type=F32)
    wl = _mm(jnp.tanh(jnp.dot(mix(1), wd_ref[0], preferred_element_type=F32)), w2_ref[...])
    al = _mm(jnp.dot(mix(4), wd_ref[1], preferred_element_type=F32), a2_ref[...])
    gg = _mm(_sigmoid(jnp.dot(mix(5), wd_ref[2], preferred_element_type=F32)), g2_ref[...])

    kkr = k * vec_ref[10:11]
    n2 = _head_sum64(kkr * kkr, _group_ones())
    kk = kkr / jnp.maximum(jnp.sqrt(n2), 1e-12)
    for d in range(2):
        lw_ref[d] = -math.exp(-0.5) * _sigmoid(vec_ref[6 + d:7 + d] + wl[:, d * D:(d + 1) * D])
        a = _sigmoid(vec_ref[8 + d:9 + d] + al[:, d * D:(d + 1) * D])
        kd_ref[d] = k * (1.0 + (a - 1.0) * vec_ref[11:12])
        b_ref[d] = kk * a
    r_ref[...] = r
    v_ref[...] = v
    g_ref[...] = gg
    kk_ref[...] = kk


def _rw_proj_call(x, mod, ng, w):
    b, t, _ = x.shape
    tm = min(ROW_TILE, t)
    one = jax.ShapeDtypeStruct((b, t, D), F32)
    two = jax.ShapeDtypeStruct((2, b, t, D), F32)
    return pl.pallas_call(
        _rw_proj_body,
        grid=(b, t // tm),
        in_specs=_row_specs(t, tm) + [_MOD_SPEC, _full((3, D)), _full((16, D)), _full((3, D, D)),
                                      _full((3, D, LANE)), _full((LANE, 2 * D)), _full((LANE, 2 * D)),
                                      _full((LANE, D))],
        out_specs=[_tile(D, tm)] * 4 + [_tile2(D, tm)] * 3,
        out_shape=[one] * 4 + [two] * 3,
        compiler_params=_params("parallel", "parallel"),
        name="rwkv_proj",
    )(x, x, x, mod, ng, w["vec"], w["wrkv"], w["wd"], w["w2"], w["a2"], w["g2"])


def _rw_scan_body(r_ref, v_ref, kk_ref, lw_ref, kd_ref, b_ref, s0_ref, y_ref, s_ref):
    d = pl.program_id(0)
    c = pl.program_id(2)
    n = r_ref.shape[0]
    n2 = 2 * n

    @pl.when(c == 0)
    def _():
        s_ref[...] = s0_ref[...]

    sgn = 1 - 2 * d
    lw = lw_ref[...]
    before = ((_iota((n, n), 0) - _iota((n, n), 1)) * sgn >= 0).astype(F32)
    cum = jnp.dot(before, lw, precision=HI, preferred_element_type=F32)
    tot = jnp.sum(lw, axis=0, keepdims=True)
    kd, bb, v = kd_ref[...], b_ref[...], v_ref[...]
    rt = r_ref[...] * jnp.exp(cum)
    at = -kk_ref[...] * jnp.exp(cum - lw)
    e_out = jnp.exp(-cum)
    kt, bt = kd * e_out, bb * e_out
    e_tail = jnp.exp(tot - cum)
    kh, bh = kd * e_tail, bb * e_tail
    etot = jnp.exp(tot)

    head0 = _iota((1, LANE), 1) < RW_HEAD
    ri = jnp.bitwise_and(_iota((n2, n2), 0), n - 1)
    cj = jnp.bitwise_and(_iota((n2, n2), 1), n - 1)
    rel = (ri - cj) * sgn
    strict, incl = rel > 0, rel >= 0
    eye = (_iota((n2, n2), 0) == _iota((n2, n2), 1)).astype(F32)

    def expand(m):
        return jnp.concatenate([jnp.where(head0, m, 0.0), jnp.where(head0, 0.0, m)], axis=0).astype(BF16)

    ys = []
    for p in range(RW_PAIRS):
        sl = slice(p * LANE, (p + 1) * LANE)
        ea, er, ev = expand(at[:, sl]), expand(rt[:, sl]), expand(v[:, sl])
        lhs = jnp.concatenate([ea, er], axis=0)
        rhs = jnp.concatenate([expand(bt[:, sl]), expand(kt[:, sl])], axis=0)
        prod = lax.dot_general(lhs, rhs, NT, preferred_element_type=F32)
        a_ab = jnp.where(strict, prod[:n2, :n2], 0.0)
        a_ak = jnp.where(strict, prod[:n2, n2:], 0.0)
        a_rbk = jnp.concatenate([jnp.where(incl, prod[n2:, :n2], 0.0),
                                 jnp.where(incl, prod[n2:, n2:], 0.0)], axis=1)
        inv = eye + a_ab
        pw = a_ab
        for _ in range(int(math.log2(n)) - 1):
            pwb = pw.astype(BF16)
            pw = jnp.dot(pwb, pwb, preferred_element_type=F32)
            inv = inv + _mm(inv, pw.astype(BF16))
        s = s_ref[p]
        sb = s.astype(BF16)
        z = lax.dot_general(ea, sb, NT, preferred_element_type=F32) + _mm(a_ak, ev)
        u = _mm(inv, z.astype(BF16))
        uv = jnp.concatenate([u.astype(BF16), ev], axis=0)
        ye = lax.dot_general(er, sb, NT, preferred_element_type=F32) + _mm(a_rbk, uv)
        ys.append(ye[:n] + ye[n:])
        bk = jnp.concatenate([expand(bh[:, sl]), expand(kh[:, sl])], axis=0)
        s_ref[p] = s * etot[:, sl] + lax.dot_general(uv, bk, TN, preferred_element_type=F32)
    y_ref[...] = jnp.concatenate(ys, axis=-1)


def _rw_scan_call(r, v, kk, lw, kd, bb, s0):
    b, t, _ = r.shape
    n = RW_CHUNK
    nc = t // n
    seq = pl.BlockSpec((None, n, D), lambda d, bi, c: (bi, _chunk_index(d, c, nc), 0))
    seq2 = pl.BlockSpec((None, None, n, D), lambda d, bi, c: (d, bi, _chunk_index(d, c, nc), 0))
    st = pl.BlockSpec((None, None, RW_PAIRS, LANE, LANE), lambda d, bi, c: (d, bi, 0, 0, 0))
    return pl.pallas_call(
        _rw_scan_body,
        grid=(2, b, nc),
        in_specs=[seq, seq, seq, seq2, seq2, seq2, st],
        out_specs=[seq2, st],
        out_shape=[jax.ShapeDtypeStruct((2, b, t, D), F32), jax.ShapeDtypeStruct(s0.shape, F32)],
        compiler_params=_params("parallel", "parallel", "arbitrary"),
        name="rwkv_scan",
    )(r, v, kk, lw, kd, bb, s0)


def _rw_out_body(y_ref, r_ref, v_ref, g_ref, kd_ref, x_ref, mod_ref, vec_ref, wo_ref, o_ref):
    ones = _group_ones()
    y = y_ref[0] + y_ref[1]
    yc = y - _head_sum64(y, ones) * (1.0 / RW_HEAD)
    var = _head_sum64(yc * yc, ones) * (1.0 / RW_HEAD)
    yh = yc * lax.rsqrt(var + RW_LNX_EPS) * vec_ref[0:1] + vec_ref[1:2]
    bonus = _head_sum64(r_ref[...] * (kd_ref[0] + kd_ref[1]) * vec_ref[2:3], ones) * v_ref[...]
    o = _mm((yh + bonus) * g_ref[...], wo_ref[...])
    o_ref[...] = x_ref[...] + mod_ref[5:6] * o


def _rw_out_call(y, r, v, g, kd, x, mod, w):
    b, t, _ = x.shape
    tm = min(ROW_TILE, t)
    return pl.pallas_call(
        _rw_out_body,
        grid=(b, t // tm),
        in_specs=[_tile2(D, tm), _tile(D, tm), _tile(D, tm), _tile(D, tm), _tile2(D, tm), _tile(D, tm),
                  _MOD_SPEC, _full((SUB, D)), _full((D, D))],
        out_specs=_tile(D, tm),
        out_shape=jax.ShapeDtypeStruct((b, t, D), F32),
        compiler_params=_params("parallel", "parallel"),
        name="rwkv_readout",
    )(y, r, v, g, kd, x, mod, w["ovec"], w["wo"])


def _rwkv_mixer(xc, xl, mc, ml, ng, w, ctx_out):
    b = xl.shape[0]
    pc = _rw_proj_call(xc, mc, ng, w)
    pl_ = _rw_proj_call(xl, ml, ng, w)
    s0 = jnp.zeros((2, b, RW_PAIRS, LANE, LANE), F32)

    def scan(p, s):
        r, v, _, kk, lw, kd, bb = p
        return _rw_scan_call(r, v, kk, lw, kd, bb, s)

    def out(y, p, x, mod):
        r, v, g, _, _, kd, _ = p
        return _rw_out_call(y, r, v, g, kd, x, mod, w)

    yc, sc = scan(pc, s0)
    yl, _ = scan(pl_, sc)
    return (out(yc, pc, xc, mc) if ctx_out else None), out(yl, pl_, xl, ml)


def _rwkv_weights(mu, w_rkv, w0, w1, w2, a0, a1, a2, g1, g2, k_k, k_a, r_k, lnx_g, lnx_b, w_o):
    zeros = jnp.zeros((RW_HEAD, D), F32)

    def two_dir(m):
        return jnp.concatenate([jnp.concatenate([m[0], zeros], axis=1),
                                jnp.concatenate([zeros, m[1]], axis=1)], axis=0).astype(BF16)

    vec = jnp.concatenate([mu, w0, a0, k_k[None], k_a[None], jnp.zeros((4, D), F32)], axis=0)
    ovec = jnp.concatenate([lnx_g[None], lnx_b[None], r_k.reshape(1, D), jnp.zeros((5, D), F32)], axis=0)
    wd = jnp.stack([jnp.concatenate([w1[0], w1[1]], axis=1), jnp.concatenate([a1[0], a1[1]], axis=1), g1])
    return {"vec": vec, "ovec": ovec, "wrkv": w_rkv.astype(BF16), "wd": wd.astype(BF16), "w2": two_dir(w2),
            "a2": two_dir(a2), "g2": g2.astype(BF16), "wo": w_o.astype(BF16)}


def _rt_proj_body(x_ref, mod_ref, ng_ref, win_ref, *rest, rope):
    if rope:
        cos_ref, sin_ref, q_ref, k_ref, v_ref, g_ref = rest
    else:
        q_ref, k_ref, v_ref, g_ref = rest
    h = _modnorm(x_ref[...], ng_ref[1:2], mod_ref[3:4], mod_ref[4:5]).astype(BF16)
    q = jnp.dot(h, win_ref[:, 0:D], preferred_element_type=F32)
    k = jnp.dot(h, win_ref[:, D:2 * D], preferred_element_type=F32) * RT_DK ** -0.5
    if rope:
        cos = jnp.concatenate([cos_ref[...]] * RT_HEADS, axis=-1)
        sin = jnp.concatenate([sin_ref[...]] * RT_HEADS, axis=-1)
        low = jnp.bitwise_and(_iota((1, D), 1), LANE - 1) < LANE // 2

        def rot(t):
            partner = jnp.where(low, pltpu.roll(t, D - LANE // 2, axis=1), pltpu.roll(t, LANE // 2, axis=1))
            return t * cos + partner * sin

        q, k = rot(q), rot(k)
    q_ref[...] = q.astype(BF16)
    k_ref[...] = k.astype(BF16)
    v_ref[...] = jnp.dot(h, win_ref[:, 2 * D:4 * D], preferred_element_type=F32).astype(BF16)
    g_ref[...] = jnp.dot(h, win_ref[:, 4 * D:6 * D], preferred_element_type=F32)


def _rt_proj_call(x, mod, ng, win, tables):
    b, t, _ = x.shape
    tm = min(ROW_TILE, t)
    in_specs = [_tile(D, tm), _MOD_SPEC, _full((3, D)), _full((D, 6 * D))]
    args = [x, mod, ng, win]
    if tables is not None:
        in_specs += [pl.BlockSpec((tm, RT_DK), lambda bi, i: (i, 0))] * 2
        args += list(tables)
    return pl.pallas_call(
        functools.partial(_rt_proj_body, rope=tables is not None),
        grid=(b, t // tm),
        in_specs=in_specs,
        out_specs=[_tile(D, tm), _tile(D, tm), _tile(2 * D, tm), _tile(2 * D, tm)],
        out_shape=[jax.ShapeDtypeStruct((b, t, D), BF16), jax.ShapeDtypeStruct((b, t, D), BF16),
                   jax.ShapeDtypeStruct((b, t, 2 * D), BF16), jax.ShapeDtypeStruct((b, t, 2 * D), F32)],
        compiler_params=_params("parallel", "parallel"),
        name="retention_proj",
    )(*args)


def _rt_scan_body(lg_ref, q_ref, k_ref, v_ref, s0_ref, o_ref, s_ref):
    d = pl.program_id(0)
    c = pl.program_id(2)
    n = q_ref.shape[0]

    @pl.when(c == 0)
    def _():
        s_ref[...] = s0_ref[...]

    sgn = 1 - 2 * d
    rel = (_iota((n, n), 0) - _iota((n, n), 1)) * sgn
    dist = jnp.maximum(rel, 0).astype(F32)
    pos = (_iota((n, 1), 0) * sgn + d * (n - 1)).astype(F32)
    for h in range(RT_HEADS):
        lg = -jnp.abs(lg_ref[d, h])
        intra = jnp.where(rel >= 0, jnp.exp(dist * lg), 0.0)
        q_dec = jnp.exp((pos + 1.0) * lg)
        k_dec = jnp.exp((n - 1.0 - pos) * lg)
        c_dec = jnp.exp(jnp.full((1, 1), float(n), F32) * lg)
        qh = q_ref[:, h * RT_DK:(h + 1) * RT_DK]
        kh = k_ref[:, h * RT_DK:(h + 1) * RT_DK]
        vh = v_ref[:, h * RT_DV:(h + 1) * RT_DV]
        s = lax.dot_general(qh, kh, NT, preferred_element_type=F32) * intra
        st = s_ref[h]
        o_ref[:, h * RT_DV:(h + 1) * RT_DV] = _mm(s, vh) + q_dec * _mm(qh, st.astype(BF16))
        kv = lax.dot_general(kh, (k_dec * vh.astype(F32)).astype(BF16), TN, preferred_element_type=F32)
        s_ref[h] = c_dec * st + kv


def _rt_scan_call(lg, q, k, v, s0):
    b, t, _ = q.shape
    n = RT_CHUNK
    nc = t // n

    def seq(width):
        return pl.BlockSpec((None, n, width), lambda d, bi, c: (bi, _chunk_index(d, c, nc), 0))

    st = pl.BlockSpec((None, None, RT_HEADS, RT_DK, RT_DV), lambda d, bi, c: (d, bi, 0, 0, 0))
    return pl.pallas_call(
        _rt_scan_body,
        grid=(2, b, nc),
        in_specs=[pl.BlockSpec(memory_space=pltpu.SMEM), seq(D), seq(D), seq(2 * D), st],
        out_specs=[pl.BlockSpec((None, None, n, 2 * D), lambda d, bi, c: (d, bi, _chunk_index(d, c, nc), 0)), st],
        out_shape=[jax.ShapeDtypeStruct((2, b, t, 2 * D), F32), jax.ShapeDtypeStruct(s0.shape, F32)],
        compiler_params=_params("parallel", "parallel", "arbitrary"),
        name="retention_scan",
    )(lg, q, k, v, s0)


def _rt_out_body(o_ref, g_ref, x_ref, mod_ref, wo_ref, out_ref):
    o = o_ref[0] + o_ref[1]
    on = jnp.concatenate([_rms(o[:, h * RT_DV:(h + 1) * RT_DV], NORM_EPS) for h in range(RT_HEADS)], axis=-1)
    out_ref[...] = x_ref[...] + mod_ref[5:6] * _mm(_silu(g_ref[...]) * on, wo_ref[...])


def _rt_out_call(o, g, x, mod, wo):
    b, t, _ = x.shape
    tm = min(ROW_TILE, t)
    return pl.pallas_call(
        _rt_out_body,
        grid=(b, t // tm),
        in_specs=[_tile2(2 * D, tm), _tile(2 * D, tm), _tile(D, tm), _MOD_SPEC, _full((2 * D, D))],
        out_specs=_tile(D, tm),
        out_shape=jax.ShapeDtypeStruct((b, t, D), F32),
        compiler_params=_params("parallel", "parallel"),
        name="retention_readout",
    )(o, g, x, mod, wo)


def _rope_tables(t):
    pos = jnp.arange(t, dtype=jnp.int32)
    quarter = RT_DK // 4
    freqs = ROPE_BASE ** (-jnp.arange(quarter, dtype=F32) / quarter)
    ar = (pos // GRID_W).astype(F32)[:, None] * freqs
    ac = (pos % GRID_W).astype(F32)[:, None] * freqs
    cos = jnp.concatenate([jnp.cos(ar), jnp.cos(ar), jnp.cos(ac), jnp.cos(ac)], axis=-1)
    sin = jnp.concatenate([-jnp.sin(ar), jnp.sin(ar), -jnp.sin(ac), jnp.sin(ac)], axis=-1)
    return cos, sin


def _retention_mixer(xc, xl, mc, ml, ng, w_in, log_gamma, w_o, ctx_out):
    b, t, _ = xl.shape
    win, wo = w_in.astype(BF16), w_o.astype(BF16)
    qc, kc, vc, gc = _rt_proj_call(xc, mc, ng, win, None)
    ql, kl, vl, gl = _rt_proj_call(xl, ml, ng, win, _rope_tables(t))
    s0 = jnp.zeros((2, b, RT_HEADS, RT_DK, RT_DV), F32)
    oc, sc = _rt_scan_call(log_gamma, qc, kc, vc, s0)
    ol, _ = _rt_scan_call(log_gamma, ql, kl, vl, sc)
    return (_rt_out_call(oc, gc, xc, mc, wo) if ctx_out else None), _rt_out_call(ol, gl, xl, ml, wo)


def _ml_proj_body(x_ref, xp_ref, xn_ref, mod_ref, ng_ref, win_ref, cv_ref, wqk_ref, wv_ref, wif_ref, bif_ref,
                  q_ref, k_ref, v_ref, xc_ref, z_ref, gt_ref, y_scr, xm_scr):
    i = pl.program_id(1)
    tm = x_ref.shape[0]
    g, shift, scale = ng_ref[1:2], mod_ref[3:4], mod_ref[4:5]
    y_scr[0:SUB] = _modnorm(xp_ref[...], g, shift, scale).astype(BF16)
    y_scr[SUB:SUB + tm] = _modnorm(x_ref[...], g, shift, scale).astype(BF16)
    y_scr[SUB + tm:] = _modnorm(xn_ref[...], g, shift, scale).astype(BF16)
    xm = jnp.dot(y_scr[...], win_ref[:, 0:ML_INNER], preferred_element_type=F32)
    row = _iota((tm + 2 * SUB, 1), 0)
    lo = jnp.where(i > 0, 0, SUB)
    hi = jnp.where(i < pl.num_programs(1) - 1, tm + 2 * SUB, tm + SUB)
    xm_scr[...] = jnp.where((row >= lo) & (row < hi), xm, 0.0)
    xm_mid = xm_scr[pl.ds(SUB, tm)]
    conv = (xm_scr[pl.ds(SUB - 1, tm)] * cv_ref[0:1] + xm_mid * cv_ref[1:2]
            + xm_scr[pl.ds(SUB + 1, tm)] * cv_ref[2:3] + cv_ref[3:4])
    xc = _silu(conv)
    xc_ref[...] = xc
    z_ref[...] = jnp.dot(y_scr[SUB:SUB + tm], win_ref[:, ML_INNER:], preferred_element_type=F32)
    qs, ks, vs = [], [], []
    for t in range(ML_TILES):
        sl = slice(t * LANE, (t + 1) * LANE)
        qk = _mm(xc[:, sl], wqk_ref[t])
        qs.append(qk[:, :LANE])
        ks.append(qk[:, LANE:])
        vs.append(_mm(xm_mid[:, sl], wv_ref[t]))
    q = jnp.concatenate(qs, axis=-1)
    k = jnp.concatenate(ks, axis=-1)
    v = jnp.concatenate(vs, axis=-1)
    pre = (_mm(q, wif_ref[0:ML_INNER]) + _mm(k, wif_ref[ML_INNER:2 * ML_INNER])
           + _mm(v, wif_ref[2 * ML_INNER:]) + bif_ref[...])
    is_forget = jnp.bitwise_and(_iota((1, LANE), 1), 2 * ML_HEADS - 1) >= ML_HEADS
    log_sig = jnp.minimum(pre, 0.0) - jnp.log(1.0 + jnp.exp(-jnp.abs(pre)))
    gt_ref[...] = jnp.where(is_forget, log_sig, pre)
    q_ref[...] = q.astype(BF16)
    k_ref[...] = (k * ML_DH ** -0.5).astype(BF16)
    v_ref[...] = v.astype(BF16)


def _ml_proj_call(x, mod, ng, w):
    b, t, _ = x.shape
    tm = min(ROW_TILE, t)
    wide = jax.ShapeDtypeStruct((b, t, ML_INNER), BF16)
    wide32 = jax.ShapeDtypeStruct((b, t, ML_INNER), F32)
    return pl.pallas_call(
        _ml_proj_body,
        grid=(b, t // tm),
        in_specs=_row_specs(t, tm) + [_MOD_SPEC, _full((3, D)), _full((D, 2 * ML_INNER)), _full((SUB, ML_INNER)),
                                      _full((ML_TILES, LANE, 2 * LANE)), _full((ML_TILES, LANE, LANE)),
                                      _full((3 * ML_INNER, LANE)), _full((1, LANE))],
        out_specs=[_tile(ML_INNER, tm)] * 5 + [_tile(LANE, tm)],
        out_shape=[wide, wide, wide, wide32, wide32, jax.ShapeDtypeStruct((b, t, LANE), F32)],
        scratch_shapes=[pltpu.VMEM((tm + 2 * SUB, D), BF16), pltpu.VMEM((tm + 2 * SUB, ML_INNER), F32)],
        compiler_params=_params("parallel", "parallel"),
        name="mlstm_proj",
    )(x, x, x, mod, ng, w["win"], w["cv"], w["wqk"], w["wv"], w["wif"], w["bif"])


def _ml_scan_body(q_ref, k_ref, v_ref, gt_ref, c0_ref, nm0_ref, h_ref, c_ref, nm_ref):
    d = pl.program_id(0)
    c = pl.program_id(2)
    n = q_ref.shape[0]

    @pl.when(c == 0)
    def _():
        c_ref[...] = c0_ref[...]
        nm_ref[...] = nm0_ref[...]

    sgn = 1 - 2 * d
    valid = (_iota((n, n), 0) - _iota((n, n), 1)) * sgn >= 0
    gt = gt_ref[...]
    cum = jnp.dot(valid.astype(F32), gt, precision=HI, preferred_element_type=F32)
    tot = jnp.sum(gt, axis=0, keepdims=True)
    gt_t, cum_t = gt.T, cum.T
    back = d == 1
    for h in range(ML_HEADS):
        li, lf = h, ML_HEADS + h
        ri, rf = 2 * ML_HEADS + h, 3 * ML_HEADS + h
        ig_col = jnp.where(back, gt[:, ri:ri + 1], gt[:, li:li + 1])
        ig_row = jnp.where(back, gt_t[ri:ri + 1], gt_t[li:li + 1])
        b_col = jnp.where(back, cum[:, rf:rf + 1], cum[:, lf:lf + 1])
        b_row = jnp.where(back, cum_t[rf:rf + 1], cum_t[lf:lf + 1])
        b_tot = jnp.where(back, tot[:, rf:rf + 1], tot[:, lf:lf + 1])
        m_prev = nm_ref[ML_HEADS + h:ML_HEADS + h + 1, 0:1]
        n_prev = nm_ref[h:h + 1, :]
        logd = jnp.where(valid, b_col - b_row + ig_row, -jnp.inf)
        m_inter = b_col + m_prev
        m_t = jnp.maximum(m_inter, jnp.max(logd, axis=-1, keepdims=True))
        sl = slice(h * ML_DH, (h + 1) * ML_DH)
        qh, kh, vh = q_ref[:, sl], k_ref[:, sl], v_ref[:, sl]
        s = lax.dot_general(qh, kh, NT, preferred_element_type=F32) * jnp.exp(logd - m_t)
        inter = jnp.exp(m_inter - m_t)
        ct = c_ref[h]
        num = _mm(s, vh) + inter * _mm(qh, ct.astype(BF16))
        den = (jnp.sum(s, axis=-1, keepdims=True)
               + inter * jnp.sum(qh.astype(F32) * n_prev, axis=-1, keepdims=True))
        h_ref[:, sl] = num / jnp.maximum(jnp.abs(den), jnp.exp(-m_t))
        gj = b_tot - b_col + ig_col
        m_end = b_tot + m_prev
        m_new = jnp.maximum(m_end, jnp.max(gj, axis=0, keepdims=True))
        carry = jnp.exp(m_end - m_new)
        wk = jnp.exp(gj - m_new) * kh.astype(F32)
        c_ref[h] = carry * ct + lax.dot_general(wk.astype(BF16), vh, TN, preferred_element_type=F32)
        nm_ref[h:h + 1, :] = carry * n_prev + jnp.sum(wk, axis=0, keepdims=True)
        nm_ref[ML_HEADS + h:ML_HEADS + h + 1, :] = jnp.broadcast_to(m_new, (1, ML_DH))


def _ml_scan_call(q, k, v, gt, c0, nm0):
    b, t, _ = q.shape
    n = ML_CHUNK
    nc = t // n

    def seq(width):
        return pl.BlockSpec((None, n, width), lambda d, bi, c: (bi, _chunk_index(d, c, nc), 0))

    cst = pl.BlockSpec((None, None, ML_HEADS, ML_DH, ML_DH), lambda d, bi, c: (d, bi, 0, 0, 0))
    nst = pl.BlockSpec((None, None, 2 * ML_HEADS, ML_DH), lambda d, bi, c: (d, bi, 0, 0))
    return pl.pallas_call(
        _ml_scan_body,
        grid=(2, b, nc),
        in_specs=[seq(ML_INNER), seq(ML_INNER), seq(ML_INNER), seq(LANE), cst, nst],
        out_specs=[pl.BlockSpec((None, None, n, ML_INNER), lambda d, bi, c: (d, bi, _chunk_index(d, c, nc), 0)),
                   cst, nst],
        out_shape=[jax.ShapeDtypeStruct((2, b, t, ML_INNER), F32), jax.ShapeDtypeStruct(c0.shape, F32),
                   jax.ShapeDtypeStruct(nm0.shape, F32)],
        compiler_params=_params("parallel", "parallel", "arbitrary"),
        name="mlstm_scan",
    )(q, k, v, gt, c0, nm0)


def _ml_out_body(h_ref, xc_ref, z_ref, x_ref, mod_ref, vec_ref, wo_ref, o_ref):
    y = h_ref[0] + h_ref[1]
    parts = []
    for h in range(ML_HEADS):
        yh = y[:, h * ML_DH:(h + 1) * ML_DH]
        parts.append(_rms(yh - jnp.mean(yh, axis=-1, keepdims=True), ML_NORM_EPS))
    yn = jnp.concatenate(parts, axis=-1) * vec_ref[0:1]
    o = _mm((yn + vec_ref[1:2] * xc_ref[...]) * _silu(z_ref[...]), wo_ref[...])
    o_ref[...] = x_ref[...] + mod_ref[5:6] * o


def _ml_out_call(hh, xc, z, x, mod, w):
    b, t, _ = x.shape
    tm = min(ROW_TILE, t)
    return pl.pallas_call(
        _ml_out_body,
        grid=(b, t // tm),
        in_specs=[_tile2(ML_INNER, tm), _tile(ML_INNER, tm), _tile(ML_INNER, tm), _tile(D, tm), _MOD_SPEC,
                  _full((SUB, ML_INNER)), _full((ML_INNER, D))],
        out_specs=_tile(D, tm),
        out_shape=jax.ShapeDtypeStruct((b, t, D), F32),
        compiler_params=_params("parallel", "parallel"),
        name="mlstm_readout",
    )(hh, xc, z, x, mod, w["ovec"], w["wo"])


def _mlstm_weights(w_in, conv_w, conv_b, w_q, w_k, w_v, w_if, b_if, skip, norm_g, w_o):
    per_tile = LANE // ML_BLOCK
    eye = jnp.eye(per_tile, dtype=F32)

    def block_diag(w):
        wt = w.reshape(ML_TILES, per_tile, ML_BLOCK, ML_BLOCK)
        return jnp.einsum('tgio,gh->tgiho', wt, eye).reshape(ML_TILES, LANE, LANE)

    wqk = jnp.concatenate([block_diag(w_q), block_diag(w_k)], axis=-1).astype(BF16)
    n_gate = 2 * 2 * ML_HEADS
    wif = jnp.transpose(w_if, (1, 0, 2)).reshape(3 * ML_INNER, n_gate)
    wif = jnp.pad(wif, ((0, 0), (0, LANE - n_gate))).astype(BF16)
    bif = jnp.pad(b_if.reshape(1, n_gate), ((0, 0), (0, LANE - n_gate)))
    cv = jnp.concatenate([conv_w, conv_b[None], jnp.zeros((SUB - 4, ML_INNER), F32)], axis=0)
    ovec = jnp.concatenate([norm_g[None], skip[None], jnp.zeros((SUB - 2, ML_INNER), F32)], axis=0)
    return {"win": w_in.astype(BF16), "cv": cv, "wqk": wqk, "wv": block_diag(w_v).astype(BF16), "wif": wif,
            "bif": bif, "ovec": ovec, "wo": w_o.astype(BF16)}


def _mlstm_mixer(xc, xl, mc, ml, ng, w, ctx_out):
    b = xl.shape[0]
    pc = _ml_proj_call(xc, mc, ng, w)
    pl_ = _ml_proj_call(xl, ml, ng, w)
    c0 = jnp.zeros((2, b, ML_HEADS, ML_DH, ML_DH), F32)
    nm0 = jnp.zeros((2, b, 2 * ML_HEADS, ML_DH), F32)

    def scan(p, cs, nms):
        q, k, v, _, _, gt = p
        return _ml_scan_call(q, k, v, gt, cs, nms)

    def out(hh, p, x, mod):
        return _ml_out_call(hh, p[3], p[4], x, mod, w)

    hc, cc, nmc = scan(pc, c0, nm0)
    hl, _, _ = scan(pl_, cc, nmc)
    return (out(hc, pc, xc, mc) if ctx_out else None), out(hl, pl_, xl, ml)


def kernel(x, c, ctx, c_ctx, mod_w, mod_b, norm_g, ffn_w13, ffn_w2, final_g, rw_mu, rw_wrkv, rw_w0, rw_w1, rw_w2, rw_a0, rw_a1, rw_a2, rw_g1, rw_g2, rw_kk, rw_ka, rw_rk, rw_lnx_g, rw_lnx_b, rw_wo, rt_win, rt_log_gamma, rt_wo, ml_win, ml_conv_w, ml_conv_b, ml_wq, ml_wk, ml_wv, ml_wif, ml_bif, ml_skip, ml_norm_g, ml_wo):
    b = x.shape[0]
    depth = mod_w.shape[0]
    assert b < SUB
    cvec = jnp.concatenate([c, c_ctx[None], jnp.zeros((SUB - b - 1, D), F32)], axis=0)
    mods = _mod_call(cvec, mod_w, mod_b)
    xl, xc = x, ctx
    for i in range(depth):
        kind, s = i % 3, i // 3
        last = i == depth - 1
        ml = mods[i, :b].reshape(b, N_MOD, D)
        mc = jnp.broadcast_to(mods[i, b].reshape(1, N_MOD, D), (b, N_MOD, D))
        ng = norm_g[i]
        w13 = ffn_w13[i].astype(BF16)
        w2 = ffn_w2[i].astype(BF16)
        xl = _ffn_call(xl, ml, ng, w13[0], w2[0], 0)
        xc = _ffn_call(xc, mc, ng, w13[0], w2[0], 0)
        if kind == 0:
            w = _rwkv_weights(rw_mu[s], rw_wrkv[s], rw_w0[s], rw_w1[s], rw_w2[s], rw_a0[s], rw_a1[s], rw_a2[s],
                              rw_g1[s], rw_g2[s], rw_kk[s], rw_ka[s], rw_rk[s], rw_lnx_g[s], rw_lnx_b[s], rw_wo[s])
            xc, xl = _rwkv_mixer(xc, xl, mc, ml, ng, w, not last)
        elif kind == 1:
            xc, xl = _retention_mixer(xc, xl, mc, ml, ng, rt_win[s], rt_log_gamma[s], rt_wo[s], not last)
        else:
            w = _mlstm_weights(ml_win[s], ml_conv_w[s], ml_conv_b[s], ml_wq[s], ml_wk[s], ml_wv[s], ml_wif[s],
                               ml_bif[s], ml_skip[s], ml_norm_g[s], ml_wo[s])
            xc, xl = _mlstm_mixer(xc, xl, mc, ml, ng, w, not last)
        xl = _ffn_call(xl, ml, ng, w13[1], w2[1], 2, final_g if last else None)
        if not last:
            xc = _ffn_call(xc, mc, ng, w13[1], w2[1], 2)
    return xl
```

```python
import functools
import math

import jax
import jax.numpy as jnp
from jax import lax
from jax.experimental import pallas as pl
from jax.experimental.pallas import tpu as pltpu

F32 = jnp.float32
BF16 = jnp.bfloat16
HI = lax.Precision.HIGHEST

LANE = 128
SUB = 8
VMEM_LIMIT = 56 * 1024 * 1024

D = 1024
D_FF = 2816
N_MOD = 9
NORM_EPS = 1e-6
GRID_W = 64
RW_HEAD = 64
RW_LNX_EPS = 64e-5
RW_CHUNK = 64
RW_PAIRS = D // LANE
RT_HEADS = 4
RT_DK = D // RT_HEADS
RT_DV = 2 * D // RT_HEADS
ROPE_BASE = 10000.0
RT_CHUNK = 128
ML_INNER = 2 * D
ML_HEADS = 4
ML_DH = ML_INNER // ML_HEADS
ML_BLOCK = 4
ML_NORM_EPS = 1e-5
ML_CHUNK = 128
ML_TILES = ML_INNER // LANE

ROW_TILE = 256
FFN_ROW_TILE = 512
FFN_COL_TILE = D_FF // 2

NT = (((1,), (1,)), ((), ()))
TN = (((0,), (0,)), ((), ()))


def _params(*sem):
    return pltpu.CompilerParams(dimension_semantics=sem, vmem_limit_bytes=VMEM_LIMIT)


def _sigmoid(x):
    return 1.0 / (1.0 + jnp.exp(-x))


def _silu(x):
    return x * _sigmoid(x)


def _rms(x, eps):
    return x * lax.rsqrt(jnp.mean(x * x, axis=-1, keepdims=True) + eps)


def _modnorm(x, g, shift, scale):
    return _rms(x, NORM_EPS) * g * (1.0 + scale) + shift


def _mm(a, b):
    return jnp.dot(a.astype(BF16), b, preferred_element_type=F32)


def _iota(shape, dim):
    return lax.broadcasted_iota(jnp.int32, shape, dim)


def _mod_body(c_ref, w_ref, b_ref, o_ref):
    s = _silu(c_ref[...])
    o_ref[...] = jnp.dot(s, w_ref[...], precision=HI, preferred_element_type=F32) + b_ref[...]


def _mod_call(cvec, mod_w, mod_b):
    depth, _, n = mod_w.shape
    tn = D
    return pl.pallas_call(
        _mod_body,
        grid=(depth, n // tn),
        in_specs=[pl.BlockSpec((SUB, D), lambda l, j: (0, 0)),
                  pl.BlockSpec((None, D, tn), lambda l, j: (l, 0, j)),
                  pl.BlockSpec((None, 1, tn), lambda l, j: (l, 0, j))],
        out_specs=pl.BlockSpec((None, SUB, tn), lambda l, j: (l, 0, j)),
        out_shape=jax.ShapeDtypeStruct((depth, SUB, n), F32),
        compiler_params=_params("parallel", "parallel"),
        name="adaln_mod",
    )(cvec, mod_w, mod_b.reshape(depth, 1, n))


def _ffn_body(x_ref, mod_ref, ng_ref, w1_ref, w3_ref, w2_ref, *rest, sub, final):
    if final:
        fg_ref, o_ref, y_scr, acc_scr = rest
    else:
        o_ref, y_scr, acc_scr = rest
    j = pl.program_id(2)

    @pl.when(j == 0)
    def _():
        y = _modnorm(x_ref[...], ng_ref[sub:sub + 1], mod_ref[3 * sub:3 * sub + 1],
                     mod_ref[3 * sub + 1:3 * sub + 2])
        y_scr[...] = y.astype(BF16)
        acc_scr[...] = jnp.zeros_like(acc_scr)

    y = y_scr[...]
    a = jnp.dot(y, w1_ref[...], preferred_element_type=F32)
    b = jnp.dot(y, w3_ref[...], preferred_element_type=F32)
    acc_scr[...] += _mm(_silu(a) * b, w2_ref[...])

    @pl.when(j == pl.num_programs(2) - 1)
    def _():
        out = x_ref[...] + 0.5 * mod_ref[3 * sub + 2:3 * sub + 3] * acc_scr[...]
        if final:
            out = _rms(out, NORM_EPS) * fg_ref[...]
        o_ref[...] = out


def _ffn_call(x, mod, ng, w13, w2, sub, final_g=None):
    b, t, _ = x.shape
    tm = min(FFN_ROW_TILE, t)
    nf = D_FF // FFN_COL_TILE
    in_specs = [pl.BlockSpec((None, tm, D), lambda bi, i, j: (bi, i, 0)),
                pl.BlockSpec((None, N_MOD, D), lambda bi, i, j: (bi, 0, 0)),
                pl.BlockSpec((3, D), lambda bi, i, j: (0, 0)),
                pl.BlockSpec((D, FFN_COL_TILE), lambda bi, i, j: (0, j)),
                pl.BlockSpec((D, FFN_COL_TILE), lambda bi, i, j: (0, j + nf)),
                pl.BlockSpec((FFN_COL_TILE, D), lambda bi, i, j: (j, 0))]
    args = [x, mod, ng, w13, w13, w2]
    if final_g is not None:
        in_specs.append(pl.BlockSpec((1, D), lambda bi, i, j: (0, 0)))
        args.append(final_g.reshape(1, D))
    return pl.pallas_call(
        functools.partial(_ffn_body, sub=sub, final=final_g is not None),
        grid=(b, t // tm, nf),
        in_specs=in_specs,
        out_specs=pl.BlockSpec((None, tm, D), lambda bi, i, j: (bi, i, 0)),
        out_shape=jax.ShapeDtypeStruct((b, t, D), F32),
        scratch_shapes=[pltpu.VMEM((tm, D), BF16), pltpu.VMEM((tm, D), F32)],
        compiler_params=_params("parallel", "parallel", "arbitrary"),
        name="half_ffn",
    )(*args)


def _row_specs(t, tm):
    nb = tm // SUB
    last = t // SUB - 1
    return [pl.BlockSpec((None, tm, D), lambda bi, i: (bi, i, 0)),
            pl.BlockSpec((None, SUB, D), lambda bi, i: (bi, jnp.maximum(i * nb - 1, 0), 0)),
            pl.BlockSpec((None, SUB, D), lambda bi, i: (bi, jnp.minimum((i + 1) * nb, last), 0))]


def _full(shape):
    nd = len(shape)
    return pl.BlockSpec(shape, lambda bi, i: (0,) * nd)


def _tile(width, tm):
    return pl.BlockSpec((None, tm, width), lambda bi, i: (bi, i, 0))


def _tile2(width, tm):
    return pl.BlockSpec((2, None, tm, width), lambda bi, i: (0, bi, i, 0))


_MOD_SPEC = pl.BlockSpec((None, N_MOD, D), lambda bi, i: (bi, 0, 0))


def _chunk_index(d, c, nc):
    return c + d * (nc - 1 - 2 * c)


def _group_ones():
    r = jnp.right_shift(_iota((LANE, LANE), 0), 6)
    c = jnp.right_shift(_iota((LANE, LANE), 1), 6)
    return (r == c).astype(BF16)


def _head_sum64(x, ones):
    parts = [_mm(x[:, t * LANE:(t + 1) * LANE], ones) for t in range(x.shape[1] // LANE)]
    return jnp.concatenate(parts, axis=-1)


def _rw_proj_body(x_ref, xp_ref, xn_ref, mod_ref, ng_ref, vec_ref, wrkv_ref, wd_ref, w2_ref, a2_ref, g2_ref,
                  r_ref, v_ref, g_ref, kk_ref, lw_ref, kd_ref, b_ref):
    i = pl.program_id(1)
    tm = x_ref.shape[0]
    g, shift, scale = ng_ref[1:2], mod_ref[3:4], mod_ref[4:5]
    h = _modnorm(x_ref[...], g, shift, scale)
    hp = jnp.where(i > 0, _modnorm(xp_ref[SUB - 1:SUB], g, shift, scale), 0.0)
    hn = jnp.where(i < pl.num_programs(1) - 1, _modnorm(xn_ref[0:1], g, shift, scale), 0.0)
    row = _iota((tm, 1), 0)
    h_prev = jnp.where(row == 0, hp, pltpu.roll(h, 1, axis=0))
    h_next = jnp.where(row == tm - 1, hn, pltpu.roll(h, tm - 1, axis=0))
    xx = 0.5 * (h_prev + h_next) - h

    def mix(n):
        return (h + xx * vec_ref[n:n + 1]).astype(BF16)

    r = jnp.dot(mix(0), wrkv_ref[0], preferred_element_type=F32)
    k = jnp.dot(mix(2), wrkv_ref[1], preferred_element_type=F32)
    v = jnp.dot(mix(3), wrkv_ref[2], preferred_element_type=F32)
    wl = _mm(jnp.tanh(jnp.dot(mix(1), wd_ref[0], preferred_element_type=F32)), w2_ref[...])
    al = _mm(jnp.dot(mix(4), wd_ref[1], preferred_element_type=F32), a2_ref[...])
    gg = _mm(_sigmoid(jnp.dot(mix(5), wd_ref[2], preferred_element_type=F32)), g2_ref[...])

    kkr = k * vec_ref[10:11]
    n2 = _head_sum64(kkr * kkr, _group_ones())
    kk = kkr / jnp.maximum(jnp.sqrt(n2), 1e-12)
    for d in range(2):
        lw_ref[d] = -math.exp(-0.5) * _sigmoid(vec_ref[6 + d:7 + d] + wl[:, d * D:(d + 1) * D])
        a = _sigmoid(vec_ref[8 + d:9 + d] + al[:, d * D:(d + 1) * D])
        kd_ref[d] = k * (1.0 + (a - 1.0) * vec_ref[11:12])
        b_ref[d] = kk * a
    r_ref[...] = r
    v_ref[...] = v
    g_ref[...] = gg
    kk_ref[...] = kk


def _rw_proj_call(x, mod, ng, w):
    b, t, _ = x.shape
    tm = min(ROW_TILE, t)
    one = jax.ShapeDtypeStruct((b, t, D), F32)
    two = jax.ShapeDtypeStruct((2, b, t, D), F32)
    return pl.pallas_call(
        _rw_proj_body,
        grid=(b, t // tm),
        in_specs=_row_specs(t, tm) + [_MOD_SPEC, _full((3, D)), _full((16, D)), _full((3, D, D)),
                                      _full((3, D, LANE)), _full((LANE, 2 * D)), _full((LANE, 2 * D)),
                                      _full((LANE, D))],
        out_specs=[_tile(D, tm)] * 4 + [_tile2(D, tm)] * 3,
        out_shape=[one] * 4 + [two] * 3,
        compiler_params=_params("parallel", "parallel"),
        name="rwkv_proj",
    )(x, x, x, mod, ng, w["vec"], w["wrkv"], w["wd"], w["w2"], w["a2"], w["g2"])


def _rw_scan_body(r_ref, v_ref, kk_ref, lw_ref, kd_ref, b_ref, s0_ref, y_ref, s_ref):
    d = pl.program_id(0)
    c = pl.program_id(2)
    n = r_ref.shape[0]
    n2 = 2 * n

    @pl.when(c == 0)
    def _():
        s_ref[...] = s0_ref[...]

    sgn = 1 - 2 * d
    lw = lw_ref[...]
    before = ((_iota((n, n), 0) - _iota((n, n), 1)) * sgn >= 0).astype(F32)
    cum = jnp.dot(before, lw, precision=HI, preferred_element_type=F32)
    tot = jnp.sum(lw, axis=0, keepdims=True)
    kd, bb, v = kd_ref[...], b_ref[...], v_ref[...]
    rt = r_ref[...] * jnp.exp(cum)
    at = -kk_ref[...] * jnp.exp(cum - lw)
    e_out = jnp.exp(-cum)
    kt, bt = kd * e_out, bb * e_out
    e_tail = jnp.exp(tot - cum)
    kh, bh = kd * e_tail, bb * e_tail

    head0 = _iota((1, LANE), 1) < RW_HEAD
    ri = jnp.bitwise_and(_iota((n2, n2), 0), n - 1)
    cj = jnp.bitwise_and(_iota((n2, n2), 1), n - 1)
    rel = (ri - cj) * sgn
    strict, incl = rel > 0, rel >= 0
    eye = (_iota((n2, n2), 0) == _iota((n2, n2), 1)).astype(F32)

    def expand32(m):
        return jnp.concatenate([jnp.where(head0, m, 0.0), jnp.where(head0, 0.0, m)], axis=0)

    def expand(m):
        return expand32(m).astype(BF16)

    pairs = range(RW_PAIRS)
    sls = [slice(p * LANE, (p + 1) * LANE) for p in pairs]
    ea = [expand(at[:, sl]) for sl in sls]
    er = [expand(rt[:, sl]) for sl in sls]
    ev = [expand(v[:, sl]) for sl in sls]
    prod = [lax.dot_general(jnp.concatenate([ea[p], er[p]], axis=0),
                            jnp.concatenate([expand(bt[:, sls[p]]), expand(kt[:, sls[p]])], axis=0),
                            NT, preferred_element_type=F32) for p in pairs]
    a_ab = [jnp.where(strict, prod[p][:n2, :n2], 0.0) for p in pairs]
    a_ak = [jnp.where(strict, prod[p][:n2, n2:], 0.0).astype(BF16) for p in pairs]
    a_rbk = [jnp.concatenate([jnp.where(incl, prod[p][n2:, :n2], 0.0),
                              jnp.where(incl, prod[p][n2:, n2:], 0.0)], axis=1).astype(BF16) for p in pairs]
    inv = [eye + a_ab[p] for p in pairs]
    pw = [a_ab[p].astype(BF16) for p in pairs]
    pw = [jnp.dot(pw[p], pw[p], preferred_element_type=F32).astype(BF16) for p in pairs]
    st = [s_ref[p] for p in pairs]
    sb = [st[p].astype(BF16) for p in pairs]
    z = [jnp.dot(ea[p], sb[p], preferred_element_type=F32) + jnp.dot(a_ak[p], ev[p], preferred_element_type=F32)
         for p in pairs]
    for _ in range(int(math.log2(n)) - 2):
        both = [jnp.dot(jnp.concatenate([pw[p], inv[p].astype(BF16)], axis=0), pw[p], preferred_element_type=F32)
                for p in pairs]
        inv = [inv[p] + both[p][n2:] for p in pairs]
        pw = [both[p][:n2].astype(BF16) for p in pairs]
    inv = [inv[p] + _mm(inv[p], pw[p]) for p in pairs]
    bk_t = [jnp.concatenate([expand32(bh[:, sl]).T, expand32(kh[:, sl]).T], axis=1).astype(BF16) for sl in sls]
    keep = [jnp.exp(jnp.broadcast_to(tot[:, sl], (LANE, LANE)).T) for sl in sls]
    u = [_mm(inv[p], z[p].astype(BF16)) for p in pairs]
    uv = [jnp.concatenate([u[p].astype(BF16), ev[p]], axis=0) for p in pairs]
    for p in pairs:
        s_ref[p] = st[p] * keep[p] + jnp.dot(bk_t[p], uv[p], preferred_element_type=F32)
    ye = [jnp.dot(er[p], sb[p], preferred_element_type=F32) + jnp.dot(a_rbk[p], uv[p], preferred_element_type=F32)
          for p in pairs]
    y_ref[...] = jnp.concatenate([ye[p][:n] + ye[p][n:] for p in pairs], axis=-1)


def _rw_scan_call(r, v, kk, lw, kd, bb, s0):
    b, t, _ = r.shape
    n = RW_CHUNK
    nc = t // n
    seq = pl.BlockSpec((None, n, D), lambda d, bi, c: (bi, _chunk_index(d, c, nc), 0))
    seq2 = pl.BlockSpec((None, None, n, D), lambda d, bi, c: (d, bi, _chunk_index(d, c, nc), 0))
    st = pl.BlockSpec((None, None, RW_PAIRS, LANE, LANE), lambda d, bi, c: (d, bi, 0, 0, 0))
    return pl.pallas_call(
        _rw_scan_body,
        grid=(2, b, nc),
        in_specs=[seq, seq, seq, seq2, seq2, seq2, st],
        out_specs=[seq2, st],
        out_shape=[jax.ShapeDtypeStruct((2, b, t, D), F32), jax.ShapeDtypeStruct(s0.shape, F32)],
        compiler_params=_params("parallel", "parallel", "arbitrary"),
        name="rwkv_scan",
    )(r, v, kk, lw, kd, bb, s0)


def _rw_out_body(y_ref, r_ref, v_ref, g_ref, kd_ref, x_ref, mod_ref, vec_ref, wo_ref, o_ref):
    ones = _group_ones()
    y = y_ref[0] + y_ref[1]
    yc = y - _head_sum64(y, ones) * (1.0 / RW_HEAD)
    var = _head_sum64(yc * yc, ones) * (1.0 / RW_HEAD)
    yh = yc * lax.rsqrt(var + RW_LNX_EPS) * vec_ref[0:1] + vec_ref[1:2]
    bonus = _head_sum64(r_ref[...] * (kd_ref[0] + kd_ref[1]) * vec_ref[2:3], ones) * v_ref[...]
    o = _mm((yh + bonus) * g_ref[...], wo_ref[...])
    o_ref[...] = x_ref[...] + mod_ref[5:6] * o


def _rw_out_call(y, r, v, g, kd, x, mod, w):
    b, t, _ = x.shape
    tm = min(ROW_TILE, t)
    return pl.pallas_call(
        _rw_out_body,
        grid=(b, t // tm),
        in_specs=[_tile2(D, tm), _tile(D, tm), _tile(D, tm), _tile(D, tm), _tile2(D, tm), _tile(D, tm),
                  _MOD_SPEC, _full((SUB, D)), _full((D, D))],
        out_specs=_tile(D, tm),
        out_shape=jax.ShapeDtypeStruct((b, t, D), F32),
        compiler_params=_params("parallel", "parallel"),
        name="rwkv_readout",
    )(y, r, v, g, kd, x, mod, w["ovec"], w["wo"])


def _rwkv_mixer(xc, xl, mc, ml, ng, w, ctx_out):
    b = xl.shape[0]
    pc = _rw_proj_call(xc, mc, ng, w)
    pl_ = _rw_proj_call(xl, ml, ng, w)
    s0 = jnp.zeros((2, b, RW_PAIRS, LANE, LANE), F32)

    def scan(p, s):
        r, v, _, kk, lw, kd, bb = p
        return _rw_scan_call(r, v, kk, lw, kd, bb, s)

    def out(y, p, x, mod):
        r, v, g, _, _, kd, _ = p
        return _rw_out_call(y, r, v, g, kd, x, mod, w)

    yc, sc = scan(pc, s0)
    yl, _ = scan(pl_, sc)
    return (out(yc, pc, xc, mc) if ctx_out else None), out(yl, pl_, xl, ml)


def _rwkv_weights(mu, w_rkv, w0, w1, w2, a0, a1, a2, g1, g2, k_k, k_a, r_k, lnx_g, lnx_b, w_o):
    zeros = jnp.zeros((RW_HEAD, D), F32)

    def two_dir(m):
        return jnp.concatenate([jnp.concatenate([m[0], zeros], axis=1),
                                jnp.concatenate([zeros, m[1]], axis=1)], axis=0).astype(BF16)

    vec = jnp.concatenate([mu, w0, a0, k_k[None], k_a[None], jnp.zeros((4, D), F32)], axis=0)
    ovec = jnp.concatenate([lnx_g[None], lnx_b[None], r_k.reshape(1, D), jnp.zeros((5, D), F32)], axis=0)
    wd = jnp.stack([jnp.concatenate([w1[0], w1[1]], axis=1), jnp.concatenate([a1[0], a1[1]], axis=1), g1])
    return {"vec": vec, "ovec": ovec, "wrkv": w_rkv.astype(BF16), "wd": wd.astype(BF16), "w2": two_dir(w2),
            "a2": two_dir(a2), "g2": g2.astype(BF16), "wo": w_o.astype(BF16)}


def _rt_proj_body(x_ref, mod_ref, ng_ref, win_ref, *rest, rope):
    if rope:
        cos_ref, sin_ref, q_ref, k_ref, v_ref, g_ref = rest
    else:
        q_ref, k_ref, v_ref, g_ref = rest
    h = _modnorm(x_ref[...], ng_ref[1:2], mod_ref[3:4], mod_ref[4:5]).astype(BF16)
    q = jnp.dot(h, win_ref[:, 0:D], preferred_element_type=F32)
    k = jnp.dot(h, win_ref[:, D:2 * D], preferred_element_type=F32) * RT_DK ** -0.5
    if rope:
        cos = jnp.concatenate([cos_ref[...]] * RT_HEADS, axis=-1)
        sin = jnp.concatenate([sin_ref[...]] * RT_HEADS, axis=-1)
        low = jnp.bitwise_and(_iota((1, D), 1), LANE - 1) < LANE // 2

        def rot(t):
            partner = jnp.where(low, pltpu.roll(t, D - LANE // 2, axis=1), pltpu.roll(t, LANE // 2, axis=1))
            return t * cos + partner * sin

        q, k = rot(q), rot(k)
    q_ref[...] = q.astype(BF16)
    k_ref[...] = k.astype(BF16)
    v_ref[...] = jnp.dot(h, win_ref[:, 2 * D:4 * D], preferred_element_type=F32).astype(BF16)
    g_ref[...] = jnp.dot(h, win_ref[:, 4 * D:6 * D], preferred_element_type=F32)


def _rt_proj_call(x, mod, ng, win, tables):
    b, t, _ = x.shape
    tm = min(ROW_TILE, t)
    in_specs = [_tile(D, tm), _MOD_SPEC, _full((3, D)), _full((D, 6 * D))]
    args = [x, mod, ng, win]
    if tables is not None:
        in_specs += [pl.BlockSpec((tm, RT_DK), lambda bi, i: (i, 0))] * 2
        args += list(tables)
    return pl.pallas_call(
        functools.partial(_rt_proj_body, rope=tables is not None),
        grid=(b, t // tm),
        in_specs=in_specs,
        out_specs=[_tile(D, tm), _tile(D, tm), _tile(2 * D, tm), _tile(2 * D, tm)],
        out_shape=[jax.ShapeDtypeStruct((b, t, D), BF16), jax.ShapeDtypeStruct((b, t, D), BF16),
                   jax.ShapeDtypeStruct((b, t, 2 * D), BF16), jax.ShapeDtypeStruct((b, t, 2 * D), F32)],
        compiler_params=_params("parallel", "parallel"),
        name="retention_proj",
    )(*args)


def _rt_scan_body(lg_ref, q_ref, k_ref, v_ref, s0_ref, o_ref, s_ref):
    d = pl.program_id(0)
    c = pl.program_id(2)
    n = q_ref.shape[0]

    @pl.when(c == 0)
    def _():
        s_ref[...] = s0_ref[...]

    sgn = 1 - 2 * d
    rel = (_iota((n, n), 0) - _iota((n, n), 1)) * sgn
    dist = jnp.maximum(rel, 0).astype(F32)
    pos = (_iota((n, 1), 0) * sgn + d * (n - 1)).astype(F32)
    for h in range(RT_HEADS):
        lg = -jnp.abs(lg_ref[d, h])
        intra = jnp.where(rel >= 0, jnp.exp(dist * lg), 0.0)
        q_dec = jnp.exp((pos + 1.0) * lg)
        k_dec = jnp.exp((n - 1.0 - pos) * lg)
        c_dec = jnp.exp(jnp.full((1, 1), float(n), F32) * lg)
        qh = q_ref[:, h * RT_DK:(h + 1) * RT_DK]
        kh = k_ref[:, h * RT_DK:(h + 1) * RT_DK]
        vh = v_ref[:, h * RT_DV:(h + 1) * RT_DV]
        s = lax.dot_general(qh, kh, NT, preferred_element_type=F32) * intra
        st = s_ref[h]
        o_ref[:, h * RT_DV:(h + 1) * RT_DV] = _mm(s, vh) + q_dec * _mm(qh, st.astype(BF16))
        kv = lax.dot_general(kh, (k_dec * vh.astype(F32)).astype(BF16), TN, preferred_element_type=F32)
        s_ref[h] = c_dec * st + kv


def _rt_scan_call(lg, q, k, v, s0):
    b, t, _ = q.shape
    n = RT_CHUNK
    nc = t // n

    def seq(width):
        return pl.BlockSpec((None, n, width), lambda d, bi, c: (bi, _chunk_index(d, c, nc), 0))

    st = pl.BlockSpec((None, None, RT_HEADS, RT_DK, RT_DV), lambda d, bi, c: (d, bi, 0, 0, 0))
    return pl.pallas_call(
        _rt_scan_body,
        grid=(2, b, nc),
        in_specs=[pl.BlockSpec(memory_space=pltpu.SMEM), seq(D), seq(D), seq(2 * D), st],
        out_specs=[pl.BlockSpec((None, None, n, 2 * D), lambda d, bi, c: (d, bi, _chunk_index(d, c, nc), 0)), st],
        out_shape=[jax.ShapeDtypeStruct((2, b, t, 2 * D), F32), jax.ShapeDtypeStruct(s0.shape, F32)],
        compiler_params=_params("parallel", "parallel", "arbitrary"),
        name="retention_scan",
    )(lg, q, k, v, s0)


def _rt_out_body(o_ref, g_ref, x_ref, mod_ref, wo_ref, out_ref):
    o = o_ref[0] + o_ref[1]
    on = jnp.concatenate([_rms(o[:, h * RT_DV:(h + 1) * RT_DV], NORM_EPS) for h in range(RT_HEADS)], axis=-1)
    out_ref[...] = x_ref[...] + mod_ref[5:6] * _mm(_silu(g_ref[...]) * on, wo_ref[...])


def _rt_out_call(o, g, x, mod, wo):
    b, t, _ = x.shape
    tm = min(ROW_TILE, t)
    return pl.pallas_call(
        _rt_out_body,
        grid=(b, t // tm),
        in_specs=[_tile2(2 * D, tm), _tile(2 * D, tm), _tile(D, tm), _MOD_SPEC, _full((2 * D, D))],
        out_specs=_tile(D, tm),
        out_shape=jax.ShapeDtypeStruct((b, t, D), F32),
        compiler_params=_params("parallel", "parallel"),
        name="retention_readout",
    )(o, g, x, mod, wo)


def _rope_tables(t):
    pos = jnp.arange(t, dtype=jnp.int32)
    quarter = RT_DK // 4
    freqs = ROPE_BASE ** (-jnp.arange(quarter, dtype=F32) / quarter)
    ar = (pos // GRID_W).astype(F32)[:, None] * freqs
    ac = (pos % GRID_W).astype(F32)[:, None] * freqs
    cos = jnp.concatenate([jnp.cos(ar), jnp.cos(ar), jnp.cos(ac), jnp.cos(ac)], axis=-1)
    sin = jnp.concatenate([-jnp.sin(ar), jnp.sin(ar), -jnp.sin(ac), jnp.sin(ac)], axis=-1)
    return cos, sin


def _retention_mixer(xc, xl, mc, ml, ng, w_in, log_gamma, w_o, ctx_out):
    b, t, _ = xl.shape
    win, wo = w_in.astype(BF16), w_o.astype(BF16)
    qc, kc, vc, gc = _rt_proj_call(xc, mc, ng, win, None)
    ql, kl, vl, gl = _rt_proj_call(xl, ml, ng, win, _rope_tables(t))
    s0 = jnp.zeros((2, b, RT_HEADS, RT_DK, RT_DV), F32)
    oc, sc = _rt_scan_call(log_gamma, qc, kc, vc, s0)
    ol, _ = _rt_scan_call(log_gamma, ql, kl, vl, sc)
    return (_rt_out_call(oc, gc, xc, mc, wo) if ctx_out else None), _rt_out_call(ol, gl, xl, ml, wo)


def _ml_proj_body(x_ref, xp_ref, xn_ref, mod_ref, ng_ref, win_ref, cv_ref, wqk_ref, wv_ref, wif_ref, bif_ref,
                  q_ref, k_ref, v_ref, xc_ref, z_ref, gt_ref, y_scr, xm_scr):
    i = pl.program_id(1)
    tm = x_ref.shape[0]
    g, shift, scale = ng_ref[1:2], mod_ref[3:4], mod_ref[4:5]
    y_scr[0:SUB] = _modnorm(xp_ref[...], g, shift, scale).astype(BF16)
    y_scr[SUB:SUB + tm] = _modnorm(x_ref[...], g, shift, scale).astype(BF16)
    y_scr[SUB + tm:] = _modnorm(xn_ref[...], g, shift, scale).astype(BF16)
    xm = jnp.dot(y_scr[...], win_ref[:, 0:ML_INNER], preferred_element_type=F32)
    row = _iota((tm + 2 * SUB, 1), 0)
    lo = jnp.where(i > 0, 0, SUB)
    hi = jnp.where(i < pl.num_programs(1) - 1, tm + 2 * SUB, tm + SUB)
    xm_scr[...] = jnp.where((row >= lo) & (row < hi), xm, 0.0)
    xm_mid = xm_scr[pl.ds(SUB, tm)]
    conv = (xm_scr[pl.ds(SUB - 1, tm)] * cv_ref[0:1] + xm_mid * cv_ref[1:2]
            + xm_scr[pl.ds(SUB + 1, tm)] * cv_ref[2:3] + cv_ref[3:4])
    xc = _silu(conv)
    xc_ref[...] = xc
    z_ref[...] = jnp.dot(y_scr[SUB:SUB + tm], win_ref[:, ML_INNER:], preferred_element_type=F32)
    qs, ks, vs = [], [], []
    for t in range(ML_TILES):
        sl = slice(t * LANE, (t + 1) * LANE)
        qk = _mm(xc[:, sl], wqk_ref[t])
        qs.append(qk[:, :LANE])
        ks.append(qk[:, LANE:])
        vs.append(_mm(xm_mid[:, sl], wv_ref[t]))
    q = jnp.concatenate(qs, axis=-1)
    k = jnp.concatenate(ks, axis=-1)
    v = jnp.concatenate(vs, axis=-1)
    pre = (_mm(q, wif_ref[0:ML_INNER]) + _mm(k, wif_ref[ML_INNER:2 * ML_INNER])
           + _mm(v, wif_ref[2 * ML_INNER:]) + bif_ref[...])
    is_forget = jnp.bitwise_and(_iota((1, LANE), 1), 2 * ML_HEADS - 1) >= ML_HEADS
    log_sig = jnp.minimum(pre, 0.0) - jnp.log(1.0 + jnp.exp(-jnp.abs(pre)))
    gt_ref[...] = jnp.where(is_forget, log_sig, pre)
    q_ref[...] = q.astype(BF16)
    k_ref[...] = (k * ML_DH ** -0.5).astype(BF16)
    v_ref[...] = v.astype(BF16)


def _ml_proj_call(x, mod, ng, w):
    b, t, _ = x.shape
    tm = min(ROW_TILE, t)
    wide = jax.ShapeDtypeStruct((b, t, ML_INNER), BF16)
    wide32 = jax.ShapeDtypeStruct((b, t, ML_INNER), F32)
    return pl.pallas_call(
        _ml_proj_body,
        grid=(b, t // tm),
        in_specs=_row_specs(t, tm) + [_MOD_SPEC, _full((3, D)), _full((D, 2 * ML_INNER)), _full((SUB, ML_INNER)),
                                      _full((ML_TILES, LANE, 2 * LANE)), _full((ML_TILES, LANE, LANE)),
                                      _full((3 * ML_INNER, LANE)), _full((1, LANE))],
        out_specs=[_tile(ML_INNER, tm)] * 5 + [_tile(LANE, tm)],
        out_shape=[wide, wide, wide, wide32, wide32, jax.ShapeDtypeStruct((b, t, LANE), F32)],
        scratch_shapes=[pltpu.VMEM((tm + 2 * SUB, D), BF16), pltpu.VMEM((tm + 2 * SUB, ML_INNER), F32)],
        compiler_params=_params("parallel", "parallel"),
        name="mlstm_proj",
    )(x, x, x, mod, ng, w["win"], w["cv"], w["wqk"], w["wv"], w["wif"], w["bif"])


def _ml_scan_body(q_ref, k_ref, v_ref, gt_ref, c0_ref, nm0_ref, h_ref, c_ref, nm_ref):
    d = pl.program_id(0)
    c = pl.program_id(2)
    n = q_ref.shape[0]

    @pl.when(c == 0)
    def _():
        c_ref[...] = c0_ref[...]
        nm_ref[...] = nm0_ref[...]

    sgn = 1 - 2 * d
    valid = (_iota((n, n), 0) - _iota((n, n), 1)) * sgn >= 0
    gt = gt_ref[...]
    cum = jnp.dot(valid.astype(F32), gt, precision=HI, preferred_element_type=F32)
    tot = jnp.sum(gt, axis=0, keepdims=True)
    gt_t, cum_t = gt.T, cum.T
    back = d == 1
    for h in range(ML_HEADS):
        li, lf = h, ML_HEADS + h
        ri, rf = 2 * ML_HEADS + h, 3 * ML_HEADS + h
        ig_col = jnp.where(back, gt[:, ri:ri + 1], gt[:, li:li + 1])
        ig_row = jnp.where(back, gt_t[ri:ri + 1], gt_t[li:li + 1])
        b_col = jnp.where(back, cum[:, rf:rf + 1], cum[:, lf:lf + 1])
        b_row = jnp.where(back, cum_t[rf:rf + 1], cum_t[lf:lf + 1])
        b_tot = jnp.where(back, tot[:, rf:rf + 1], tot[:, lf:lf + 1])
        m_prev = nm_ref[ML_HEADS + h:ML_HEADS + h + 1, 0:1]
        n_prev = nm_ref[h:h + 1, :]
        logd = jnp.where(valid, b_col - b_row + ig_row, -jnp.inf)
        m_inter = b_col + m_prev
        m_t = jnp.maximum(m_inter, jnp.max(logd, axis=-1, keepdims=True))
        sl = slice(h * ML_DH, (h + 1) * ML_DH)
        qh, kh, vh = q_ref[:, sl], k_ref[:, sl], v_ref[:, sl]
        s = lax.dot_general(qh, kh, NT, preferred_element_type=F32) * jnp.exp(logd - m_t)
        inter = jnp.exp(m_inter - m_t)
        ct = c_ref[h]
        num = _mm(s, vh) + inter * _mm(qh, ct.astype(BF16))
        den = (jnp.sum(s, axis=-1, keepdims=True)
               + inter * jnp.sum(qh.astype(F32) * n_prev, axis=-1, keepdims=True))
        h_ref[:, sl] = num / jnp.maximum(jnp.abs(den), jnp.exp(-m_t))
        gj = b_tot - b_col + ig_col
        m_end = b_tot + m_prev
        m_new = jnp.maximum(m_end, jnp.max(gj, axis=0, keepdims=True))
        carry = jnp.exp(m_end - m_new)
        wk = jnp.exp(gj - m_new) * kh.astype(F32)
        c_ref[h] = carry * ct + lax.dot_general(wk.astype(BF16), vh, TN, preferred_element_type=F32)
        nm_ref[h:h + 1, :] = carry * n_prev + jnp.sum(wk, axis=0, keepdims=True)
        nm_ref[ML_HEADS + h:ML_HEADS + h + 1, :] = jnp.broadcast_to(m_new, (1, ML_DH))


def _ml_scan_call(q, k, v, gt, c0, nm0):
    b, t, _ = q.shape
    n = ML_CHUNK
    nc = t // n

    def seq(width):
        return pl.BlockSpec((None, n, width), lambda d, bi, c: (bi, _chunk_index(d, c, nc), 0))

    cst = pl.BlockSpec((None, None, ML_HEADS, ML_DH, ML_DH), lambda d, bi, c: (d, bi, 0, 0, 0))
    nst = pl.BlockSpec((None, None, 2 * ML_HEADS, ML_DH), lambda d, bi, c: (d, bi, 0, 0))
    return pl.pallas_call(
        _ml_scan_body,
        grid=(2, b, nc),
        in_specs=[seq(ML_INNER), seq(ML_INNER), seq(ML_INNER), seq(LANE), cst, nst],
        out_specs=[pl.BlockSpec((None, None, n, ML_INNER), lambda d, bi, c: (d, bi, _chunk_index(d, c, nc), 0)),
                   cst, nst],
        out_shape=[jax.ShapeDtypeStruct((2, b, t, ML_INNER), F32), jax.ShapeDtypeStruct(c0.shape, F32),
                   jax.ShapeDtypeStruct(nm0.shape, F32)],
        compiler_params=_params("parallel", "parallel", "arbitrary"),
        name="mlstm_scan",
    )(q, k, v, gt, c0, nm0)


def _ml_out_body(h_ref, xc_ref, z_ref, x_ref, mod_ref, vec_ref, wo_ref, o_ref):
    y = h_ref[0] + h_ref[1]
    parts = []
    for h in range(ML_HEADS):
        yh = y[:, h * ML_DH:(h + 1) * ML_DH]
        parts.append(_rms(yh - jnp.mean(yh, axis=-1, keepdims=True), ML_NORM_EPS))
    yn = jnp.concatenate(parts, axis=-1) * vec_ref[0:1]
    o = _mm((yn + vec_ref[1:2] * xc_ref[...]) * _silu(z_ref[...]), wo_ref[...])
    o_ref[...] = x_ref[...] + mod_ref[5:6] * o


def _ml_out_call(hh, xc, z, x, mod, w):
    b, t, _ = x.shape
    tm = min(ROW_TILE, t)
    return pl.pallas_call(
        _ml_out_body,
        grid=(b, t // tm),
        in_specs=[_tile2(ML_INNER, tm), _tile(ML_INNER, tm), _tile(ML_INNER, tm), _tile(D, tm), _MOD_SPEC,
                  _full((SUB, ML_INNER)), _full((ML_INNER, D))],
        out_specs=_tile(D, tm),
        out_shape=jax.ShapeDtypeStruct((b, t, D), F32),
        compiler_params=_params("parallel", "parallel"),
        name="mlstm_readout",
    )(hh, xc, z, x, mod, w["ovec"], w["wo"])


def _mlstm_weights(w_in, conv_w, conv_b, w_q, w_k, w_v, w_if, b_if, skip, norm_g, w_o):
    per_tile = LANE // ML_BLOCK
    eye = jnp.eye(per_tile, dtype=F32)

    def block_diag(w):
        wt = w.reshape(ML_TILES, per_tile, ML_BLOCK, ML_BLOCK)
        return jnp.einsum('tgio,gh->tgiho', wt, eye).reshape(ML_TILES, LANE, LANE)

    wqk = jnp.concatenate([block_diag(w_q), block_diag(w_k)], axis=-1).astype(BF16)
    n_gate = 2 * 2 * ML_HEADS
    wif = jnp.transpose(w_if, (1, 0, 2)).reshape(3 * ML_INNER, n_gate)
    wif = jnp.pad(wif, ((0, 0), (0, LANE - n_gate))).astype(BF16)
    bif = jnp.pad(b_if.reshape(1, n_gate), ((0, 0), (0, LANE - n_gate)))
    cv = jnp.concatenate([conv_w, conv_b[None], jnp.zeros((SUB - 4, ML_INNER), F32)], axis=0)
    ovec = jnp.concatenate([norm_g[None], skip[None], jnp.zeros((SUB - 2, ML_INNER), F32)], axis=0)
    return {"win": w_in.astype(BF16), "cv": cv, "wqk": wqk, "wv": block_diag(w_v).astype(BF16), "wif": wif,
            "bif": bif, "ovec": ovec, "wo": w_o.astype(BF16)}


def _mlstm_mixer(xc, xl, mc, ml, ng, w, ctx_out):
    b = xl.shape[0]
    pc = _ml_proj_call(xc, mc, ng, w)
    pl_ = _ml_proj_call(xl, ml, ng, w)
    c0 = jnp.zeros((2, b, ML_HEADS, ML_DH, ML_DH), F32)
    nm0 = jnp.zeros((2, b, 2 * ML_HEADS, ML_DH), F32)

    def scan(p, cs, nms):
        q, k, v, _, _, gt = p
        return _ml_scan_call(q, k, v, gt, cs, nms)

    def out(hh, p, x, mod):
        return _ml_out_call(hh, p[3], p[4], x, mod, w)

    hc, cc, nmc = scan(pc, c0, nm0)
    hl, _, _ = scan(pl_, cc, nmc)
    return (out(hc, pc, xc, mc) if ctx_out else None), out(hl, pl_, xl, ml)


def kernel(x, c, ctx, c_ctx, mod_w, mod_b, norm_g, ffn_w13, ffn_w2, final_g, rw_mu, rw_wrkv, rw_w0, rw_w1, rw_w2, rw_a0, rw_a1, rw_a2, rw_g1, rw_g2, rw_kk, rw_ka, rw_rk, rw_lnx_g, rw_lnx_b, rw_wo, rt_win, rt_log_gamma, rt_wo, ml_win, ml_conv_w, ml_conv_b, ml_wq, ml_wk, ml_wv, ml_wif, ml_bif, ml_skip, ml_norm_g, ml_wo):
    b = x.shape[0]
    depth = mod_w.shape[0]
    assert b < SUB
    cvec = jnp.concatenate([c, c_ctx[None], jnp.zeros((SUB - b - 1, D), F32)], axis=0)
    mods = _mod_call(cvec, mod_w, mod_b)
    xl, xc = x, ctx
    for i in range(depth):
        kind, s = i % 3, i // 3
        last = i == depth - 1
        ml = mods[i, :b].reshape(b, N_MOD, D)
        mc = jnp.broadcast_to(mods[i, b].reshape(1, N_MOD, D), (b, N_MOD, D))
        ng = norm_g[i]
        w13 = ffn_w13[i].astype(BF16)
        w2 = ffn_w2[i].astype(BF16)
        xl = _ffn_call(xl, ml, ng, w13[0], w2[0], 0)
        xc = _ffn_call(xc, mc, ng, w13[0], w2[0], 0)
        if kind == 0:
            w = _rwkv_weights(rw_mu[s], rw_wrkv[s], rw_w0[s], rw_w1[s], rw_w2[s], rw_a0[s], rw_a1[s], rw_a2[s],
                              rw_g1[s], rw_g2[s], rw_kk[s], rw_ka[s], rw_rk[s], rw_lnx_g[s], rw_lnx_b[s], rw_wo[s])
            xc, xl = _rwkv_mixer(xc, xl, mc, ml, ng, w, not last)
        elif kind == 1:
            xc, xl = _retention_mixer(xc, xl, mc, ml, ng, rt_win[s], rt_log_gamma[s], rt_wo[s], not last)
        else:
            w = _mlstm_weights(ml_win[s], ml_conv_w[s], ml_conv_b[s], ml_wq[s], ml_wk[s], ml_wv[s], ml_wif[s],
                               ml_bif[s], ml_skip[s], ml_norm_g[s], ml_wo[s])
            xc, xl = _mlstm_mixer(xc, xl, mc, ml, ng, w, not last)
        xl = _ffn_call(xl, ml, ng, w13[1], w2[1], 2, final_g if last else None)
        if not last:
            xc = _ffn_call(xc, mc, ng, w13[1], w2[1], 2)
    return xl
```

```python
import functools
import math

import jax
import jax.numpy as jnp
from jax import lax
from jax.experimental import pallas as pl
from jax.experimental.pallas import tpu as pltpu

F32 = jnp.float32
BF16 = jnp.bfloat16
HI = lax.Precision.HIGHEST

LANE = 128
SUB = 8
VMEM_LIMIT = 56 * 1024 * 1024

D = 1024
D_FF = 2816
N_MOD = 9
NORM_EPS = 1e-6
GRID_W = 64
RW_HEAD = 64
RW_LNX_EPS = 64e-5
RW_CHUNK = 64
RW_CHUNKS_PER_STEP = 2
RW_GROUP = 4
RW_GROUPS = D // (RW_GROUP * RW_HEAD)
RW_STATE = (RW_GROUPS, RW_GROUP * RW_HEAD, RW_GROUP * RW_HEAD)
RT_HEADS = 4
RT_DK = D // RT_HEADS
RT_DV = 2 * D // RT_HEADS
ROPE_BASE = 10000.0
RT_CHUNK = 256
ML_INNER = 2 * D
ML_HEADS = 4
ML_DH = ML_INNER // ML_HEADS
ML_BLOCK = 4
ML_NORM_EPS = 1e-5
ML_CHUNK = 256
ML_TILES = ML_INNER // LANE

ROW_TILE = 256
FFN_ROW_TILE = 512
FFN_COL_TILE = D_FF // 2

NT = (((1,), (1,)), ((), ()))
TN = (((0,), (0,)), ((), ()))


def _params(*sem):
    return pltpu.CompilerParams(dimension_semantics=sem, vmem_limit_bytes=VMEM_LIMIT)


def _sigmoid(x):
    return 1.0 / (1.0 + jnp.exp(-x))


def _silu(x):
    return x * _sigmoid(x)


def _rms(x, eps):
    return x * lax.rsqrt(jnp.mean(x * x, axis=-1, keepdims=True) + eps)


def _modnorm(x, g, shift, scale):
    return _rms(x, NORM_EPS) * g * (1.0 + scale) + shift


def _mm(a, b):
    return jnp.dot(a.astype(BF16), b, preferred_element_type=F32)


def _iota(shape, dim):
    return lax.broadcasted_iota(jnp.int32, shape, dim)


def _mod_body(c_ref, w_ref, b_ref, o_ref):
    s = _silu(c_ref[...])
    o_ref[...] = jnp.dot(s, w_ref[...], precision=HI, preferred_element_type=F32) + b_ref[...]


def _mod_call(cvec, mod_w, mod_b):
    depth, _, n = mod_w.shape
    tn = D
    return pl.pallas_call(
        _mod_body,
        grid=(depth, n // tn),
        in_specs=[pl.BlockSpec((SUB, D), lambda l, j: (0, 0)),
                  pl.BlockSpec((None, D, tn), lambda l, j: (l, 0, j)),
                  pl.BlockSpec((None, 1, tn), lambda l, j: (l, 0, j))],
        out_specs=pl.BlockSpec((None, SUB, tn), lambda l, j: (l, 0, j)),
        out_shape=jax.ShapeDtypeStruct((depth, SUB, n), F32),
        compiler_params=_params("parallel", "parallel"),
        name="adaln_mod",
    )(cvec, mod_w, mod_b.reshape(depth, 1, n))


def _ffn_body(x_ref, mod_ref, ng_ref, w1_ref, w3_ref, w2_ref, *rest, sub, final):
    if final:
        fg_ref, o_ref, y_scr, acc_scr = rest
    else:
        o_ref, y_scr, acc_scr = rest
    j = pl.program_id(2)

    @pl.when(j == 0)
    def _():
        y = _modnorm(x_ref[...], ng_ref[sub:sub + 1], mod_ref[3 * sub:3 * sub + 1],
                     mod_ref[3 * sub + 1:3 * sub + 2])
        y_scr[...] = y.astype(BF16)
        acc_scr[...] = jnp.zeros_like(acc_scr)

    y = y_scr[...]
    a = jnp.dot(y, w1_ref[...], preferred_element_type=F32)
    b = jnp.dot(y, w3_ref[...], preferred_element_type=F32)
    acc_scr[...] += _mm(_silu(a) * b, w2_ref[...])

    @pl.when(j == pl.num_programs(2) - 1)
    def _():
        out = x_ref[...] + 0.5 * mod_ref[3 * sub + 2:3 * sub + 3] * acc_scr[...]
        if final:
            out = _rms(out, NORM_EPS) * fg_ref[...]
        o_ref[...] = out


def _ffn_call(x, mod, ng, w13, w2, sub, final_g=None):
    b, t, _ = x.shape
    tm = min(FFN_ROW_TILE, t)
    nf = D_FF // FFN_COL_TILE
    in_specs = [pl.BlockSpec((None, tm, D), lambda bi, i, j: (bi, i, 0)),
                pl.BlockSpec((None, N_MOD, D), lambda bi, i, j: (bi, 0, 0)),
                pl.BlockSpec((3, D), lambda bi, i, j: (0, 0)),
                pl.BlockSpec((D, FFN_COL_TILE), lambda bi, i, j: (0, j)),
                pl.BlockSpec((D, FFN_COL_TILE), lambda bi, i, j: (0, j + nf)),
                pl.BlockSpec((FFN_COL_TILE, D), lambda bi, i, j: (j, 0))]
    args = [x, mod, ng, w13, w13, w2]
    if final_g is not None:
        in_specs.append(pl.BlockSpec((1, D), lambda bi, i, j: (0, 0)))
        args.append(final_g.reshape(1, D))
    return pl.pallas_call(
        functools.partial(_ffn_body, sub=sub, final=final_g is not None),
        grid=(b, t // tm, nf),
        in_specs=in_specs,
        out_specs=pl.BlockSpec((None, tm, D), lambda bi, i, j: (bi, i, 0)),
        out_shape=jax.ShapeDtypeStruct((b, t, D), F32),
        scratch_shapes=[pltpu.VMEM((tm, D), BF16), pltpu.VMEM((tm, D), F32)],
        compiler_params=_params("parallel", "parallel", "arbitrary"),
        name="half_ffn",
    )(*args)


def _row_specs(t, tm):
    nb = tm // SUB
    last = t // SUB - 1
    return [pl.BlockSpec((None, tm, D), lambda bi, i: (bi, i, 0)),
            pl.BlockSpec((None, SUB, D), lambda bi, i: (bi, jnp.maximum(i * nb - 1, 0), 0)),
            pl.BlockSpec((None, SUB, D), lambda bi, i: (bi, jnp.minimum((i + 1) * nb, last), 0))]


def _full(shape):
    nd = len(shape)
    return pl.BlockSpec(shape, lambda bi, i: (0,) * nd)


def _tile(width, tm):
    return pl.BlockSpec((None, tm, width), lambda bi, i: (bi, i, 0))


def _tile2(width, tm):
    return pl.BlockSpec((2, None, tm, width), lambda bi, i: (0, bi, i, 0))


_MOD_SPEC = pl.BlockSpec((None, N_MOD, D), lambda bi, i: (bi, 0, 0))


def _chunk_index(d, c, nc):
    return c + d * (nc - 1 - 2 * c)


def _group_ones():
    r = jnp.right_shift(_iota((LANE, LANE), 0), 6)
    c = jnp.right_shift(_iota((LANE, LANE), 1), 6)
    return (r == c).astype(BF16)


def _head_sum64(x, ones):
    parts = [_mm(x[:, t * LANE:(t + 1) * LANE], ones) for t in range(x.shape[1] // LANE)]
    return jnp.concatenate(parts, axis=-1)


def _rw_proj_body(x_ref, xp_ref, xn_ref, mod_ref, ng_ref, vec_ref, wrkv_ref, wd_ref, w2_ref, a2_ref, g2_ref,
                  r_ref, v_ref, g_ref, kk_ref, lw_ref, kd_ref, b_ref):
    i = pl.program_id(1)
    tm = x_ref.shape[0]
    g, shift, scale = ng_ref[1:2], mod_ref[3:4], mod_ref[4:5]
    h = _modnorm(x_ref[...], g, shift, scale)
    hp = jnp.where(i > 0, _modnorm(xp_ref[SUB - 1:SUB], g, shift, scale), 0.0)
    hn = jnp.where(i < pl.num_programs(1) - 1, _modnorm(xn_ref[0:1], g, shift, scale), 0.0)
    row = _iota((tm, 1), 0)
    h_prev = jnp.where(row == 0, hp, pltpu.roll(h, 1, axis=0))
    h_next = jnp.where(row == tm - 1, hn, pltpu.roll(h, tm - 1, axis=0))
    xx = 0.5 * (h_prev + h_next) - h

    def mix(n):
        return (h + xx * vec_ref[n:n + 1]).astype(BF16)

    r = jnp.dot(mix(0), wrkv_ref[0], preferred_element_type=F32)
    k = jnp.dot(mix(2), wrkv_ref[1], preferred_element_type=F32)
    v = jnp.dot(mix(3), wrkv_ref[2], preferred_element_type=F32)
    wl = _mm(jnp.tanh(jnp.dot(mix(1), wd_ref[0], preferred_element_type=F32)), w2_ref[...])
    al = _mm(jnp.dot(mix(4), wd_ref[1], preferred_element_type=F32), a2_ref[...])
    gg = _mm(_sigmoid(jnp.dot(mix(5), wd_ref[2], preferred_element_type=F32)), g2_ref[...])

    kkr = k * vec_ref[10:11]
    n2 = _head_sum64(kkr * kkr, _group_ones())
    kk = kkr / jnp.maximum(jnp.sqrt(n2), 1e-12)
    for d in range(2):
        lw_ref[d] = -math.exp(-0.5) * _sigmoid(vec_ref[6 + d:7 + d] + wl[:, d * D:(d + 1) * D])
        a = _sigmoid(vec_ref[8 + d:9 + d] + al[:, d * D:(d + 1) * D])
        kd_ref[d] = k * (1.0 + (a - 1.0) * vec_ref[11:12])
        b_ref[d] = kk * a
    r_ref[...] = r
    v_ref[...] = v
    g_ref[...] = gg
    kk_ref[...] = kk


def _rw_proj_call(x, mod, ng, w):
    b, t, _ = x.shape
    tm = min(ROW_TILE, t)
    one = jax.ShapeDtypeStruct((b, t, D), F32)
    two = jax.ShapeDtypeStruct((2, b, t, D), F32)
    return pl.pallas_call(
        _rw_proj_body,
        grid=(b, t // tm),
        in_specs=_row_specs(t, tm) + [_MOD_SPEC, _full((3, D)), _full((16, D)), _full((3, D, D)),
                                      _full((3, D, LANE)), _full((LANE, 2 * D)), _full((LANE, 2 * D)),
                                      _full((LANE, D))],
        out_specs=[_tile(D, tm)] * 4 + [_tile2(D, tm)] * 3,
        out_shape=[one] * 4 + [two] * 3,
        compiler_params=_params("parallel", "parallel"),
        name="rwkv_proj",
    )(x, x, x, mod, ng, w["vec"], w["wrkv"], w["wd"], w["w2"], w["a2"], w["g2"])


def _rw_scan_body(rf_ref, rb_ref, vf_ref, vb_ref, kkf_ref, kkb_ref, lwf_ref, lwb_ref, kdf_ref, kdb_ref,
                  bf_ref, bb_ref, s0_ref, yf_ref, yb_ref, s_ref):
    c = pl.program_id(1)
    n = RW_CHUNK
    subs = rf_ref.shape[0] // n
    refs = ((rf_ref, vf_ref, kkf_ref, lwf_ref, kdf_ref, bf_ref, yf_ref),
            (rb_ref, vb_ref, kkb_ref, lwb_ref, kdb_ref, bb_ref, yb_ref))

    @pl.when(c == 0)
    def _():
        s_ref[...] = s0_ref[...]

    gl = RW_GROUP * RW_HEAD
    shift = int(math.log2(RW_HEAD))
    tri = _iota((n, n), 0) - _iota((n, n), 1)
    rel = _iota((n, gl), 0) - jnp.bitwise_and(_iota((n, gl), 1), RW_HEAD - 1)
    before = [(tri >= 0).astype(BF16), (tri <= 0).astype(BF16)]
    strict = [rel > 0, rel < 0]
    incl = [rel >= 0, rel <= 0]
    eye = (rel == 0).astype(F32)
    row_head = jnp.right_shift(_iota((gl, 2 * gl), 0), shift)
    col_head = jnp.right_shift(jnp.bitwise_and(_iota((gl, 2 * gl), 1), gl - 1), shift)
    same2 = (row_head == col_head).astype(BF16)
    same = same2[:, :gl]
    same32 = same.astype(F32)
    low = _iota((1, LANE), 1) < RW_HEAD

    def bdiag(m):
        return jnp.concatenate([m] * RW_GROUP, axis=0) * same

    def bf16_parts(a):
        hi = a.astype(BF16)
        rest = a - hi.astype(F32)
        mid = rest.astype(BF16)
        return hi, mid, (rest - mid.astype(F32)).astype(BF16)

    groups = range(D // gl)
    sls = [slice(q * gl, (q + 1) * gl) for q in groups]
    dirs = range(2)
    chains = [(d, j, q) for j in range(subs) for d in dirs for q in groups]
    rows, ab, rb, vb, rhs, hk_t, keep = {}, {}, {}, {}, {}, {}, {}
    for j in range(subs):
        for d in dirs:
            r_ref, v_ref, kk_ref, lw_ref, kd_ref, b_ref, _ = refs[d]
            rows[d, j] = pl.ds((subs - 1 - j if d else j) * n, n)
            rw = rows[d, j]
            lw = lw_ref[rw, :]
            cum = sum(jnp.dot(before[d], part, preferred_element_type=F32) for part in bf16_parts(lw))
            tot = jnp.sum(lw, axis=0, keepdims=True)
            kd, bb = kd_ref[rw, :], b_ref[rw, :]
            rt = r_ref[rw, :] * jnp.exp(cum)
            at = -kk_ref[rw, :] * jnp.exp(cum - lw)
            e_out = jnp.exp(-cum)
            kt, bt = kd * e_out, bb * e_out
            e_tail = jnp.exp(tot - cum)
            kh, bh = kd * e_tail, bb * e_tail
            v = v_ref[rw, :]
            etot = jnp.exp(tot)
            for q in groups:
                sl = sls[q]
                t = jnp.concatenate([bt[:, sl], kt[:, sl]], axis=0).T
                sw = pltpu.roll(t, RW_HEAD, axis=1)
                b2 = jnp.where(low, t, sw).astype(BF16)
                k2 = jnp.where(low, sw, t).astype(BF16)
                rhs[d, j, q] = jnp.concatenate([b2, b2, k2, k2], axis=1) * same2
                ab[d, j, q] = at[:, sl].astype(BF16)
                rb[d, j, q] = rt[:, sl].astype(BF16)
                vb[d, j, q] = v[:, sl].astype(BF16)
                hk_t[d, j, q] = jnp.concatenate([bh[:, sl], kh[:, sl]], axis=0).T.astype(BF16)
                keep[d, j, q] = jnp.concatenate([jnp.broadcast_to(etot[:, sl], (LANE, gl)).T] * 2, axis=1)
    prod = {k: jnp.dot(jnp.concatenate([ab[k], rb[k]], axis=0), rhs[k], preferred_element_type=F32) for k in chains}
    a_ab = {k: jnp.where(strict[k[0]], prod[k][:n, :gl], 0.0) for k in chains}
    a_ak = {k: jnp.where(strict[k[0]], prod[k][:n, gl:], 0.0).astype(BF16) for k in chains}
    a_rb = {k: jnp.where(incl[k[0]], prod[k][n:, :gl], 0.0).astype(BF16) for k in chains}
    a_rk = {k: jnp.where(incl[k[0]], prod[k][n:, gl:], 0.0).astype(BF16) for k in chains}
    inv = {k: eye + a_ab[k] for k in chains}
    pw = {k: a_ab[k].astype(BF16) for k in chains}
    pw = {k: jnp.dot(pw[k], bdiag(pw[k]), preferred_element_type=F32).astype(BF16) for k in chains}
    bdv = {k: bdiag(vb[k]) for k in chains}
    for _ in range(int(math.log2(n)) - 2):
        both = {k: jnp.dot(jnp.concatenate([pw[k], inv[k].astype(BF16)], axis=0), bdiag(pw[k]),
                           preferred_element_type=F32) for k in chains}
        inv = {k: inv[k] + both[k][n:] for k in chains}
        pw = {k: both[k][:n].astype(BF16) for k in chains}
    inv = {k: (inv[k] + _mm(inv[k], bdiag(pw[k]))).astype(BF16) for k in chains}
    heads = [(d, q) for d in dirs for q in groups]
    st = {k: s_ref[k[0], k[1]] for k in heads}
    for j in range(subs):
        at_j = lambda tab, k: tab[k[0], j, k[1]]
        sb = {k: st[k].astype(BF16) for k in heads}
        z = {k: jnp.dot(at_j(ab, k), sb[k], preferred_element_type=F32)
             + jnp.dot(at_j(a_ak, k), at_j(bdv, k), preferred_element_type=F32) for k in heads}
        u = {k: jnp.dot(at_j(inv, k), bdiag(z[k].astype(BF16)), preferred_element_type=F32).astype(BF16)
             for k in heads}
        upd = {k: jnp.dot(at_j(hk_t, k), jnp.concatenate([u[k], at_j(vb, k)], axis=0), preferred_element_type=F32)
               for k in heads}
        y = {k: jnp.dot(at_j(rb, k), sb[k], preferred_element_type=F32)
             + jnp.dot(at_j(a_rb, k), bdiag(u[k]), preferred_element_type=F32)
             + jnp.dot(at_j(a_rk, k), at_j(bdv, k), preferred_element_type=F32) for k in heads}
        st = {k: st[k] * at_j(keep, k) + upd[k] * same32 for k in heads}
        for d in dirs:
            refs[d][-1][rows[d, j], :] = jnp.concatenate([y[d, q] for q in groups], axis=-1)
    for k in heads:
        s_ref[k[0], k[1]] = st[k]


def _rw_scan_call(r, v, kk, lw, kd, bb, s0):
    b, t, _ = r.shape
    n = RW_CHUNK * RW_CHUNKS_PER_STEP
    nc = t // n
    fwd = pl.BlockSpec((None, n, D), lambda bi, c: (bi, c, 0))
    bwd = pl.BlockSpec((None, n, D), lambda bi, c: (bi, nc - 1 - c, 0))
    fwd2 = pl.BlockSpec((None, None, n, D), lambda bi, c: (0, bi, c, 0))
    bwd2 = pl.BlockSpec((None, None, n, D), lambda bi, c: (1, bi, nc - 1 - c, 0))
    st = pl.BlockSpec((2, None) + RW_STATE, lambda bi, c: (0, bi, 0, 0, 0))
    seq = jax.ShapeDtypeStruct((b, t, D), F32)
    return pl.pallas_call(
        _rw_scan_body,
        grid=(b, nc),
        in_specs=[fwd, bwd, fwd, bwd, fwd, bwd, fwd2, bwd2, fwd2, bwd2, fwd2, bwd2, st],
        out_specs=[fwd, bwd, st],
        out_shape=[seq, seq, jax.ShapeDtypeStruct(s0.shape, F32)],
        compiler_params=_params("parallel", "arbitrary"),
        name="rwkv_scan",
    )(r, r, v, v, kk, kk, lw, lw, kd, kd, bb, bb, s0)


def _rw_out_body(yf_ref, yb_ref, r_ref, v_ref, g_ref, kd_ref, x_ref, mod_ref, vec_ref, wo_ref, o_ref):
    ones = _group_ones()
    y = yf_ref[...] + yb_ref[...]
    yc = y - _head_sum64(y, ones) * (1.0 / RW_HEAD)
    var = _head_sum64(yc * yc, ones) * (1.0 / RW_HEAD)
    yh = yc * lax.rsqrt(var + RW_LNX_EPS) * vec_ref[0:1] + vec_ref[1:2]
    bonus = _head_sum64(r_ref[...] * (kd_ref[0] + kd_ref[1]) * vec_ref[2:3], ones) * v_ref[...]
    o = _mm((yh + bonus) * g_ref[...], wo_ref[...])
    o_ref[...] = x_ref[...] + mod_ref[5:6] * o


def _rw_out_call(yf, yb, r, v, g, kd, x, mod, w):
    b, t, _ = x.shape
    tm = min(ROW_TILE, t)
    return pl.pallas_call(
        _rw_out_body,
        grid=(b, t // tm),
        in_specs=[_tile(D, tm)] * 5 + [_tile2(D, tm), _tile(D, tm), _MOD_SPEC, _full((SUB, D)), _full((D, D))],
        out_specs=_tile(D, tm),
        out_shape=jax.ShapeDtypeStruct((b, t, D), F32),
        compiler_params=_params("parallel", "parallel"),
        name="rwkv_readout",
    )(yf, yb, r, v, g, kd, x, mod, w["ovec"], w["wo"])


def _rwkv_mixer(xc, xl, mc, ml, ng, w, ctx_out):
    b = xl.shape[0]
    pc = _rw_proj_call(xc, mc, ng, w)
    pl_ = _rw_proj_call(xl, ml, ng, w)
    s0 = jnp.zeros((2, b) + RW_STATE, F32)

    def scan(p, s):
        r, v, _, kk, lw, kd, bb = p
        return _rw_scan_call(r, v, kk, lw, kd, bb, s)

    def out(yf, yb, p, x, mod):
        r, v, g, _, _, kd, _ = p
        return _rw_out_call(yf, yb, r, v, g, kd, x, mod, w)

    ycf, ycb, sc = scan(pc, s0)
    ylf, ylb, _ = scan(pl_, sc)
    return (out(ycf, ycb, pc, xc, mc) if ctx_out else None), out(ylf, ylb, pl_, xl, ml)


def _rwkv_weights(mu, w_rkv, w0, w1, w2, a0, a1, a2, g1, g2, k_k, k_a, r_k, lnx_g, lnx_b, w_o):
    zeros = jnp.zeros((RW_HEAD, D), F32)

    def two_dir(m):
        return jnp.concatenate([jnp.concatenate([m[0], zeros], axis=1),
                                jnp.concatenate([zeros, m[1]], axis=1)], axis=0).astype(BF16)

    vec = jnp.concatenate([mu, w0, a0, k_k[None], k_a[None], jnp.zeros((4, D), F32)], axis=0)
    ovec = jnp.concatenate([lnx_g[None], lnx_b[None], r_k.reshape(1, D), jnp.zeros((5, D), F32)], axis=0)
    wd = jnp.stack([jnp.concatenate([w1[0], w1[1]], axis=1), jnp.concatenate([a1[0], a1[1]], axis=1), g1])
    return {"vec": vec, "ovec": ovec, "wrkv": w_rkv.astype(BF16), "wd": wd.astype(BF16), "w2": two_dir(w2),
            "a2": two_dir(a2), "g2": g2.astype(BF16), "wo": w_o.astype(BF16)}


def _rt_proj_body(x_ref, mod_ref, ng_ref, win_ref, *rest, rope):
    if rope:
        cos_ref, sin_ref, q_ref, k_ref, v_ref, g_ref = rest
    else:
        q_ref, k_ref, v_ref, g_ref = rest
    h = _modnorm(x_ref[...], ng_ref[1:2], mod_ref[3:4], mod_ref[4:5]).astype(BF16)
    q = jnp.dot(h, win_ref[:, 0:D], preferred_element_type=F32)
    k = jnp.dot(h, win_ref[:, D:2 * D], preferred_element_type=F32) * RT_DK ** -0.5
    if rope:
        cos = jnp.concatenate([cos_ref[...]] * RT_HEADS, axis=-1)
        sin = jnp.concatenate([sin_ref[...]] * RT_HEADS, axis=-1)
        low = jnp.bitwise_and(_iota((1, D), 1), LANE - 1) < LANE // 2

        def rot(t):
            partner = jnp.where(low, pltpu.roll(t, D - LANE // 2, axis=1), pltpu.roll(t, LANE // 2, axis=1))
            return t * cos + partner * sin

        q, k = rot(q), rot(k)
    q_ref[...] = q.astype(BF16)
    k_ref[...] = k.astype(BF16)
    v_ref[...] = jnp.dot(h, win_ref[:, 2 * D:4 * D], preferred_element_type=F32).astype(BF16)
    g_ref[...] = jnp.dot(h, win_ref[:, 4 * D:6 * D], preferred_element_type=F32)


def _rt_proj_call(x, mod, ng, win, tables):
    b, t, _ = x.shape
    tm = min(ROW_TILE, t)
    in_specs = [_tile(D, tm), _MOD_SPEC, _full((3, D)), _full((D, 6 * D))]
    args = [x, mod, ng, win]
    if tables is not None:
        in_specs += [pl.BlockSpec((tm, RT_DK), lambda bi, i: (i, 0))] * 2
        args += list(tables)
    return pl.pallas_call(
        functools.partial(_rt_proj_body, rope=tables is not None),
        grid=(b, t // tm),
        in_specs=in_specs,
        out_specs=[_tile(D, tm), _tile(D, tm), _tile(2 * D, tm), _tile(2 * D, tm)],
        out_shape=[jax.ShapeDtypeStruct((b, t, D), BF16), jax.ShapeDtypeStruct((b, t, D), BF16),
                   jax.ShapeDtypeStruct((b, t, 2 * D), BF16), jax.ShapeDtypeStruct((b, t, 2 * D), F32)],
        compiler_params=_params("parallel", "parallel"),
        name="retention_proj",
    )(*args)


def _rt_scan_body(lg_ref, qf_ref, qb_ref, kf_ref, kb_ref, vf_ref, vb_ref, s0_ref, of_ref, ob_ref, s_ref):
    c = pl.program_id(1)
    n = qf_ref.shape[0]
    refs = ((qf_ref, kf_ref, vf_ref, of_ref), (qb_ref, kb_ref, vb_ref, ob_ref))

    @pl.when(c == 0)
    def _():
        s_ref[...] = s0_ref[...]

    tri = _iota((n, n), 0) - _iota((n, n), 1)
    row = _iota((n, 1), 0)
    rel = [tri, -tri]
    pos = [row.astype(F32), (n - 1 - row).astype(F32)]
    heads = [(d, h) for d in range(2) for h in range(RT_HEADS)]
    lg = {k: -jnp.abs(lg_ref[k[0], k[1]]) for k in heads}
    qs = {k: refs[k[0]][0][:, k[1] * RT_DK:(k[1] + 1) * RT_DK] for k in heads}
    ks = {k: refs[k[0]][1][:, k[1] * RT_DK:(k[1] + 1) * RT_DK] for k in heads}
    vs = {k: refs[k[0]][2][:, k[1] * RT_DV:(k[1] + 1) * RT_DV] for k in heads}
    st = {k: s_ref[k[0], k[1]] for k in heads}
    qk = {k: lax.dot_general(qs[k], ks[k], NT, preferred_element_type=F32) for k in heads}
    qr = {k: _mm(qs[k], st[k].astype(BF16)) for k in heads}
    kv = {k: lax.dot_general(ks[k], (jnp.exp((n - 1.0 - pos[k[0]]) * lg[k]) * vs[k].astype(F32)).astype(BF16), TN,
                             preferred_element_type=F32) for k in heads}
    for k in heads:
        s_ref[k[0], k[1]] = jnp.exp(jnp.full((1, 1), float(n), F32) * lg[k]) * st[k] + kv[k]
    for k in heads:
        d, h = k
        dist = jnp.maximum(rel[d], 0).astype(F32)
        s = qk[k] * jnp.where(rel[d] >= 0, jnp.exp(dist * lg[k]), 0.0)
        refs[d][3][:, h * RT_DV:(h + 1) * RT_DV] = _mm(s, vs[k]) + jnp.exp((pos[d] + 1.0) * lg[k]) * qr[k]


def _rt_scan_call(lg, q, k, v, s0):
    b, t, _ = q.shape
    n = RT_CHUNK
    nc = t // n

    def fwd(width):
        return pl.BlockSpec((None, n, width), lambda bi, c: (bi, c, 0))

    def bwd(width):
        return pl.BlockSpec((None, n, width), lambda bi, c: (bi, nc - 1 - c, 0))

    st = pl.BlockSpec((2, None, RT_HEADS, RT_DK, RT_DV), lambda bi, c: (0, bi, 0, 0, 0))
    seq = jax.ShapeDtypeStruct((b, t, 2 * D), F32)
    return pl.pallas_call(
        _rt_scan_body,
        grid=(b, nc),
        in_specs=[pl.BlockSpec(memory_space=pltpu.SMEM), fwd(D), bwd(D), fwd(D), bwd(D), fwd(2 * D), bwd(2 * D), st],
        out_specs=[fwd(2 * D), bwd(2 * D), st],
        out_shape=[seq, seq, jax.ShapeDtypeStruct(s0.shape, F32)],
        compiler_params=_params("parallel", "arbitrary"),
        name="retention_scan",
    )(lg, q, q, k, k, v, v, s0)


def _rt_out_body(of_ref, ob_ref, g_ref, x_ref, mod_ref, wo_ref, out_ref):
    o = of_ref[...] + ob_ref[...]
    on = jnp.concatenate([_rms(o[:, h * RT_DV:(h + 1) * RT_DV], NORM_EPS) for h in range(RT_HEADS)], axis=-1)
    out_ref[...] = x_ref[...] + mod_ref[5:6] * _mm(_silu(g_ref[...]) * on, wo_ref[...])


def _rt_out_call(of, ob, g, x, mod, wo):
    b, t, _ = x.shape
    tm = min(ROW_TILE, t)
    return pl.pallas_call(
        _rt_out_body,
        grid=(b, t // tm),
        in_specs=[_tile(2 * D, tm)] * 3 + [_tile(D, tm), _MOD_SPEC, _full((2 * D, D))],
        out_specs=_tile(D, tm),
        out_shape=jax.ShapeDtypeStruct((b, t, D), F32),
        compiler_params=_params("parallel", "parallel"),
        name="retention_readout",
    )(of, ob, g, x, mod, wo)


def _rope_tables(t):
    pos = jnp.arange(t, dtype=jnp.int32)
    quarter = RT_DK // 4
    freqs = ROPE_BASE ** (-jnp.arange(quarter, dtype=F32) / quarter)
    ar = (pos // GRID_W).astype(F32)[:, None] * freqs
    ac = (pos % GRID_W).astype(F32)[:, None] * freqs
    cos = jnp.concatenate([jnp.cos(ar), jnp.cos(ar), jnp.cos(ac), jnp.cos(ac)], axis=-1)
    sin = jnp.concatenate([-jnp.sin(ar), jnp.sin(ar), -jnp.sin(ac), jnp.sin(ac)], axis=-1)
    return cos, sin


def _retention_mixer(xc, xl, mc, ml, ng, w_in, log_gamma, w_o, ctx_out):
    b, t, _ = xl.shape
    win, wo = w_in.astype(BF16), w_o.astype(BF16)
    qc, kc, vc, gc = _rt_proj_call(xc, mc, ng, win, None)
    ql, kl, vl, gl = _rt_proj_call(xl, ml, ng, win, _rope_tables(t))
    s0 = jnp.zeros((2, b, RT_HEADS, RT_DK, RT_DV), F32)
    ocf, ocb, sc = _rt_scan_call(log_gamma, qc, kc, vc, s0)
    olf, olb, _ = _rt_scan_call(log_gamma, ql, kl, vl, sc)
    return ((_rt_out_call(ocf, ocb, gc, xc, mc, wo) if ctx_out else None),
            _rt_out_call(olf, olb, gl, xl, ml, wo))


def _ml_proj_body(x_ref, xp_ref, xn_ref, mod_ref, ng_ref, win_ref, cv_ref, wqk_ref, wv_ref, wif_ref, bif_ref,
                  q_ref, k_ref, v_ref, xc_ref, z_ref, gt_ref, y_scr, xm_scr):
    i = pl.program_id(1)
    tm = x_ref.shape[0]
    g, shift, scale = ng_ref[1:2], mod_ref[3:4], mod_ref[4:5]
    y_scr[0:SUB] = _modnorm(xp_ref[...], g, shift, scale).astype(BF16)
    y_scr[SUB:SUB + tm] = _modnorm(x_ref[...], g, shift, scale).astype(BF16)
    y_scr[SUB + tm:] = _modnorm(xn_ref[...], g, shift, scale).astype(BF16)
    xm = jnp.dot(y_scr[...], win_ref[:, 0:ML_INNER], preferred_element_type=F32)
    row = _iota((tm + 2 * SUB, 1), 0)
    lo = jnp.where(i > 0, 0, SUB)
    hi = jnp.where(i < pl.num_programs(1) - 1, tm + 2 * SUB, tm + SUB)
    xm_scr[...] = jnp.where((row >= lo) & (row < hi), xm, 0.0)
    xm_mid = xm_scr[pl.ds(SUB, tm)]
    conv = (xm_scr[pl.ds(SUB - 1, tm)] * cv_ref[0:1] + xm_mid * cv_ref[1:2]
            + xm_scr[pl.ds(SUB + 1, tm)] * cv_ref[2:3] + cv_ref[3:4])
    xc = _silu(conv)
    xc_ref[...] = xc
    z_ref[...] = jnp.dot(y_scr[SUB:SUB + tm], win_ref[:, ML_INNER:], preferred_element_type=F32)
    qs, ks, vs = [], [], []
    for t in range(ML_TILES):
        sl = slice(t * LANE, (t + 1) * LANE)
        qk = _mm(xc[:, sl], wqk_ref[t])
        qs.append(qk[:, :LANE])
        ks.append(qk[:, LANE:])
        vs.append(_mm(xm_mid[:, sl], wv_ref[t]))
    q = jnp.concatenate(qs, axis=-1)
    k = jnp.concatenate(ks, axis=-1)
    v = jnp.concatenate(vs, axis=-1)
    pre = (_mm(q, wif_ref[0:ML_INNER]) + _mm(k, wif_ref[ML_INNER:2 * ML_INNER])
           + _mm(v, wif_ref[2 * ML_INNER:]) + bif_ref[...])
    is_forget = jnp.bitwise_and(_iota((1, LANE), 1), 2 * ML_HEADS - 1) >= ML_HEADS
    log_sig = jnp.minimum(pre, 0.0) - jnp.log(1.0 + jnp.exp(-jnp.abs(pre)))
    gt_ref[...] = jnp.where(is_forget, log_sig, pre)
    q_ref[...] = q.astype(BF16)
    k_ref[...] = (k * ML_DH ** -0.5).astype(BF16)
    v_ref[...] = v.astype(BF16)


def _ml_proj_call(x, mod, ng, w):
    b, t, _ = x.shape
    tm = min(ROW_TILE, t)
    wide = jax.ShapeDtypeStruct((b, t, ML_INNER), BF16)
    wide32 = jax.ShapeDtypeStruct((b, t, ML_INNER), F32)
    return pl.pallas_call(
        _ml_proj_body,
        grid=(b, t // tm),
        in_specs=_row_specs(t, tm) + [_MOD_SPEC, _full((3, D)), _full((D, 2 * ML_INNER)), _full((SUB, ML_INNER)),
                                      _full((ML_TILES, LANE, 2 * LANE)), _full((ML_TILES, LANE, LANE)),
                                      _full((3 * ML_INNER, LANE)), _full((1, LANE))],
        out_specs=[_tile(ML_INNER, tm)] * 5 + [_tile(LANE, tm)],
        out_shape=[wide, wide, wide, wide32, wide32, jax.ShapeDtypeStruct((b, t, LANE), F32)],
        scratch_shapes=[pltpu.VMEM((tm + 2 * SUB, D), BF16), pltpu.VMEM((tm + 2 * SUB, ML_INNER), F32)],
        compiler_params=_params("parallel", "parallel"),
        name="mlstm_proj",
    )(x, x, x, mod, ng, w["win"], w["cv"], w["wqk"], w["wv"], w["wif"], w["bif"])


def _ml_scan_body(qf_ref, qb_ref, kf_ref, kb_ref, vf_ref, vb_ref, gf_ref, gb_ref, c0_ref, nm0_ref,
                  hf_ref, hb_ref, c_ref, nm_ref):
    c = pl.program_id(1)
    n = qf_ref.shape[0]
    refs = ((qf_ref, kf_ref, vf_ref, gf_ref, hf_ref), (qb_ref, kb_ref, vb_ref, gb_ref, hb_ref))

    @pl.when(c == 0)
    def _():
        c_ref[...] = c0_ref[...]
        nm_ref[...] = nm0_ref[...]

    tri = _iota((n, n), 0) - _iota((n, n), 1)
    valid = [tri >= 0, tri <= 0]
    dirs = range(2)
    gt, cum, tot, gt_t, cum_t = [], [], [], [], []
    for d in dirs:
        gt.append(refs[d][3][...])
        cum.append(jnp.dot(valid[d].astype(F32), gt[d], precision=HI, preferred_element_type=F32))
        tot.append(jnp.sum(gt[d], axis=0, keepdims=True))
        gt_t.append(gt[d].T)
        cum_t.append(cum[d].T)
    heads = [(d, h) for d in dirs for h in range(ML_HEADS)]
    sls = [slice(h * ML_DH, (h + 1) * ML_DH) for h in range(ML_HEADS)]
    qs = {k: refs[k[0]][0][:, sls[k[1]]] for k in heads}
    ks = {k: refs[k[0]][1][:, sls[k[1]]] for k in heads}
    vs = {k: refs[k[0]][2][:, sls[k[1]]] for k in heads}
    ct = {k: c_ref[k[0], k[1]] for k in heads}
    m_prev = {k: nm_ref[k[0], ML_HEADS + k[1]:ML_HEADS + k[1] + 1, 0:1] for k in heads}
    n_prev = {k: nm_ref[k[0], k[1]:k[1] + 1, :] for k in heads}
    qk = {k: lax.dot_general(qs[k], ks[k], NT, preferred_element_type=F32) for k in heads}
    qc = {k: _mm(qs[k], ct[k].astype(BF16)) for k in heads}
    ig_lane = {k: 2 * ML_HEADS * k[0] + k[1] for k in heads}
    lf_lane = {k: 2 * ML_HEADS * k[0] + ML_HEADS + k[1] for k in heads}
    b_col = {k: cum[k[0]][:, lf_lane[k]:lf_lane[k] + 1] for k in heads}
    for k in heads:
        d, h = k
        b_tot = tot[d][:, lf_lane[k]:lf_lane[k] + 1]
        gj = b_tot - b_col[k] + gt[d][:, ig_lane[k]:ig_lane[k] + 1]
        m_end = b_tot + m_prev[k]
        m_new = jnp.maximum(m_end, jnp.max(gj, axis=0, keepdims=True))
        carry = jnp.exp(m_end - m_new)
        wk = jnp.exp(gj - m_new) * ks[k].astype(F32)
        c_ref[d, h] = carry * ct[k] + lax.dot_general(wk.astype(BF16), vs[k], TN, preferred_element_type=F32)
        nm_ref[d, h:h + 1, :] = carry * n_prev[k] + jnp.sum(wk, axis=0, keepdims=True)
        nm_ref[d, ML_HEADS + h:ML_HEADS + h + 1, :] = jnp.broadcast_to(m_new, (1, ML_DH))
    for k in heads:
        d, h = k
        b_row = cum_t[d][lf_lane[k]:lf_lane[k] + 1]
        ig_row = gt_t[d][ig_lane[k]:ig_lane[k] + 1]
        logd = jnp.where(valid[d], b_col[k] - b_row + ig_row, -jnp.inf)
        m_inter = b_col[k] + m_prev[k]
        m_t = jnp.maximum(m_inter, jnp.max(logd, axis=-1, keepdims=True))
        s = qk[k] * jnp.exp(logd - m_t)
        inter = jnp.exp(m_inter - m_t)
        num = _mm(s, vs[k]) + inter * qc[k]
        den = (jnp.sum(s, axis=-1, keepdims=True)
               + inter * jnp.sum(qs[k].astype(F32) * n_prev[k], axis=-1, keepdims=True))
        refs[d][4][:, sls[h]] = num / jnp.maximum(jnp.abs(den), jnp.exp(-m_t))


def _ml_scan_call(q, k, v, gt, c0, nm0):
    b, t, _ = q.shape
    n = ML_CHUNK
    nc = t // n

    def fwd(width):
        return pl.BlockSpec((None, n, width), lambda bi, c: (bi, c, 0))

    def bwd(width):
        return pl.BlockSpec((None, n, width), lambda bi, c: (bi, nc - 1 - c, 0))

    cst = pl.BlockSpec((2, None, ML_HEADS, ML_DH, ML_DH), lambda bi, c: (0, bi, 0, 0, 0),
                       pipeline_mode=pl.Buffered(1))
    nst = pl.BlockSpec((2, None, 2 * ML_HEADS, ML_DH), lambda bi, c: (0, bi, 0, 0))
    seq = jax.ShapeDtypeStruct((b, t, ML_INNER), F32)
    w = ML_INNER
    return pl.pallas_call(
        _ml_scan_body,
        grid=(b, nc),
        in_specs=[fwd(w), bwd(w), fwd(w), bwd(w), fwd(w), bwd(w), fwd(LANE), bwd(LANE), cst, nst],
        out_specs=[fwd(w), bwd(w), cst, nst],
        out_shape=[seq, seq, jax.ShapeDtypeStruct(c0.shape, F32), jax.ShapeDtypeStruct(nm0.shape, F32)],
        compiler_params=_params("parallel", "arbitrary"),
        name="mlstm_scan",
    )(q, q, k, k, v, v, gt, gt, c0, nm0)


def _ml_out_body(hf_ref, hb_ref, xc_ref, z_ref, x_ref, mod_ref, vec_ref, wo_ref, o_ref):
    y = hf_ref[...] + hb_ref[...]
    parts = []
    for h in range(ML_HEADS):
        yh = y[:, h * ML_DH:(h + 1) * ML_DH]
        parts.append(_rms(yh - jnp.mean(yh, axis=-1, keepdims=True), ML_NORM_EPS))
    yn = jnp.concatenate(parts, axis=-1) * vec_ref[0:1]
    o = _mm((yn + vec_ref[1:2] * xc_ref[...]) * _silu(z_ref[...]), wo_ref[...])
    o_ref[...] = x_ref[...] + mod_ref[5:6] * o


def _ml_out_call(hf, hb, xc, z, x, mod, w):
    b, t, _ = x.shape
    tm = min(ROW_TILE, t)
    return pl.pallas_call(
        _ml_out_body,
        grid=(b, t // tm),
        in_specs=[_tile(ML_INNER, tm)] * 4 + [_tile(D, tm), _MOD_SPEC, _full((SUB, ML_INNER)), _full((ML_INNER, D))],
        out_specs=_tile(D, tm),
        out_shape=jax.ShapeDtypeStruct((b, t, D), F32),
        compiler_params=_params("parallel", "parallel"),
        name="mlstm_readout",
    )(hf, hb, xc, z, x, mod, w["ovec"], w["wo"])


def _mlstm_weights(w_in, conv_w, conv_b, w_q, w_k, w_v, w_if, b_if, skip, norm_g, w_o):
    per_tile = LANE // ML_BLOCK
    eye = jnp.eye(per_tile, dtype=F32)

    def block_diag(w):
        wt = w.reshape(ML_TILES, per_tile, ML_BLOCK, ML_BLOCK)
        return jnp.einsum('tgio,gh->tgiho', wt, eye).reshape(ML_TILES, LANE, LANE)

    wqk = jnp.concatenate([block_diag(w_q), block_diag(w_k)], axis=-1).astype(BF16)
    n_gate = 2 * 2 * ML_HEADS
    wif = jnp.transpose(w_if, (1, 0, 2)).reshape(3 * ML_INNER, n_gate)
    wif = jnp.pad(wif, ((0, 0), (0, LANE - n_gate))).astype(BF16)
    bif = jnp.pad(b_if.reshape(1, n_gate), ((0, 0), (0, LANE - n_gate)))
    cv = jnp.concatenate([conv_w, conv_b[None], jnp.zeros((SUB - 4, ML_INNER), F32)], axis=0)
    ovec = jnp.concatenate([norm_g[None], skip[None], jnp.zeros((SUB - 2, ML_INNER), F32)], axis=0)
    return {"win": w_in.astype(BF16), "cv": cv, "wqk": wqk, "wv": block_diag(w_v).astype(BF16), "wif": wif,
            "bif": bif, "ovec": ovec, "wo": w_o.astype(BF16)}


def _mlstm_mixer(xc, xl, mc, ml, ng, w, ctx_out):
    b = xl.shape[0]
    pc = _ml_proj_call(xc, mc, ng, w)
    pl_ = _ml_proj_call(xl, ml, ng, w)
    c0 = jnp.zeros((2, b, ML_HEADS, ML_DH, ML_DH), F32)
    nm0 = jnp.zeros((2, b, 2 * ML_HEADS, ML_DH), F32)

    def scan(p, cs, nms):
        q, k, v, _, _, gt = p
        return _ml_scan_call(q, k, v, gt, cs, nms)

    def out(hf, hb, p, x, mod):
        return _ml_out_call(hf, hb, p[3], p[4], x, mod, w)

    hcf, hcb, cc, nmc = scan(pc, c0, nm0)
    hlf, hlb, _, _ = scan(pl_, cc, nmc)
    return (out(hcf, hcb, pc, xc, mc) if ctx_out else None), out(hlf, hlb, pl_, xl, ml)


def kernel(x, c, ctx, c_ctx, mod_w, mod_b, norm_g, ffn_w13, ffn_w2, final_g, rw_mu, rw_wrkv, rw_w0, rw_w1, rw_w2, rw_a0, rw_a1, rw_a2, rw_g1, rw_g2, rw_kk, rw_ka, rw_rk, rw_lnx_g, rw_lnx_b, rw_wo, rt_win, rt_log_gamma, rt_wo, ml_win, ml_conv_w, ml_conv_b, ml_wq, ml_wk, ml_wv, ml_wif, ml_bif, ml_skip, ml_norm_g, ml_wo):
    b = x.shape[0]
    depth = mod_w.shape[0]
    assert b < SUB
    cvec = jnp.concatenate([c, c_ctx[None], jnp.zeros((SUB - b - 1, D), F32)], axis=0)
    mods = _mod_call(cvec, mod_w, mod_b)
    xl, xc = x, ctx
    for i in range(depth):
        kind, s = i % 3, i // 3
        last = i == depth - 1
        ml = mods[i, :b].reshape(b, N_MOD, D)
        mc = jnp.broadcast_to(mods[i, b].reshape(1, N_MOD, D), (b, N_MOD, D))
        ng = norm_g[i]
        w13 = ffn_w13[i].astype(BF16)
        w2 = ffn_w2[i].astype(BF16)
        xl = _ffn_call(xl, ml, ng, w13[0], w2[0], 0)
        xc = _ffn_call(xc, mc, ng, w13[0], w2[0], 0)
        if kind == 0:
            w = _rwkv_weights(rw_mu[s], rw_wrkv[s], rw_w0[s], rw_w1[s], rw_w2[s], rw_a0[s], rw_a1[s], rw_a2[s],
                              rw_g1[s], rw_g2[s], rw_kk[s], rw_ka[s], rw_rk[s], rw_lnx_g[s], rw_lnx_b[s], rw_wo[s])
            xc, xl = _rwkv_mixer(xc, xl, mc, ml, ng, w, not last)
        elif kind == 1:
            xc, xl = _retention_mixer(xc, xl, mc, ml, ng, rt_win[s], rt_log_gamma[s], rt_wo[s], not last)
        else:
            w = _mlstm_weights(ml_win[s], ml_conv_w[s], ml_conv_b[s], ml_wq[s], ml_wk[s], ml_wv[s], ml_wif[s],
                               ml_bif[s], ml_skip[s], ml_norm_g[s], ml_wo[s])
            xc, xl = _mlstm_mixer(xc, xl, mc, ml, ng, w, not last)
        xl = _ffn_call(xl, ml, ng, w13[1], w2[1], 2, final_g if last else None)
        if not last:
            xc = _ffn_call(xc, mc, ng, w13[1], w2[1], 2)
    return xl
```

```python
import functools
import math

import jax
import jax.numpy as jnp
from jax import lax
from jax.experimental import pallas as pl
from jax.experimental.pallas import tpu as pltpu

F32 = jnp.float32
BF16 = jnp.bfloat16
HI = lax.Precision.HIGHEST

LANE = 128
SUB = 8
VMEM_LIMIT = 56 * 1024 * 1024

D = 1024
D_FF = 2816
N_MOD = 9
NORM_EPS = 1e-6
GRID_W = 64
RW_HEAD = 64
RW_LNX_EPS = 64e-5
RW_CHUNK = 64
RW_CHUNKS_PER_STEP = 2
RW_GROUP = 4
RW_GROUPS = D // (RW_GROUP * RW_HEAD)
RW_STATE = (RW_GROUPS, RW_GROUP * RW_HEAD, RW_GROUP * RW_HEAD)
RT_HEADS = 4
RT_DK = D // RT_HEADS
RT_DV = 2 * D // RT_HEADS
ROPE_BASE = 10000.0
RT_CHUNK = 256
ML_INNER = 2 * D
ML_HEADS = 4
ML_DH = ML_INNER // ML_HEADS
ML_BLOCK = 4
ML_NORM_EPS = 1e-5
ML_CHUNK = 256
ML_TILES = ML_INNER // LANE

ROW_TILE = 256
FFN_ROW_TILE = 512

NT = (((1,), (1,)), ((), ()))
TN = (((0,), (0,)), ((), ()))


def _params(*sem):
    return pltpu.CompilerParams(dimension_semantics=sem, vmem_limit_bytes=VMEM_LIMIT)


def _sigmoid(x):
    return 1.0 / (1.0 + jnp.exp(-x))


def _silu(x):
    return x * _sigmoid(x)


def _rms(x, eps):
    return x * lax.rsqrt(jnp.mean(x * x, axis=-1, keepdims=True) + eps)


def _modnorm(x, g, shift, scale):
    return _rms(x, NORM_EPS) * g * (1.0 + scale) + shift


def _mm(a, b):
    return jnp.dot(a.astype(BF16), b, preferred_element_type=F32)


def _iota(shape, dim):
    return lax.broadcasted_iota(jnp.int32, shape, dim)


def _mod_body(c_ref, w_ref, b_ref, o_ref):
    s = _silu(c_ref[...])
    o_ref[...] = jnp.dot(s, w_ref[...], precision=HI, preferred_element_type=F32) + b_ref[...]


def _mod_call(cvec, mod_w, mod_b):
    depth, _, n = mod_w.shape
    tn = D
    return pl.pallas_call(
        _mod_body,
        grid=(depth, n // tn),
        in_specs=[pl.BlockSpec((SUB, D), lambda l, j: (0, 0)),
                  pl.BlockSpec((None, D, tn), lambda l, j: (l, 0, j)),
                  pl.BlockSpec((None, 1, tn), lambda l, j: (l, 0, j))],
        out_specs=pl.BlockSpec((None, SUB, tn), lambda l, j: (l, 0, j)),
        out_shape=jax.ShapeDtypeStruct((depth, SUB, n), F32),
        compiler_params=_params("parallel", "parallel"),
        name="adaln_mod",
    )(cvec, mod_w, mod_b.reshape(depth, 1, n))


def _ffn_body(x_ref, mod_ref, ng_ref, w13_ref, w2_ref, *rest, sub, final):
    if final:
        fg_ref, o_ref = rest
    else:
        (o_ref,) = rest
    x = x_ref[...]
    y = _modnorm(x, ng_ref[sub:sub + 1], mod_ref[3 * sub:3 * sub + 1], mod_ref[3 * sub + 1:3 * sub + 2])
    y = y.astype(BF16)
    a = jnp.dot(y, w13_ref[:, :D_FF], preferred_element_type=F32)
    b = jnp.dot(y, w13_ref[:, D_FF:], preferred_element_type=F32)
    out = x + 0.5 * mod_ref[3 * sub + 2:3 * sub + 3] * _mm(_silu(a) * b, w2_ref[...])
    if final:
        out = _rms(out, NORM_EPS) * fg_ref[...]
    o_ref[...] = out


def _ffn_call(x, mod, ng, w13, w2, layer, sub, final_g=None):
    b, t, _ = x.shape
    tm = min(FFN_ROW_TILE, t)
    half = sub // 2
    in_specs = [pl.BlockSpec((None, tm, D), lambda bi, i: (bi, i, 0)),
                pl.BlockSpec((None, N_MOD, D), lambda bi, i: (bi, 0, 0)),
                pl.BlockSpec((3, D), lambda bi, i: (0, 0)),
                pl.BlockSpec((None, None, D, 2 * D_FF), lambda bi, i: (layer, half, 0, 0),
                             pipeline_mode=pl.Buffered(1)),
                pl.BlockSpec((None, None, D_FF, D), lambda bi, i: (layer, half, 0, 0),
                             pipeline_mode=pl.Buffered(1))]
    args = [x, mod, ng, w13, w2]
    if final_g is not None:
        in_specs.append(pl.BlockSpec((1, D), lambda bi, i: (0, 0)))
        args.append(final_g.reshape(1, D))
    return pl.pallas_call(
        functools.partial(_ffn_body, sub=sub, final=final_g is not None),
        grid=(b, t // tm),
        in_specs=in_specs,
        out_specs=pl.BlockSpec((None, tm, D), lambda bi, i: (bi, i, 0)),
        out_shape=jax.ShapeDtypeStruct((b, t, D), F32),
        compiler_params=_params("parallel", "parallel"),
        name="half_ffn",
    )(*args)


def _row_specs(t, tm):
    nb = tm // SUB
    last = t // SUB - 1
    return [pl.BlockSpec((None, tm, D), lambda bi, i: (bi, i, 0)),
            pl.BlockSpec((None, SUB, D), lambda bi, i: (bi, jnp.maximum(i * nb - 1, 0), 0)),
            pl.BlockSpec((None, SUB, D), lambda bi, i: (bi, jnp.minimum((i + 1) * nb, last), 0))]


def _full(shape):
    nd = len(shape)
    return pl.BlockSpec(shape, lambda bi, i: (0,) * nd)


def _tile(width, tm):
    return pl.BlockSpec((None, tm, width), lambda bi, i: (bi, i, 0))


def _tile2(width, tm):
    return pl.BlockSpec((2, None, tm, width), lambda bi, i: (0, bi, i, 0))


_MOD_SPEC = pl.BlockSpec((None, N_MOD, D), lambda bi, i: (bi, 0, 0))


def _chunk_index(d, c, nc):
    return c + d * (nc - 1 - 2 * c)


def _group_ones():
    r = jnp.right_shift(_iota((LANE, LANE), 0), 6)
    c = jnp.right_shift(_iota((LANE, LANE), 1), 6)
    return (r == c).astype(BF16)


def _head_sum64(x, ones):
    parts = [_mm(x[:, t * LANE:(t + 1) * LANE], ones) for t in range(x.shape[1] // LANE)]
    return jnp.concatenate(parts, axis=-1)


def _rw_proj_body(x_ref, xp_ref, xn_ref, mod_ref, ng_ref, vec_ref, wrkv_ref, wd_ref, w2_ref, a2_ref, g2_ref,
                  r_ref, v_ref, g_ref, kk_ref, lw_ref, kd_ref, b_ref):
    i = pl.program_id(1)
    tm = x_ref.shape[0]
    g, shift, scale = ng_ref[1:2], mod_ref[3:4], mod_ref[4:5]
    h = _modnorm(x_ref[...], g, shift, scale)
    hp = jnp.where(i > 0, _modnorm(xp_ref[SUB - 1:SUB], g, shift, scale), 0.0)
    hn = jnp.where(i < pl.num_programs(1) - 1, _modnorm(xn_ref[0:1], g, shift, scale), 0.0)
    row = _iota((tm, 1), 0)
    h_prev = jnp.where(row == 0, hp, pltpu.roll(h, 1, axis=0))
    h_next = jnp.where(row == tm - 1, hn, pltpu.roll(h, tm - 1, axis=0))
    xx = 0.5 * (h_prev + h_next) - h

    def mix(n):
        return (h + xx * vec_ref[n:n + 1]).astype(BF16)

    r = jnp.dot(mix(0), wrkv_ref[0], preferred_element_type=F32)
    k = jnp.dot(mix(2), wrkv_ref[1], preferred_element_type=F32)
    v = jnp.dot(mix(3), wrkv_ref[2], preferred_element_type=F32)
    wl = _mm(jnp.tanh(jnp.dot(mix(1), wd_ref[0], preferred_element_type=F32)), w2_ref[...])
    al = _mm(jnp.dot(mix(4), wd_ref[1], preferred_element_type=F32), a2_ref[...])
    gg = _mm(_sigmoid(jnp.dot(mix(5), wd_ref[2], preferred_element_type=F32)), g2_ref[...])

    kkr = k * vec_ref[10:11]
    n2 = _head_sum64(kkr * kkr, _group_ones())
    kk = kkr / jnp.maximum(jnp.sqrt(n2), 1e-12)
    for d in range(2):
        lw_ref[d] = -math.exp(-0.5) * _sigmoid(vec_ref[6 + d:7 + d] + wl[:, d * D:(d + 1) * D])
        a = _sigmoid(vec_ref[8 + d:9 + d] + al[:, d * D:(d + 1) * D])
        kd_ref[d] = k * (1.0 + (a - 1.0) * vec_ref[11:12])
        b_ref[d] = kk * a
    r_ref[...] = r
    v_ref[...] = v.astype(BF16)
    g_ref[...] = gg.astype(BF16)
    kk_ref[...] = kk


def _rw_proj_call(x, mod, ng, w):
    b, t, _ = x.shape
    tm = min(ROW_TILE, t)
    one = jax.ShapeDtypeStruct((b, t, D), F32)
    half = jax.ShapeDtypeStruct((b, t, D), BF16)
    two = jax.ShapeDtypeStruct((2, b, t, D), F32)
    return pl.pallas_call(
        _rw_proj_body,
        grid=(b, t // tm),
        in_specs=_row_specs(t, tm) + [_MOD_SPEC, _full((3, D)), _full((16, D)), _full((3, D, D)),
                                      _full((3, D, LANE)), _full((LANE, 2 * D)), _full((LANE, 2 * D)),
                                      _full((LANE, D))],
        out_specs=[_tile(D, tm)] * 4 + [_tile2(D, tm)] * 3,
        out_shape=[one, half, half, one] + [two] * 3,
        compiler_params=_params("parallel", "parallel"),
        name="rwkv_proj",
    )(x, x, x, mod, ng, w["vec"], w["wrkv"], w["wd"], w["w2"], w["a2"], w["g2"])


def _rw_scan_body(rf_ref, rb_ref, vf_ref, vb_ref, kkf_ref, kkb_ref, lwf_ref, lwb_ref, kdf_ref, kdb_ref,
                  bf_ref, bb_ref, s0_ref, yf_ref, yb_ref, s_ref):
    c = pl.program_id(1)
    n = RW_CHUNK
    subs = rf_ref.shape[0] // n
    refs = ((rf_ref, vf_ref, kkf_ref, lwf_ref, kdf_ref, bf_ref, yf_ref),
            (rb_ref, vb_ref, kkb_ref, lwb_ref, kdb_ref, bb_ref, yb_ref))

    @pl.when(c == 0)
    def _():
        s_ref[...] = s0_ref[...]

    gl = RW_GROUP * RW_HEAD
    shift = int(math.log2(RW_HEAD))
    tri = _iota((n, n), 0) - _iota((n, n), 1)
    rel = _iota((n, gl), 0) - jnp.bitwise_and(_iota((n, gl), 1), RW_HEAD - 1)
    before = [(tri >= 0).astype(BF16), (tri <= 0).astype(BF16)]
    strict = [rel > 0, rel < 0]
    incl = [rel >= 0, rel <= 0]
    eye = (rel == 0).astype(F32)
    row_head = jnp.right_shift(_iota((gl, 2 * gl), 0), shift)
    col_head = jnp.right_shift(jnp.bitwise_and(_iota((gl, 2 * gl), 1), gl - 1), shift)
    same2 = (row_head == col_head).astype(BF16)
    same = same2[:, :gl]
    same32 = same.astype(F32)
    low = _iota((1, LANE), 1) < RW_HEAD

    def bdiag(m):
        return jnp.concatenate([m] * RW_GROUP, axis=0) * same

    def bf16_parts(a):
        hi = a.astype(BF16)
        rest = a - hi.astype(F32)
        mid = rest.astype(BF16)
        return hi, mid, (rest - mid.astype(F32)).astype(BF16)

    groups = range(D // gl)
    sls = [slice(q * gl, (q + 1) * gl) for q in groups]
    dirs = range(2)
    rows, ab, rb, vb, rhs, hk_t, keep = {}, {}, {}, {}, {}, {}, {}
    a_rb, akv, inv = {}, {}, {}

    def prepare(j):
        for d in dirs:
            r_ref, v_ref, kk_ref, lw_ref, kd_ref, b_ref, _ = refs[d]
            rows[d, j] = pl.ds((subs - 1 - j if d else j) * n, n)
            rw = rows[d, j]
            lw = lw_ref[rw, :]
            cum = sum(jnp.dot(before[d], part, preferred_element_type=F32) for part in bf16_parts(lw))
            tot = jnp.sum(lw, axis=0, keepdims=True)
            kd, bb = kd_ref[rw, :], b_ref[rw, :]
            rt = r_ref[rw, :] * jnp.exp(cum)
            at = -kk_ref[rw, :] * jnp.exp(cum - lw)
            e_out = jnp.exp(-cum)
            kt, bt = kd * e_out, bb * e_out
            e_tail = jnp.exp(tot - cum)
            kh, bh = kd * e_tail, bb * e_tail
            v = v_ref[rw, :]
            etot = jnp.exp(tot)
            yield
            for q in groups:
                sl = sls[q]
                t = jnp.concatenate([bt[:, sl], kt[:, sl]], axis=0).T
                sw = pltpu.roll(t, RW_HEAD, axis=1)
                b2 = jnp.where(low, t, sw).astype(BF16)
                k2 = jnp.where(low, sw, t).astype(BF16)
                rhs[d, j, q] = jnp.concatenate([b2, b2, k2, k2], axis=1) * same2
                ab[d, j, q] = at[:, sl].astype(BF16)
                rb[d, j, q] = rt[:, sl].astype(BF16)
                vb[d, j, q] = v[:, sl].astype(BF16)
                hk_t[d, j, q] = jnp.concatenate([bh[:, sl], kh[:, sl]], axis=0).T.astype(BF16)
                keep[d, j, q] = jnp.concatenate([jnp.broadcast_to(etot[:, sl], (LANE, gl)).T] * 2, axis=1)
                if q % 2:
                    yield

    def invert(j):
        chains = [(d, j, q) for d in dirs for q in groups]
        prod = {k: jnp.dot(jnp.concatenate([ab[k], rb[k]], axis=0), rhs[k], preferred_element_type=F32)
                for k in chains}
        yield
        a_ab = {k: jnp.where(strict[k[0]], prod[k][:n, :gl], 0.0) for k in chains}
        for k in chains:
            a_rb[k] = jnp.where(incl[k[0]], prod[k][n:, :gl], 0.0).astype(BF16)
            a_k = jnp.concatenate([jnp.where(strict[k[0]], prod[k][:n, gl:], 0.0),
                                   jnp.where(incl[k[0]], prod[k][n:, gl:], 0.0)], axis=0)
            akv[k] = _mm(a_k, bdiag(vb[k]))
        acc = {k: eye + a_ab[k] for k in chains}
        pw = {k: a_ab[k].astype(BF16) for k in chains}
        pw = {k: jnp.dot(pw[k], bdiag(pw[k]), preferred_element_type=F32).astype(BF16) for k in chains}
        yield
        for _ in range(int(math.log2(n)) - 2):
            both = {k: jnp.dot(jnp.concatenate([pw[k], acc[k].astype(BF16)], axis=0), bdiag(pw[k]),
                               preferred_element_type=F32) for k in chains}
            yield
            acc = {k: acc[k] + both[k][n:] for k in chains}
            pw = {k: both[k][:n].astype(BF16) for k in chains}
        for k in chains:
            inv[k] = (acc[k] + _mm(acc[k], bdiag(pw[k]))).astype(BF16)
        yield

    heads = [(d, q) for d in dirs for q in groups]
    st = {k: s_ref[k[0], k[1]] for k in heads}

    def advance(j):
        at_j = lambda tab, k: tab[k[0], j, k[1]]
        ss = {k: _mm(jnp.concatenate([at_j(ab, k), at_j(rb, k)], axis=0), st[k].astype(BF16)) + at_j(akv, k)
              for k in heads}
        yield
        u = {k: jnp.dot(at_j(inv, k), bdiag(ss[k][:n].astype(BF16)), preferred_element_type=F32).astype(BF16)
             for k in heads}
        yield
        upd = {k: jnp.dot(at_j(hk_t, k), jnp.concatenate([u[k], at_j(vb, k)], axis=0), preferred_element_type=F32)
               for k in heads}
        yield
        y = {k: ss[k][n:] + jnp.dot(at_j(a_rb, k), bdiag(u[k]), preferred_element_type=F32) for k in heads}
        yield
        for k in heads:
            st[k] = st[k] * at_j(keep, k) + upd[k] * same32
        for d in dirs:
            refs[d][-1][rows[d, j], :] = jnp.concatenate([y[d, q] for q in groups], axis=-1).astype(BF16)
        yield

    chunks = range(subs)
    for phase in [[prepare(j)] for j in chunks] + [[invert(j) for j in chunks]] + [[advance(j)] for j in chunks]:
        while phase:
            phase = [g for g in phase if next(g, phase) is not phase]
    for k in heads:
        s_ref[k[0], k[1]] = st[k]


def _rw_scan_call(r, v, kk, lw, kd, bb, s0):
    b, t, _ = r.shape
    n = RW_CHUNK * RW_CHUNKS_PER_STEP
    nc = t // n
    fwd = pl.BlockSpec((None, n, D), lambda bi, c: (bi, c, 0))
    bwd = pl.BlockSpec((None, n, D), lambda bi, c: (bi, nc - 1 - c, 0))
    fwd2 = pl.BlockSpec((None, None, n, D), lambda bi, c: (0, bi, c, 0))
    bwd2 = pl.BlockSpec((None, None, n, D), lambda bi, c: (1, bi, nc - 1 - c, 0))
    st = pl.BlockSpec((2, None) + RW_STATE, lambda bi, c: (0, bi, 0, 0, 0))
    seq = jax.ShapeDtypeStruct((b, t, D), BF16)
    return pl.pallas_call(
        _rw_scan_body,
        grid=(b, nc),
        in_specs=[fwd, bwd, fwd, bwd, fwd, bwd, fwd2, bwd2, fwd2, bwd2, fwd2, bwd2, st],
        out_specs=[fwd, bwd, st],
        out_shape=[seq, seq, jax.ShapeDtypeStruct(s0.shape, F32)],
        compiler_params=_params("parallel", "arbitrary"),
        name="rwkv_scan",
    )(r, r, v, v, kk, kk, lw, lw, kd, kd, bb, bb, s0)


def _rw_out_body(yf_ref, yb_ref, r_ref, v_ref, g_ref, kd_ref, x_ref, mod_ref, vec_ref, wo_ref, o_ref):
    ones = _group_ones()
    y = yf_ref[...].astype(F32) + yb_ref[...].astype(F32)
    yc = y - _head_sum64(y, ones) * (1.0 / RW_HEAD)
    var = _head_sum64(yc * yc, ones) * (1.0 / RW_HEAD)
    yh = yc * lax.rsqrt(var + RW_LNX_EPS) * vec_ref[0:1] + vec_ref[1:2]
    bonus = _head_sum64(r_ref[...] * (kd_ref[0] + kd_ref[1]) * vec_ref[2:3], ones) * v_ref[...]
    o = _mm((yh + bonus) * g_ref[...], wo_ref[...])
    o_ref[...] = x_ref[...] + mod_ref[5:6] * o


def _rw_out_call(yf, yb, r, v, g, kd, x, mod, w):
    b, t, _ = x.shape
    tm = min(ROW_TILE, t)
    return pl.pallas_call(
        _rw_out_body,
        grid=(b, t // tm),
        in_specs=[_tile(D, tm)] * 5 + [_tile2(D, tm), _tile(D, tm), _MOD_SPEC, _full((SUB, D)), _full((D, D))],
        out_specs=_tile(D, tm),
        out_shape=jax.ShapeDtypeStruct((b, t, D), F32),
        compiler_params=_params("parallel", "parallel"),
        name="rwkv_readout",
    )(yf, yb, r, v, g, kd, x, mod, w["ovec"], w["wo"])


def _rwkv_mixer(xc, xl, mc, ml, ng, w, ctx_out):
    b = xl.shape[0]
    pc = _rw_proj_call(xc, mc, ng, w)
    pl_ = _rw_proj_call(xl, ml, ng, w)
    s0 = jnp.zeros((2, b) + RW_STATE, F32)

    def scan(p, s):
        r, v, _, kk, lw, kd, bb = p
        return _rw_scan_call(r, v, kk, lw, kd, bb, s)

    def out(yf, yb, p, x, mod):
        r, v, g, _, _, kd, _ = p
        return _rw_out_call(yf, yb, r, v, g, kd, x, mod, w)

    ycf, ycb, sc = scan(pc, s0)
    ylf, ylb, _ = scan(pl_, sc)
    return (out(ycf, ycb, pc, xc, mc) if ctx_out else None), out(ylf, ylb, pl_, xl, ml)


def _rwkv_weights(mu, w_rkv, w0, w1, w2, a0, a1, a2, g1, g2, k_k, k_a, r_k, lnx_g, lnx_b, w_o):
    zeros = jnp.zeros((RW_HEAD, D), F32)

    def two_dir(m):
        return jnp.concatenate([jnp.concatenate([m[0], zeros], axis=1),
                                jnp.concatenate([zeros, m[1]], axis=1)], axis=0).astype(BF16)

    vec = jnp.concatenate([mu, w0, a0, k_k[None], k_a[None], jnp.zeros((4, D), F32)], axis=0)
    ovec = jnp.concatenate([lnx_g[None], lnx_b[None], r_k.reshape(1, D), jnp.zeros((5, D), F32)], axis=0)
    wd = jnp.stack([jnp.concatenate([w1[0], w1[1]], axis=1), jnp.concatenate([a1[0], a1[1]], axis=1), g1])
    return {"vec": vec, "ovec": ovec, "wrkv": w_rkv.astype(BF16), "wd": wd.astype(BF16), "w2": two_dir(w2),
            "a2": two_dir(a2), "g2": g2.astype(BF16), "wo": w_o.astype(BF16)}


def _rt_proj_body(x_ref, mod_ref, ng_ref, win_ref, *rest, rope):
    if rope:
        cos_ref, sin_ref, q_ref, k_ref, v_ref, g_ref = rest
    else:
        q_ref, k_ref, v_ref, g_ref = rest
    h = _modnorm(x_ref[...], ng_ref[1:2], mod_ref[3:4], mod_ref[4:5]).astype(BF16)
    q = jnp.dot(h, win_ref[:, 0:D], preferred_element_type=F32)
    k = jnp.dot(h, win_ref[:, D:2 * D], preferred_element_type=F32) * RT_DK ** -0.5
    if rope:
        cos = jnp.concatenate([cos_ref[...]] * RT_HEADS, axis=-1)
        sin = jnp.concatenate([sin_ref[...]] * RT_HEADS, axis=-1)
        low = jnp.bitwise_and(_iota((1, D), 1), LANE - 1) < LANE // 2

        def rot(t):
            partner = jnp.where(low, pltpu.roll(t, D - LANE // 2, axis=1), pltpu.roll(t, LANE // 2, axis=1))
            return t * cos + partner * sin

        q, k = rot(q), rot(k)
    q_ref[...] = q.astype(BF16)
    k_ref[...] = k.astype(BF16)
    v_ref[...] = jnp.dot(h, win_ref[:, 2 * D:4 * D], preferred_element_type=F32).astype(BF16)
    g_ref[...] = jnp.dot(h, win_ref[:, 4 * D:6 * D], preferred_element_type=F32).astype(BF16)


def _rt_proj_call(x, mod, ng, win, tables):
    b, t, _ = x.shape
    tm = min(ROW_TILE, t)
    in_specs = [_tile(D, tm), _MOD_SPEC, _full((3, D)), _full((D, 6 * D))]
    args = [x, mod, ng, win]
    if tables is not None:
        in_specs += [pl.BlockSpec((tm, RT_DK), lambda bi, i: (i, 0))] * 2
        args += list(tables)
    return pl.pallas_call(
        functools.partial(_rt_proj_body, rope=tables is not None),
        grid=(b, t // tm),
        in_specs=in_specs,
        out_specs=[_tile(D, tm), _tile(D, tm), _tile(2 * D, tm), _tile(2 * D, tm)],
        out_shape=[jax.ShapeDtypeStruct((b, t, D), BF16), jax.ShapeDtypeStruct((b, t, D), BF16),
                   jax.ShapeDtypeStruct((b, t, 2 * D), BF16), jax.ShapeDtypeStruct((b, t, 2 * D), BF16)],
        compiler_params=_params("parallel", "parallel"),
        name="retention_proj",
    )(*args)


def _rt_scan_body(lg_ref, qf_ref, qb_ref, kf_ref, kb_ref, vf_ref, vb_ref, s0_ref, of_ref, ob_ref, s_ref):
    c = pl.program_id(1)
    n = qf_ref.shape[0]
    refs = ((qf_ref, kf_ref, vf_ref, of_ref), (qb_ref, kb_ref, vb_ref, ob_ref))

    @pl.when(c == 0)
    def _():
        s_ref[...] = s0_ref[...]

    tri = _iota((n, n), 0) - _iota((n, n), 1)
    row = _iota((n, 1), 0)
    rel = [tri, -tri]
    pos = [row.astype(F32), (n - 1 - row).astype(F32)]
    heads = [(d, h) for d in range(2) for h in range(RT_HEADS)]
    lg = {k: -jnp.abs(lg_ref[k[0], k[1]]) for k in heads}
    qs = {k: refs[k[0]][0][:, k[1] * RT_DK:(k[1] + 1) * RT_DK] for k in heads}
    ks = {k: refs[k[0]][1][:, k[1] * RT_DK:(k[1] + 1) * RT_DK] for k in heads}
    vs = {k: refs[k[0]][2][:, k[1] * RT_DV:(k[1] + 1) * RT_DV] for k in heads}
    st = {k: s_ref[k[0], k[1]] for k in heads}
    qk = {k: lax.dot_general(qs[k], ks[k], NT, preferred_element_type=F32) for k in heads}
    qr = {k: _mm(qs[k], st[k].astype(BF16)) for k in heads}
    kv = {k: lax.dot_general(ks[k], (jnp.exp((n - 1.0 - pos[k[0]]) * lg[k]) * vs[k].astype(F32)).astype(BF16), TN,
                             preferred_element_type=F32) for k in heads}
    for k in heads:
        s_ref[k[0], k[1]] = jnp.exp(jnp.full((1, 1), float(n), F32) * lg[k]) * st[k] + kv[k]
    for k in heads:
        d, h = k
        dist = jnp.maximum(rel[d], 0).astype(F32)
        s = qk[k] * jnp.where(rel[d] >= 0, jnp.exp(dist * lg[k]), 0.0)
        o = _mm(s, vs[k]) + jnp.exp((pos[d] + 1.0) * lg[k]) * qr[k]
        refs[d][3][:, h * RT_DV:(h + 1) * RT_DV] = o.astype(BF16)


def _rt_scan_call(lg, q, k, v, s0):
    b, t, _ = q.shape
    n = RT_CHUNK
    nc = t // n

    def fwd(width):
        return pl.BlockSpec((None, n, width), lambda bi, c: (bi, c, 0))

    def bwd(width):
        return pl.BlockSpec((None, n, width), lambda bi, c: (bi, nc - 1 - c, 0))

    st = pl.BlockSpec((2, None, RT_HEADS, RT_DK, RT_DV), lambda bi, c: (0, bi, 0, 0, 0))
    seq = jax.ShapeDtypeStruct((b, t, 2 * D), BF16)
    return pl.pallas_call(
        _rt_scan_body,
        grid=(b, nc),
        in_specs=[pl.BlockSpec(memory_space=pltpu.SMEM), fwd(D), bwd(D), fwd(D), bwd(D), fwd(2 * D), bwd(2 * D), st],
        out_specs=[fwd(2 * D), bwd(2 * D), st],
        out_shape=[seq, seq, jax.ShapeDtypeStruct(s0.shape, F32)],
        compiler_params=_params("parallel", "arbitrary"),
        name="retention_scan",
    )(lg, q, q, k, k, v, v, s0)


def _rt_out_body(of_ref, ob_ref, g_ref, x_ref, mod_ref, wo_ref, out_ref):
    o = of_ref[...].astype(F32) + ob_ref[...].astype(F32)
    on = jnp.concatenate([_rms(o[:, h * RT_DV:(h + 1) * RT_DV], NORM_EPS) for h in range(RT_HEADS)], axis=-1)
    out_ref[...] = x_ref[...] + mod_ref[5:6] * _mm(_silu(g_ref[...].astype(F32)) * on, wo_ref[...])


def _rt_out_call(of, ob, g, x, mod, wo):
    b, t, _ = x.shape
    tm = min(ROW_TILE, t)
    return pl.pallas_call(
        _rt_out_body,
        grid=(b, t // tm),
        in_specs=[_tile(2 * D, tm)] * 3 + [_tile(D, tm), _MOD_SPEC, _full((2 * D, D))],
        out_specs=_tile(D, tm),
        out_shape=jax.ShapeDtypeStruct((b, t, D), F32),
        compiler_params=_params("parallel", "parallel"),
        name="retention_readout",
    )(of, ob, g, x, mod, wo)


def _rope_tables(t):
    pos = jnp.arange(t, dtype=jnp.int32)
    quarter = RT_DK // 4
    freqs = ROPE_BASE ** (-jnp.arange(quarter, dtype=F32) / quarter)
    ar = (pos // GRID_W).astype(F32)[:, None] * freqs
    ac = (pos % GRID_W).astype(F32)[:, None] * freqs
    cos = jnp.concatenate([jnp.cos(ar), jnp.cos(ar), jnp.cos(ac), jnp.cos(ac)], axis=-1)
    sin = jnp.concatenate([-jnp.sin(ar), jnp.sin(ar), -jnp.sin(ac), jnp.sin(ac)], axis=-1)
    return cos, sin


def _retention_mixer(xc, xl, mc, ml, ng, w_in, log_gamma, w_o, ctx_out):
    b, t, _ = xl.shape
    win, wo = w_in.astype(BF16), w_o.astype(BF16)
    qc, kc, vc, gc = _rt_proj_call(xc, mc, ng, win, None)
    ql, kl, vl, gl = _rt_proj_call(xl, ml, ng, win, _rope_tables(t))
    s0 = jnp.zeros((2, b, RT_HEADS, RT_DK, RT_DV), F32)
    ocf, ocb, sc = _rt_scan_call(log_gamma, qc, kc, vc, s0)
    olf, olb, _ = _rt_scan_call(log_gamma, ql, kl, vl, sc)
    return ((_rt_out_call(ocf, ocb, gc, xc, mc, wo) if ctx_out else None),
            _rt_out_call(olf, olb, gl, xl, ml, wo))


def _ml_proj_body(x_ref, xp_ref, xn_ref, mod_ref, ng_ref, win_ref, cv_ref, wqk_ref, wv_ref, wif_ref, bif_ref,
                  q_ref, k_ref, v_ref, xc_ref, z_ref, gt_ref, y_scr, xm_scr):
    i = pl.program_id(1)
    tm = x_ref.shape[0]
    g, shift, scale = ng_ref[1:2], mod_ref[3:4], mod_ref[4:5]
    y_scr[0:SUB] = _modnorm(xp_ref[...], g, shift, scale).astype(BF16)
    y_scr[SUB:SUB + tm] = _modnorm(x_ref[...], g, shift, scale).astype(BF16)
    y_scr[SUB + tm:] = _modnorm(xn_ref[...], g, shift, scale).astype(BF16)
    xm = jnp.dot(y_scr[...], win_ref[:, 0:ML_INNER], preferred_element_type=F32)
    row = _iota((tm + 2 * SUB, 1), 0)
    lo = jnp.where(i > 0, 0, SUB)
    hi = jnp.where(i < pl.num_programs(1) - 1, tm + 2 * SUB, tm + SUB)
    xm_scr[...] = jnp.where((row >= lo) & (row < hi), xm, 0.0)
    xm_mid = xm_scr[pl.ds(SUB, tm)]
    conv = (xm_scr[pl.ds(SUB - 1, tm)] * cv_ref[0:1] + xm_mid * cv_ref[1:2]
            + xm_scr[pl.ds(SUB + 1, tm)] * cv_ref[2:3] + cv_ref[3:4])
    xc = _silu(conv)
    xc_ref[...] = xc.astype(BF16)
    z_ref[...] = jnp.dot(y_scr[SUB:SUB + tm], win_ref[:, ML_INNER:], preferred_element_type=F32).astype(BF16)
    qs, ks, vs = [], [], []
    for t in range(ML_TILES):
        sl = slice(t * LANE, (t + 1) * LANE)
        qk = _mm(xc[:, sl], wqk_ref[t])
        qs.append(qk[:, :LANE])
        ks.append(qk[:, LANE:])
        vs.append(_mm(xm_mid[:, sl], wv_ref[t]))
    q = jnp.concatenate(qs, axis=-1)
    k = jnp.concatenate(ks, axis=-1)
    v = jnp.concatenate(vs, axis=-1)
    pre = (_mm(q, wif_ref[0:ML_INNER]) + _mm(k, wif_ref[ML_INNER:2 * ML_INNER])
           + _mm(v, wif_ref[2 * ML_INNER:]) + bif_ref[...])
    is_forget = jnp.bitwise_and(_iota((1, LANE), 1), 2 * ML_HEADS - 1) >= ML_HEADS
    log_sig = jnp.minimum(pre, 0.0) - jnp.log(1.0 + jnp.exp(-jnp.abs(pre)))
    gt_ref[...] = jnp.where(is_forget, log_sig, pre)
    q_ref[...] = q.astype(BF16)
    k_ref[...] = (k * ML_DH ** -0.5).astype(BF16)
    v_ref[...] = v.astype(BF16)


def _ml_proj_call(x, mod, ng, w):
    b, t, _ = x.shape
    tm = min(ROW_TILE, t)
    wide = jax.ShapeDtypeStruct((b, t, ML_INNER), BF16)
    return pl.pallas_call(
        _ml_proj_body,
        grid=(b, t // tm),
        in_specs=_row_specs(t, tm) + [_MOD_SPEC, _full((3, D)), _full((D, 2 * ML_INNER)), _full((SUB, ML_INNER)),
                                      _full((ML_TILES, LANE, 2 * LANE)), _full((ML_TILES, LANE, LANE)),
                                      _full((3 * ML_INNER, LANE)), _full((1, LANE))],
        out_specs=[_tile(ML_INNER, tm)] * 5 + [_tile(LANE, tm)],
        out_shape=[wide] * 5 + [jax.ShapeDtypeStruct((b, t, LANE), F32)],
        scratch_shapes=[pltpu.VMEM((tm + 2 * SUB, D), BF16), pltpu.VMEM((tm + 2 * SUB, ML_INNER), F32)],
        compiler_params=_params("parallel", "parallel"),
        name="mlstm_proj",
    )(x, x, x, mod, ng, w["win"], w["cv"], w["wqk"], w["wv"], w["wif"], w["bif"])


def _ml_scan_body(qf_ref, qb_ref, kf_ref, kb_ref, vf_ref, vb_ref, gf_ref, gb_ref, c0_ref, nm0_ref,
                  hf_ref, hb_ref, c_ref, nm_ref):
    c = pl.program_id(1)
    n = qf_ref.shape[0]
    refs = ((qf_ref, kf_ref, vf_ref, gf_ref, hf_ref), (qb_ref, kb_ref, vb_ref, gb_ref, hb_ref))

    @pl.when(c == 0)
    def _():
        c_ref[...] = c0_ref[...]
        nm_ref[...] = nm0_ref[...]

    tri = _iota((n, n), 0) - _iota((n, n), 1)
    valid = [tri >= 0, tri <= 0]
    dirs = range(2)
    gt, cum, tot, gt_t, cum_t = [], [], [], [], []
    for d in dirs:
        gt.append(refs[d][3][...])
        cum.append(jnp.dot(valid[d].astype(F32), gt[d], precision=HI, preferred_element_type=F32))
        tot.append(jnp.sum(gt[d], axis=0, keepdims=True))
        gt_t.append(gt[d].T)
        cum_t.append(cum[d].T)
    heads = [(d, h) for d in dirs for h in range(ML_HEADS)]
    sls = [slice(h * ML_DH, (h + 1) * ML_DH) for h in range(ML_HEADS)]
    qs = {k: refs[k[0]][0][:, sls[k[1]]] for k in heads}
    ks = {k: refs[k[0]][1][:, sls[k[1]]] for k in heads}
    vs = {k: refs[k[0]][2][:, sls[k[1]]] for k in heads}
    ct = {k: c_ref[k[0], k[1]] for k in heads}
    m_prev = {k: nm_ref[k[0], ML_HEADS + k[1]:ML_HEADS + k[1] + 1, 0:1] for k in heads}
    n_prev = {k: nm_ref[k[0], k[1]:k[1] + 1, :] for k in heads}
    qk = {k: lax.dot_general(qs[k], ks[k], NT, preferred_element_type=F32) for k in heads}
    qc = {k: _mm(qs[k], ct[k].astype(BF16)) for k in heads}
    ig_lane = {k: 2 * ML_HEADS * k[0] + k[1] for k in heads}
    lf_lane = {k: 2 * ML_HEADS * k[0] + ML_HEADS + k[1] for k in heads}
    b_col = {k: cum[k[0]][:, lf_lane[k]:lf_lane[k] + 1] for k in heads}
    for k in heads:
        d, h = k
        b_tot = tot[d][:, lf_lane[k]:lf_lane[k] + 1]
        gj = b_tot - b_col[k] + gt[d][:, ig_lane[k]:ig_lane[k] + 1]
        m_end = b_tot + m_prev[k]
        m_new = jnp.maximum(m_end, jnp.max(gj, axis=0, keepdims=True))
        carry = jnp.exp(m_end - m_new)
        wk = jnp.exp(gj - m_new) * ks[k].astype(F32)
        c_ref[d, h] = carry * ct[k] + lax.dot_general(wk.astype(BF16), vs[k], TN, preferred_element_type=F32)
        nm_ref[d, h:h + 1, :] = carry * n_prev[k] + jnp.sum(wk, axis=0, keepdims=True)
        nm_ref[d, ML_HEADS + h:ML_HEADS + h + 1, :] = jnp.broadcast_to(m_new, (1, ML_DH))
    for k in heads:
        d, h = k
        b_row = cum_t[d][lf_lane[k]:lf_lane[k] + 1]
        ig_row = gt_t[d][ig_lane[k]:ig_lane[k] + 1]
        logd = jnp.where(valid[d], b_col[k] - b_row + ig_row, -jnp.inf)
        m_inter = b_col[k] + m_prev[k]
        m_t = jnp.maximum(m_inter, jnp.max(logd, axis=-1, keepdims=True))
        s = qk[k] * jnp.exp(logd - m_t)
        inter = jnp.exp(m_inter - m_t)
        num = _mm(s, vs[k]) + inter * qc[k]
        den = (jnp.sum(s, axis=-1, keepdims=True)
               + inter * jnp.sum(qs[k].astype(F32) * n_prev[k], axis=-1, keepdims=True))
        refs[d][4][:, sls[h]] = (num / jnp.maximum(jnp.abs(den), jnp.exp(-m_t))).astype(BF16)


def _ml_scan_call(q, k, v, gt, c0, nm0):
    b, t, _ = q.shape
    n = ML_CHUNK
    nc = t // n

    def fwd(width):
        return pl.BlockSpec((None, n, width), lambda bi, c: (bi, c, 0))

    def bwd(width):
        return pl.BlockSpec((None, n, width), lambda bi, c: (bi, nc - 1 - c, 0))

    cst = pl.BlockSpec((2, None, ML_HEADS, ML_DH, ML_DH), lambda bi, c: (0, bi, 0, 0, 0),
                       pipeline_mode=pl.Buffered(1))
    nst = pl.BlockSpec((2, None, 2 * ML_HEADS, ML_DH), lambda bi, c: (0, bi, 0, 0))
    seq = jax.ShapeDtypeStruct((b, t, ML_INNER), BF16)
    w = ML_INNER
    return pl.pallas_call(
        _ml_scan_body,
        grid=(b, nc),
        in_specs=[fwd(w), bwd(w), fwd(w), bwd(w), fwd(w), bwd(w), fwd(LANE), bwd(LANE), cst, nst],
        out_specs=[fwd(w), bwd(w), cst, nst],
        out_shape=[seq, seq, jax.ShapeDtypeStruct(c0.shape, F32), jax.ShapeDtypeStruct(nm0.shape, F32)],
        compiler_params=_params("parallel", "arbitrary"),
        name="mlstm_scan",
    )(q, q, k, k, v, v, gt, gt, c0, nm0)


def _ml_out_body(hf_ref, hb_ref, xc_ref, z_ref, x_ref, mod_ref, vec_ref, wo_ref, o_ref):
    y = hf_ref[...].astype(F32) + hb_ref[...].astype(F32)
    parts = []
    for h in range(ML_HEADS):
        yh = y[:, h * ML_DH:(h + 1) * ML_DH]
        parts.append(_rms(yh - jnp.mean(yh, axis=-1, keepdims=True), ML_NORM_EPS))
    yn = jnp.concatenate(parts, axis=-1) * vec_ref[0:1]
    o = _mm((yn + vec_ref[1:2] * xc_ref[...].astype(F32)) * _silu(z_ref[...].astype(F32)), wo_ref[...])
    o_ref[...] = x_ref[...] + mod_ref[5:6] * o


def _ml_out_call(hf, hb, xc, z, x, mod, w):
    b, t, _ = x.shape
    tm = min(ROW_TILE, t)
    return pl.pallas_call(
        _ml_out_body,
        grid=(b, t // tm),
        in_specs=[_tile(ML_INNER, tm)] * 4 + [_tile(D, tm), _MOD_SPEC, _full((SUB, ML_INNER)), _full((ML_INNER, D))],
        out_specs=_tile(D, tm),
        out_shape=jax.ShapeDtypeStruct((b, t, D), F32),
        compiler_params=_params("parallel", "parallel"),
        name="mlstm_readout",
    )(hf, hb, xc, z, x, mod, w["ovec"], w["wo"])


def _mlstm_weights(w_in, conv_w, conv_b, w_q, w_k, w_v, w_if, b_if, skip, norm_g, w_o):
    per_tile = LANE // ML_BLOCK
    eye = jnp.eye(per_tile, dtype=F32)

    def block_diag(w):
        wt = w.reshape(ML_TILES, per_tile, ML_BLOCK, ML_BLOCK)
        return jnp.einsum('tgio,gh->tgiho', wt, eye).reshape(ML_TILES, LANE, LANE)

    wqk = jnp.concatenate([block_diag(w_q), block_diag(w_k)], axis=-1).astype(BF16)
    n_gate = 2 * 2 * ML_HEADS
    wif = jnp.transpose(w_if, (1, 0, 2)).reshape(3 * ML_INNER, n_gate)
    wif = jnp.pad(wif, ((0, 0), (0, LANE - n_gate))).astype(BF16)
    bif = jnp.pad(b_if.reshape(1, n_gate), ((0, 0), (0, LANE - n_gate)))
    cv = jnp.concatenate([conv_w, conv_b[None], jnp.zeros((SUB - 4, ML_INNER), F32)], axis=0)
    ovec = jnp.concatenate([norm_g[None], skip[None], jnp.zeros((SUB - 2, ML_INNER), F32)], axis=0)
    return {"win": w_in.astype(BF16), "cv": cv, "wqk": wqk, "wv": block_diag(w_v).astype(BF16), "wif": wif,
            "bif": bif, "ovec": ovec, "wo": w_o.astype(BF16)}


def _mlstm_mixer(xc, xl, mc, ml, ng, w, ctx_out):
    b = xl.shape[0]
    pc = _ml_proj_call(xc, mc, ng, w)
    pl_ = _ml_proj_call(xl, ml, ng, w)
    c0 = jnp.zeros((2, b, ML_HEADS, ML_DH, ML_DH), F32)
    nm0 = jnp.zeros((2, b, 2 * ML_HEADS, ML_DH), F32)

    def scan(p, cs, nms):
        q, k, v, _, _, gt = p
        return _ml_scan_call(q, k, v, gt, cs, nms)

    def out(hf, hb, p, x, mod):
        return _ml_out_call(hf, hb, p[3], p[4], x, mod, w)

    hcf, hcb, cc, nmc = scan(pc, c0, nm0)
    hlf, hlb, _, _ = scan(pl_, cc, nmc)
    return (out(hcf, hcb, pc, xc, mc) if ctx_out else None), out(hlf, hlb, pl_, xl, ml)


def kernel(x, c, ctx, c_ctx, mod_w, mod_b, norm_g, ffn_w13, ffn_w2, final_g, rw_mu, rw_wrkv, rw_w0, rw_w1, rw_w2, rw_a0, rw_a1, rw_a2, rw_g1, rw_g2, rw_kk, rw_ka, rw_rk, rw_lnx_g, rw_lnx_b, rw_wo, rt_win, rt_log_gamma, rt_wo, ml_win, ml_conv_w, ml_conv_b, ml_wq, ml_wk, ml_wv, ml_wif, ml_bif, ml_skip, ml_norm_g, ml_wo):
    b = x.shape[0]
    depth = mod_w.shape[0]
    assert b < SUB
    cvec = jnp.concatenate([c, c_ctx[None], jnp.zeros((SUB - b - 1, D), F32)], axis=0)
    mods = _mod_call(cvec, mod_w, mod_b)
    w13 = ffn_w13.astype(BF16)
    w2 = ffn_w2.astype(BF16)
    xl, xc = x, ctx
    for i in range(depth):
        kind, s = i % 3, i // 3
        last = i == depth - 1
        ml = mods[i, :b].reshape(b, N_MOD, D)
        mc = jnp.broadcast_to(mods[i, b].reshape(1, N_MOD, D), (b, N_MOD, D))
        ng = norm_g[i]
        xl = _ffn_call(xl, ml, ng, w13, w2, i, 0)
        xc = _ffn_call(xc, mc, ng, w13, w2, i, 0)
        if kind == 0:
            w = _rwkv_weights(rw_mu[s], rw_wrkv[s], rw_w0[s], rw_w1[s], rw_w2[s], rw_a0[s], rw_a1[s], rw_a2[s],
                              rw_g1[s], rw_g2[s], rw_kk[s], rw_ka[s], rw_rk[s], rw_lnx_g[s], rw_lnx_b[s], rw_wo[s])
            xc, xl = _rwkv_mixer(xc, xl, mc, ml, ng, w, not last)
        elif kind == 1:
            xc, xl = _retention_mixer(xc, xl, mc, ml, ng, rt_win[s], rt_log_gamma[s], rt_wo[s], not last)
        else:
            w = _mlstm_weights(ml_win[s], ml_conv_w[s], ml_conv_b[s], ml_wq[s], ml_wk[s], ml_wv[s], ml_wif[s],
                               ml_bif[s], ml_skip[s], ml_norm_g[s], ml_wo[s])
            xc, xl = _mlstm_mixer(xc, xl, mc, ml, ng, w, not last)
        xl = _ffn_call(xl, ml, ng, w13, w2, i, 2, final_g if last else None)
        if not last:
            xc = _ffn_call(xc, mc, ng, w13, w2, i, 2)
    return xl
```

```python
import functools
import math

import jax
import jax.numpy as jnp
from jax import lax
from jax.experimental import pallas as pl
from jax.experimental.pallas import tpu as pltpu

F32 = jnp.float32
BF16 = jnp.bfloat16
HI = lax.Precision.HIGHEST

LANE = 128
SUB = 8
VMEM_LIMIT = 56 * 1024 * 1024

D = 1024
D_FF = 2816
N_MOD = 9
NORM_EPS = 1e-6
GRID_W = 64
RW_HEAD = 64
RW_LNX_EPS = 64e-5
RW_CHUNK = 64
RW_CHUNKS_PER_STEP = 2
RW_GROUP = 4
RW_GROUPS = D // (RW_GROUP * RW_HEAD)
RW_STATE = (RW_GROUPS, RW_GROUP * RW_HEAD, RW_GROUP * RW_HEAD)
RT_HEADS = 4
RT_DK = D // RT_HEADS
RT_DV = 2 * D // RT_HEADS
ROPE_BASE = 10000.0
RT_CHUNK = 256
ML_INNER = 2 * D
ML_HEADS = 4
ML_DH = ML_INNER // ML_HEADS
ML_BLOCK = 4
ML_NORM_EPS = 1e-5
ML_CHUNK = 256
ML_TILES = ML_INNER // LANE

ROW_TILE = 256
FFN_ROW_TILE = 512

NT = (((1,), (1,)), ((), ()))
TN = (((0,), (0,)), ((), ()))


def _params(*sem):
    return pltpu.CompilerParams(dimension_semantics=sem, vmem_limit_bytes=VMEM_LIMIT)


def _sigmoid(x):
    return 1.0 / (1.0 + jnp.exp(-x))


def _silu(x):
    return x * _sigmoid(x)


def _rms(x, eps):
    return x * lax.rsqrt(jnp.mean(x * x, axis=-1, keepdims=True) + eps)


def _modnorm(x, g, shift, scale):
    return _rms(x, NORM_EPS) * g * (1.0 + scale) + shift


def _mm(a, b):
    return jnp.dot(a.astype(BF16), b, preferred_element_type=F32)


def _iota(shape, dim):
    return lax.broadcasted_iota(jnp.int32, shape, dim)


def _mod_body(c_ref, w_ref, b_ref, o_ref):
    s = _silu(c_ref[...])
    o_ref[...] = jnp.dot(s, w_ref[...], precision=HI, preferred_element_type=F32) + b_ref[...]


def _mod_call(cvec, mod_w, mod_b):
    depth, _, n = mod_w.shape
    tn = D
    return pl.pallas_call(
        _mod_body,
        grid=(depth, n // tn),
        in_specs=[pl.BlockSpec((SUB, D), lambda l, j: (0, 0)),
                  pl.BlockSpec((None, D, tn), lambda l, j: (l, 0, j)),
                  pl.BlockSpec((None, 1, tn), lambda l, j: (l, 0, j))],
        out_specs=pl.BlockSpec((None, SUB, tn), lambda l, j: (l, 0, j)),
        out_shape=jax.ShapeDtypeStruct((depth, SUB, n), F32),
        compiler_params=_params("parallel", "parallel"),
        name="adaln_mod",
    )(cvec, mod_w, mod_b.reshape(depth, 1, n))


def _ffn_body(x_ref, mod_ref, ng_ref, w13_ref, w2_ref, *rest, sub, final):
    if final:
        fg_ref, o_ref = rest
    else:
        (o_ref,) = rest
    x = x_ref[...]
    y = _modnorm(x, ng_ref[sub:sub + 1], mod_ref[3 * sub:3 * sub + 1], mod_ref[3 * sub + 1:3 * sub + 2])
    y = y.astype(BF16)
    a = jnp.dot(y, w13_ref[:, :D_FF], preferred_element_type=F32)
    b = jnp.dot(y, w13_ref[:, D_FF:], preferred_element_type=F32)
    out = x + 0.5 * mod_ref[3 * sub + 2:3 * sub + 3] * _mm(_silu(a) * b, w2_ref[...])
    if final:
        out = _rms(out, NORM_EPS) * fg_ref[...]
    o_ref[...] = out


def _ffn_call(x, mod, ng, w13, w2, layer, sub, final_g=None):
    b, t, _ = x.shape
    tm = min(FFN_ROW_TILE, t)
    half = sub // 2
    in_specs = [pl.BlockSpec((None, tm, D), lambda bi, i: (bi, i, 0)),
                pl.BlockSpec((None, N_MOD, D), lambda bi, i: (bi, 0, 0)),
                pl.BlockSpec((3, D), lambda bi, i: (0, 0)),
                pl.BlockSpec((None, None, D, 2 * D_FF), lambda bi, i: (layer, half, 0, 0),
                             pipeline_mode=pl.Buffered(1)),
                pl.BlockSpec((None, None, D_FF, D), lambda bi, i: (layer, half, 0, 0),
                             pipeline_mode=pl.Buffered(1))]
    args = [x, mod, ng, w13, w2]
    if final_g is not None:
        in_specs.append(pl.BlockSpec((1, D), lambda bi, i: (0, 0)))
        args.append(final_g.reshape(1, D))
    return pl.pallas_call(
        functools.partial(_ffn_body, sub=sub, final=final_g is not None),
        grid=(b, t // tm),
        in_specs=in_specs,
        out_specs=pl.BlockSpec((None, tm, D), lambda bi, i: (bi, i, 0)),
        out_shape=jax.ShapeDtypeStruct((b, t, D), F32),
        compiler_params=_params("parallel", "parallel"),
        name="half_ffn",
    )(*args)


def _row_specs(t, tm):
    nb = tm // SUB
    last = t // SUB - 1
    return [pl.BlockSpec((None, tm, D), lambda bi, i: (bi, i, 0)),
            pl.BlockSpec((None, SUB, D), lambda bi, i: (bi, jnp.maximum(i * nb - 1, 0), 0)),
            pl.BlockSpec((None, SUB, D), lambda bi, i: (bi, jnp.minimum((i + 1) * nb, last), 0))]


def _full(shape):
    nd = len(shape)
    return pl.BlockSpec(shape, lambda bi, i: (0,) * nd)


def _tile(width, tm):
    return pl.BlockSpec((None, tm, width), lambda bi, i: (bi, i, 0))


def _tile2(width, tm):
    return pl.BlockSpec((2, None, tm, width), lambda bi, i: (0, bi, i, 0))


_MOD_SPEC = pl.BlockSpec((None, N_MOD, D), lambda bi, i: (bi, 0, 0))


def _chunk_index(d, c, nc):
    return c + d * (nc - 1 - 2 * c)


def _group_ones():
    r = jnp.right_shift(_iota((LANE, LANE), 0), 6)
    c = jnp.right_shift(_iota((LANE, LANE), 1), 6)
    return (r == c).astype(BF16)


def _head_sum64(x, ones):
    parts = [_mm(x[:, t * LANE:(t + 1) * LANE], ones) for t in range(x.shape[1] // LANE)]
    return jnp.concatenate(parts, axis=-1)


def _rw_proj_body(x_ref, xp_ref, xn_ref, mod_ref, ng_ref, vec_ref, wrkv_ref, wd_ref, w2_ref, a2_ref, g2_ref,
                  r_ref, v_ref, g_ref, kk_ref, bv_ref, lw_ref, kd_ref, b_ref):
    i = pl.program_id(1)
    tm = x_ref.shape[0]
    g, shift, scale = ng_ref[1:2], mod_ref[3:4], mod_ref[4:5]
    h = _modnorm(x_ref[...], g, shift, scale)
    hp = jnp.where(i > 0, _modnorm(xp_ref[SUB - 1:SUB], g, shift, scale), 0.0)
    hn = jnp.where(i < pl.num_programs(1) - 1, _modnorm(xn_ref[0:1], g, shift, scale), 0.0)
    row = _iota((tm, 1), 0)
    h_prev = jnp.where(row == 0, hp, pltpu.roll(h, 1, axis=0))
    h_next = jnp.where(row == tm - 1, hn, pltpu.roll(h, tm - 1, axis=0))
    xx = 0.5 * (h_prev + h_next) - h

    def mix(n):
        return (h + xx * vec_ref[n:n + 1]).astype(BF16)

    r = jnp.dot(mix(0), wrkv_ref[0], preferred_element_type=F32)
    k = jnp.dot(mix(2), wrkv_ref[1], preferred_element_type=F32)
    v = jnp.dot(mix(3), wrkv_ref[2], preferred_element_type=F32)
    wl = _mm(jnp.tanh(jnp.dot(mix(1), wd_ref[0], preferred_element_type=F32)), w2_ref[...])
    al = _mm(jnp.dot(mix(4), wd_ref[1], preferred_element_type=F32), a2_ref[...])
    gg = _mm(_sigmoid(jnp.dot(mix(5), wd_ref[2], preferred_element_type=F32)), g2_ref[...])

    ones = _group_ones()
    kkr = k * vec_ref[10:11]
    n2 = _head_sum64(kkr * kkr, ones)
    kk = kkr / jnp.maximum(jnp.sqrt(n2), 1e-12)
    kd_sum = 0.0
    for d in range(2):
        lw_ref[d] = -math.exp(-0.5) * _sigmoid(vec_ref[6 + d:7 + d] + wl[:, d * D:(d + 1) * D])
        a = _sigmoid(vec_ref[8 + d:9 + d] + al[:, d * D:(d + 1) * D])
        kd = k * (1.0 + (a - 1.0) * vec_ref[11:12])
        kd_ref[d] = kd
        b_ref[d] = kk * a
        kd_sum = kd_sum + kd
    bv_ref[...] = (_head_sum64(r * kd_sum * vec_ref[12:13], ones) * v).astype(BF16)
    r_ref[...] = r
    v_ref[...] = v.astype(BF16)
    g_ref[...] = gg.astype(BF16)
    kk_ref[...] = kk


def _rw_proj_call(x, mod, ng, w):
    b, t, _ = x.shape
    tm = min(ROW_TILE, t)
    one = jax.ShapeDtypeStruct((b, t, D), F32)
    half = jax.ShapeDtypeStruct((b, t, D), BF16)
    two = jax.ShapeDtypeStruct((2, b, t, D), F32)
    return pl.pallas_call(
        _rw_proj_body,
        grid=(b, t // tm),
        in_specs=_row_specs(t, tm) + [_MOD_SPEC, _full((3, D)), _full((16, D)), _full((3, D, D)),
                                      _full((3, D, LANE)), _full((LANE, 2 * D)), _full((LANE, 2 * D)),
                                      _full((LANE, D))],
        out_specs=[_tile(D, tm)] * 5 + [_tile2(D, tm)] * 3,
        out_shape=[one, half, half, one, half] + [two] * 3,
        compiler_params=_params("parallel", "parallel"),
        name="rwkv_proj",
    )(x, x, x, mod, ng, w["vec"], w["wrkv"], w["wd"], w["w2"], w["a2"], w["g2"])


def _rw_scan_body(rf_ref, rb_ref, vf_ref, vb_ref, kkf_ref, kkb_ref, lwf_ref, lwb_ref, kdf_ref, kdb_ref,
                  bf_ref, bb_ref, s0_ref, yf_ref, yb_ref, s_ref):
    c = pl.program_id(1)
    n = RW_CHUNK
    subs = rf_ref.shape[0] // n
    refs = ((rf_ref, vf_ref, kkf_ref, lwf_ref, kdf_ref, bf_ref, yf_ref),
            (rb_ref, vb_ref, kkb_ref, lwb_ref, kdb_ref, bb_ref, yb_ref))

    @pl.when(c == 0)
    def _():
        s_ref[...] = s0_ref[...]

    gl = RW_GROUP * RW_HEAD
    shift = int(math.log2(RW_HEAD))
    tri = _iota((n, n), 0) - _iota((n, n), 1)
    rel = _iota((n, gl), 0) - jnp.bitwise_and(_iota((n, gl), 1), RW_HEAD - 1)
    before = [(tri >= 0).astype(BF16), (tri <= 0).astype(BF16)]
    strict = [rel > 0, rel < 0]
    incl = [rel >= 0, rel <= 0]
    eye = (rel == 0).astype(F32)
    row_head = jnp.right_shift(_iota((gl, 2 * gl), 0), shift)
    col_head = jnp.right_shift(jnp.bitwise_and(_iota((gl, 2 * gl), 1), gl - 1), shift)
    same2 = (row_head == col_head).astype(BF16)
    same = same2[:, :gl]
    same32 = same.astype(F32)
    low = _iota((1, LANE), 1) < RW_HEAD

    def bdiag(m):
        return jnp.concatenate([m] * RW_GROUP, axis=0) * same

    def bf16_parts(a):
        hi = a.astype(BF16)
        rest = a - hi.astype(F32)
        mid = rest.astype(BF16)
        return hi, mid, (rest - mid.astype(F32)).astype(BF16)

    groups = range(D // gl)
    sls = [slice(q * gl, (q + 1) * gl) for q in groups]
    dirs = range(2)
    rows, ab, rb, vb, rhs, hk_t, keep = {}, {}, {}, {}, {}, {}, {}
    a_rb, akv, inv = {}, {}, {}

    def prepare(j):
        for d in dirs:
            r_ref, v_ref, kk_ref, lw_ref, kd_ref, b_ref, _ = refs[d]
            rows[d, j] = pl.ds((subs - 1 - j if d else j) * n, n)
            rw = rows[d, j]
            lw = lw_ref[rw, :]
            cum = sum(jnp.dot(before[d], part, preferred_element_type=F32) for part in bf16_parts(lw))
            tot = jnp.sum(lw, axis=0, keepdims=True)
            kd, bb = kd_ref[rw, :], b_ref[rw, :]
            rt = r_ref[rw, :] * jnp.exp(cum)
            at = -kk_ref[rw, :] * jnp.exp(cum - lw)
            e_out = jnp.exp(-cum)
            kt, bt = kd * e_out, bb * e_out
            e_tail = jnp.exp(tot - cum)
            kh, bh = kd * e_tail, bb * e_tail
            v = v_ref[rw, :]
            etot = jnp.exp(tot)
            yield
            for q in groups:
                sl = sls[q]
                t = jnp.concatenate([bt[:, sl], kt[:, sl]], axis=0).T
                sw = pltpu.roll(t, RW_HEAD, axis=1)
                b2 = jnp.where(low, t, sw).astype(BF16)
                k2 = jnp.where(low, sw, t).astype(BF16)
                rhs[d, j, q] = jnp.concatenate([b2, b2, k2, k2], axis=1) * same2
                ab[d, j, q] = at[:, sl].astype(BF16)
                rb[d, j, q] = rt[:, sl].astype(BF16)
                vb[d, j, q] = v[:, sl].astype(BF16)
                hk_t[d, j, q] = jnp.concatenate([bh[:, sl], kh[:, sl]], axis=0).T.astype(BF16)
                keep[d, j, q] = jnp.concatenate([jnp.broadcast_to(etot[:, sl], (LANE, gl)).T] * 2, axis=1)
                if q % 2:
                    yield

    def invert(j):
        chains = [(d, j, q) for d in dirs for q in groups]
        prod = {k: jnp.dot(jnp.concatenate([ab[k], rb[k]], axis=0), rhs[k], preferred_element_type=F32)
                for k in chains}
        yield
        a_ab = {k: jnp.where(strict[k[0]], prod[k][:n, :gl], 0.0) for k in chains}
        for k in chains:
            a_rb[k] = jnp.where(incl[k[0]], prod[k][n:, :gl], 0.0).astype(BF16)
            a_k = jnp.concatenate([jnp.where(strict[k[0]], prod[k][:n, gl:], 0.0),
                                   jnp.where(incl[k[0]], prod[k][n:, gl:], 0.0)], axis=0)
            akv[k] = _mm(a_k, bdiag(vb[k]))
        acc = {k: eye + a_ab[k] for k in chains}
        pw = {k: a_ab[k].astype(BF16) for k in chains}
        pw = {k: jnp.dot(pw[k], bdiag(pw[k]), preferred_element_type=F32).astype(BF16) for k in chains}
        yield
        for _ in range(int(math.log2(n)) - 2):
            both = {k: jnp.dot(jnp.concatenate([pw[k], acc[k].astype(BF16)], axis=0), bdiag(pw[k]),
                               preferred_element_type=F32) for k in chains}
            yield
            acc = {k: acc[k] + both[k][n:] for k in chains}
            pw = {k: both[k][:n].astype(BF16) for k in chains}
        for k in chains:
            inv[k] = (acc[k] + _mm(acc[k], bdiag(pw[k]))).astype(BF16)
        yield

    heads = [(d, q) for d in dirs for q in groups]
    st = {k: s_ref[k[0], k[1]] for k in heads}

    def advance(j):
        at_j = lambda tab, k: tab[k[0], j, k[1]]
        ss = {k: _mm(jnp.concatenate([at_j(ab, k), at_j(rb, k)], axis=0), st[k].astype(BF16)) + at_j(akv, k)
              for k in heads}
        yield
        u = {k: jnp.dot(at_j(inv, k), bdiag(ss[k][:n].astype(BF16)), preferred_element_type=F32).astype(BF16)
             for k in heads}
        yield
        upd = {k: jnp.dot(at_j(hk_t, k), jnp.concatenate([u[k], at_j(vb, k)], axis=0), preferred_element_type=F32)
               for k in heads}
        yield
        y = {k: ss[k][n:] + jnp.dot(at_j(a_rb, k), bdiag(u[k]), preferred_element_type=F32) for k in heads}
        yield
        for k in heads:
            st[k] = st[k] * at_j(keep, k) + upd[k] * same32
        for d in dirs:
            refs[d][-1][rows[d, j], :] = jnp.concatenate([y[d, q] for q in groups], axis=-1).astype(BF16)
        yield

    chunks = range(subs)
    for phase in [[prepare(j)] for j in chunks] + [[invert(j) for j in chunks]] + [[advance(j)] for j in chunks]:
        while phase:
            phase = [g for g in phase if next(g, phase) is not phase]
    for k in heads:
        s_ref[k[0], k[1]] = st[k]


def _rw_scan_call(r, v, kk, lw, kd, bb, s0):
    b, t, _ = r.shape
    n = RW_CHUNK * RW_CHUNKS_PER_STEP
    nc = t // n
    fwd = pl.BlockSpec((None, n, D), lambda bi, c: (bi, c, 0))
    bwd = pl.BlockSpec((None, n, D), lambda bi, c: (bi, nc - 1 - c, 0))
    fwd2 = pl.BlockSpec((None, None, n, D), lambda bi, c: (0, bi, c, 0))
    bwd2 = pl.BlockSpec((None, None, n, D), lambda bi, c: (1, bi, nc - 1 - c, 0))
    st = pl.BlockSpec((2, None) + RW_STATE, lambda bi, c: (0, bi, 0, 0, 0))
    seq = jax.ShapeDtypeStruct((b, t, D), BF16)
    return pl.pallas_call(
        _rw_scan_body,
        grid=(b, nc),
        in_specs=[fwd, bwd, fwd, bwd, fwd, bwd, fwd2, bwd2, fwd2, bwd2, fwd2, bwd2, st],
        out_specs=[fwd, bwd, st],
        out_shape=[seq, seq, jax.ShapeDtypeStruct(s0.shape, F32)],
        compiler_params=_params("parallel", "arbitrary"),
        name="rwkv_scan",
    )(r, r, v, v, kk, kk, lw, lw, kd, kd, bb, bb, s0)


def _rw_out_body(yf_ref, yb_ref, bv_ref, g_ref, x_ref, mod_ref, vec_ref, wo_ref, o_ref):
    ones = _group_ones()
    y = yf_ref[...].astype(F32) + yb_ref[...].astype(F32)
    yc = y - _head_sum64(y, ones) * (1.0 / RW_HEAD)
    var = _head_sum64(yc * yc, ones) * (1.0 / RW_HEAD)
    yh = yc * lax.rsqrt(var + RW_LNX_EPS) * vec_ref[0:1] + vec_ref[1:2]
    o = _mm((yh + bv_ref[...]) * g_ref[...], wo_ref[...])
    o_ref[...] = x_ref[...] + mod_ref[5:6] * o


def _rw_out_call(yf, yb, bv, g, x, mod, w):
    b, t, _ = x.shape
    tm = min(ROW_TILE, t)
    return pl.pallas_call(
        _rw_out_body,
        grid=(b, t // tm),
        in_specs=[_tile(D, tm)] * 5 + [_MOD_SPEC, _full((SUB, D)), _full((D, D))],
        out_specs=_tile(D, tm),
        out_shape=jax.ShapeDtypeStruct((b, t, D), F32),
        compiler_params=_params("parallel", "parallel"),
        name="rwkv_readout",
    )(yf, yb, bv, g, x, mod, w["ovec"], w["wo"])


def _rwkv_mixer(xc, xl, mc, ml, ng, w, ctx_out):
    b = xl.shape[0]
    pc = _rw_proj_call(xc, mc, ng, w)
    pl_ = _rw_proj_call(xl, ml, ng, w)
    s0 = jnp.zeros((2, b) + RW_STATE, F32)

    def scan(p, s):
        r, v, _, kk, _, lw, kd, bb = p
        return _rw_scan_call(r, v, kk, lw, kd, bb, s)

    def out(yf, yb, p, x, mod):
        return _rw_out_call(yf, yb, p[4], p[2], x, mod, w)

    ycf, ycb, sc = scan(pc, s0)
    ylf, ylb, _ = scan(pl_, sc)
    return (out(ycf, ycb, pc, xc, mc) if ctx_out else None), out(ylf, ylb, pl_, xl, ml)


def _rwkv_weights(mu, w_rkv, w0, w1, w2, a0, a1, a2, g1, g2, k_k, k_a, r_k, lnx_g, lnx_b, w_o):
    zeros = jnp.zeros((RW_HEAD, D), F32)

    def two_dir(m):
        return jnp.concatenate([jnp.concatenate([m[0], zeros], axis=1),
                                jnp.concatenate([zeros, m[1]], axis=1)], axis=0).astype(BF16)

    vec = jnp.concatenate([mu, w0, a0, k_k[None], k_a[None], r_k.reshape(1, D), jnp.zeros((3, D), F32)], axis=0)
    ovec = jnp.concatenate([lnx_g[None], lnx_b[None], jnp.zeros((SUB - 2, D), F32)], axis=0)
    wd = jnp.stack([jnp.concatenate([w1[0], w1[1]], axis=1), jnp.concatenate([a1[0], a1[1]], axis=1), g1])
    return {"vec": vec, "ovec": ovec, "wrkv": w_rkv.astype(BF16), "wd": wd.astype(BF16), "w2": two_dir(w2),
            "a2": two_dir(a2), "g2": g2.astype(BF16), "wo": w_o.astype(BF16)}


def _rt_proj_body(x_ref, mod_ref, ng_ref, win_ref, *rest, rope):
    if rope:
        cos_ref, sin_ref, q_ref, k_ref, v_ref, g_ref = rest
    else:
        q_ref, k_ref, v_ref, g_ref = rest
    h = _modnorm(x_ref[...], ng_ref[1:2], mod_ref[3:4], mod_ref[4:5]).astype(BF16)
    q = jnp.dot(h, win_ref[:, 0:D], preferred_element_type=F32)
    k = jnp.dot(h, win_ref[:, D:2 * D], preferred_element_type=F32) * RT_DK ** -0.5
    if rope:
        cos = jnp.concatenate([cos_ref[...]] * RT_HEADS, axis=-1)
        sin = jnp.concatenate([sin_ref[...]] * RT_HEADS, axis=-1)
        low = jnp.bitwise_and(_iota((1, D), 1), LANE - 1) < LANE // 2

        def rot(t):
            partner = jnp.where(low, pltpu.roll(t, D - LANE // 2, axis=1), pltpu.roll(t, LANE // 2, axis=1))
            return t * cos + partner * sin

        q, k = rot(q), rot(k)
    q_ref[...] = q.astype(BF16)
    k_ref[...] = k.astype(BF16)
    v_ref[...] = jnp.dot(h, win_ref[:, 2 * D:4 * D], preferred_element_type=F32).astype(BF16)
    g_ref[...] = jnp.dot(h, win_ref[:, 4 * D:6 * D], preferred_element_type=F32).astype(BF16)


def _rt_proj_call(x, mod, ng, win, tables):
    b, t, _ = x.shape
    tm = min(ROW_TILE, t)
    in_specs = [_tile(D, tm), _MOD_SPEC, _full((3, D)), _full((D, 6 * D))]
    args = [x, mod, ng, win]
    if tables is not None:
        in_specs += [pl.BlockSpec((tm, RT_DK), lambda bi, i: (i, 0))] * 2
        args += list(tables)
    return pl.pallas_call(
        functools.partial(_rt_proj_body, rope=tables is not None),
        grid=(b, t // tm),
        in_specs=in_specs,
        out_specs=[_tile(D, tm), _tile(D, tm), _tile(2 * D, tm), _tile(2 * D, tm)],
        out_shape=[jax.ShapeDtypeStruct((b, t, D), BF16), jax.ShapeDtypeStruct((b, t, D), BF16),
                   jax.ShapeDtypeStruct((b, t, 2 * D), BF16), jax.ShapeDtypeStruct((b, t, 2 * D), BF16)],
        compiler_params=_params("parallel", "parallel"),
        name="retention_proj",
    )(*args)


def _rt_scan_body(lg_ref, qf_ref, qb_ref, kf_ref, kb_ref, vf_ref, vb_ref, s0_ref, of_ref, ob_ref, s_ref):
    c = pl.program_id(1)
    n = qf_ref.shape[0]
    refs = ((qf_ref, kf_ref, vf_ref, of_ref), (qb_ref, kb_ref, vb_ref, ob_ref))

    @pl.when(c == 0)
    def _():
        s_ref[...] = s0_ref[...]

    tri = _iota((n, n), 0) - _iota((n, n), 1)
    row = _iota((n, 1), 0)
    rel = [tri, -tri]
    pos = [row.astype(F32), (n - 1 - row).astype(F32)]
    heads = [(d, h) for d in range(2) for h in range(RT_HEADS)]
    lg = {k: -jnp.abs(lg_ref[k[0], k[1]]) for k in heads}
    qs = {k: refs[k[0]][0][:, k[1] * RT_DK:(k[1] + 1) * RT_DK] for k in heads}
    ks = {k: refs[k[0]][1][:, k[1] * RT_DK:(k[1] + 1) * RT_DK] for k in heads}
    vs = {k: refs[k[0]][2][:, k[1] * RT_DV:(k[1] + 1) * RT_DV] for k in heads}
    st = {k: s_ref[k[0], k[1]] for k in heads}
    qk = {k: lax.dot_general(qs[k], ks[k], NT, preferred_element_type=F32) for k in heads}
    qr = {k: _mm(qs[k], st[k].astype(BF16)) for k in heads}
    kv = {k: lax.dot_general(ks[k], (jnp.exp((n - 1.0 - pos[k[0]]) * lg[k]) * vs[k].astype(F32)).astype(BF16), TN,
                             preferred_element_type=F32) for k in heads}
    for k in heads:
        s_ref[k[0], k[1]] = jnp.exp(jnp.full((1, 1), float(n), F32) * lg[k]) * st[k] + kv[k]
    for k in heads:
        d, h = k
        dist = jnp.maximum(rel[d], 0).astype(F32)
        s = qk[k] * jnp.where(rel[d] >= 0, jnp.exp(dist * lg[k]), 0.0)
        o = _mm(s, vs[k]) + jnp.exp((pos[d] + 1.0) * lg[k]) * qr[k]
        refs[d][3][:, h * RT_DV:(h + 1) * RT_DV] = o.astype(BF16)


def _rt_scan_call(lg, q, k, v, s0):
    b, t, _ = q.shape
    n = RT_CHUNK
    nc = t // n

    def fwd(width):
        return pl.BlockSpec((None, n, width), lambda bi, c: (bi, c, 0))

    def bwd(width):
        return pl.BlockSpec((None, n, width), lambda bi, c: (bi, nc - 1 - c, 0))

    st = pl.BlockSpec((2, None, RT_HEADS, RT_DK, RT_DV), lambda bi, c: (0, bi, 0, 0, 0))
    seq = jax.ShapeDtypeStruct((b, t, 2 * D), BF16)
    return pl.pallas_call(
        _rt_scan_body,
        grid=(b, nc),
        in_specs=[pl.BlockSpec(memory_space=pltpu.SMEM), fwd(D), bwd(D), fwd(D), bwd(D), fwd(2 * D), bwd(2 * D), st],
        out_specs=[fwd(2 * D), bwd(2 * D), st],
        out_shape=[seq, seq, jax.ShapeDtypeStruct(s0.shape, F32)],
        compiler_params=_params("parallel", "arbitrary"),
        name="retention_scan",
    )(lg, q, q, k, k, v, v, s0)


def _rt_out_body(of_ref, ob_ref, g_ref, x_ref, mod_ref, wo_ref, out_ref):
    o = of_ref[...].astype(F32) + ob_ref[...].astype(F32)
    on = jnp.concatenate([_rms(o[:, h * RT_DV:(h + 1) * RT_DV], NORM_EPS) for h in range(RT_HEADS)], axis=-1)
    out_ref[...] = x_ref[...] + mod_ref[5:6] * _mm(_silu(g_ref[...].astype(F32)) * on, wo_ref[...])


def _rt_out_call(of, ob, g, x, mod, wo):
    b, t, _ = x.shape
    tm = min(ROW_TILE, t)
    return pl.pallas_call(
        _rt_out_body,
        grid=(b, t // tm),
        in_specs=[_tile(2 * D, tm)] * 3 + [_tile(D, tm), _MOD_SPEC, _full((2 * D, D))],
        out_specs=_tile(D, tm),
        out_shape=jax.ShapeDtypeStruct((b, t, D), F32),
        compiler_params=_params("parallel", "parallel"),
        name="retention_readout",
    )(of, ob, g, x, mod, wo)


def _rope_tables(t):
    pos = jnp.arange(t, dtype=jnp.int32)
    quarter = RT_DK // 4
    freqs = ROPE_BASE ** (-jnp.arange(quarter, dtype=F32) / quarter)
    ar = (pos // GRID_W).astype(F32)[:, None] * freqs
    ac = (pos % GRID_W).astype(F32)[:, None] * freqs
    cos = jnp.concatenate([jnp.cos(ar), jnp.cos(ar), jnp.cos(ac), jnp.cos(ac)], axis=-1)
    sin = jnp.concatenate([-jnp.sin(ar), jnp.sin(ar), -jnp.sin(ac), jnp.sin(ac)], axis=-1)
    return cos, sin


def _retention_mixer(xc, xl, mc, ml, ng, w_in, log_gamma, w_o, ctx_out):
    b, t, _ = xl.shape
    win, wo = w_in.astype(BF16), w_o.astype(BF16)
    qc, kc, vc, gc = _rt_proj_call(xc, mc, ng, win, None)
    ql, kl, vl, gl = _rt_proj_call(xl, ml, ng, win, _rope_tables(t))
    s0 = jnp.zeros((2, b, RT_HEADS, RT_DK, RT_DV), F32)
    ocf, ocb, sc = _rt_scan_call(log_gamma, qc, kc, vc, s0)
    olf, olb, _ = _rt_scan_call(log_gamma, ql, kl, vl, sc)
    return ((_rt_out_call(ocf, ocb, gc, xc, mc, wo) if ctx_out else None),
            _rt_out_call(olf, olb, gl, xl, ml, wo))


def _ml_proj_body(x_ref, xp_ref, xn_ref, mod_ref, ng_ref, win_ref, cv_ref, wqk_ref, wv_ref, wif_ref, bif_ref,
                  q_ref, k_ref, v_ref, xc_ref, z_ref, gt_ref, y_scr, xm_scr):
    i = pl.program_id(1)
    tm = x_ref.shape[0]
    g, shift, scale = ng_ref[1:2], mod_ref[3:4], mod_ref[4:5]
    y_scr[0:SUB] = _modnorm(xp_ref[...], g, shift, scale).astype(BF16)
    y_scr[SUB:SUB + tm] = _modnorm(x_ref[...], g, shift, scale).astype(BF16)
    y_scr[SUB + tm:] = _modnorm(xn_ref[...], g, shift, scale).astype(BF16)
    xm = jnp.dot(y_scr[...], win_ref[:, 0:ML_INNER], preferred_element_type=F32)
    row = _iota((tm + 2 * SUB, 1), 0)
    lo = jnp.where(i > 0, 0, SUB)
    hi = jnp.where(i < pl.num_programs(1) - 1, tm + 2 * SUB, tm + SUB)
    xm_scr[...] = jnp.where((row >= lo) & (row < hi), xm, 0.0)
    xm_mid = xm_scr[pl.ds(SUB, tm)]
    conv = (xm_scr[pl.ds(SUB - 1, tm)] * cv_ref[0:1] + xm_mid * cv_ref[1:2]
            + xm_scr[pl.ds(SUB + 1, tm)] * cv_ref[2:3] + cv_ref[3:4])
    xc = _silu(conv)
    xc_ref[...] = xc.astype(BF16)
    z_ref[...] = jnp.dot(y_scr[SUB:SUB + tm], win_ref[:, ML_INNER:], preferred_element_type=F32).astype(BF16)
    qs, ks, vs = [], [], []
    for t in range(ML_TILES):
        sl = slice(t * LANE, (t + 1) * LANE)
        qk = _mm(xc[:, sl], wqk_ref[t])
        qs.append(qk[:, :LANE])
        ks.append(qk[:, LANE:])
        vs.append(_mm(xm_mid[:, sl], wv_ref[t]))
    q = jnp.concatenate(qs, axis=-1)
    k = jnp.concatenate(ks, axis=-1)
    v = jnp.concatenate(vs, axis=-1)
    pre = (_mm(q, wif_ref[0:ML_INNER]) + _mm(k, wif_ref[ML_INNER:2 * ML_INNER])
           + _mm(v, wif_ref[2 * ML_INNER:]) + bif_ref[...])
    is_forget = jnp.bitwise_and(_iota((1, LANE), 1), 2 * ML_HEADS - 1) >= ML_HEADS
    log_sig = jnp.minimum(pre, 0.0) - jnp.log(1.0 + jnp.exp(-jnp.abs(pre)))
    gt_ref[...] = jnp.where(is_forget, log_sig, pre)
    q_ref[...] = q.astype(BF16)
    k_ref[...] = (k * ML_DH ** -0.5).astype(BF16)
    v_ref[...] = v.astype(BF16)


def _ml_proj_call(x, mod, ng, w):
    b, t, _ = x.shape
    tm = min(ROW_TILE, t)
    wide = jax.ShapeDtypeStruct((b, t, ML_INNER), BF16)
    return pl.pallas_call(
        _ml_proj_body,
        grid=(b, t // tm),
        in_specs=_row_specs(t, tm) + [_MOD_SPEC, _full((3, D)), _full((D, 2 * ML_INNER)), _full((SUB, ML_INNER)),
                                      _full((ML_TILES, LANE, 2 * LANE)), _full((ML_TILES, LANE, LANE)),
                                      _full((3 * ML_INNER, LANE)), _full((1, LANE))],
        out_specs=[_tile(ML_INNER, tm)] * 5 + [_tile(LANE, tm)],
        out_shape=[wide] * 5 + [jax.ShapeDtypeStruct((b, t, LANE), F32)],
        scratch_shapes=[pltpu.VMEM((tm + 2 * SUB, D), BF16), pltpu.VMEM((tm + 2 * SUB, ML_INNER), F32)],
        compiler_params=_params("parallel", "parallel"),
        name="mlstm_proj",
    )(x, x, x, mod, ng, w["win"], w["cv"], w["wqk"], w["wv"], w["wif"], w["bif"])


def _ml_scan_body(qf_ref, qb_ref, kf_ref, kb_ref, vf_ref, vb_ref, gf_ref, gb_ref, c0_ref, nm0_ref,
                  hf_ref, hb_ref, c_ref, nm_ref):
    c = pl.program_id(1)
    n = qf_ref.shape[0]
    refs = ((qf_ref, kf_ref, vf_ref, gf_ref, hf_ref), (qb_ref, kb_ref, vb_ref, gb_ref, hb_ref))

    @pl.when(c == 0)
    def _():
        c_ref[...] = c0_ref[...]
        nm_ref[...] = nm0_ref[...]

    tri = _iota((n, n), 0) - _iota((n, n), 1)
    valid = [tri >= 0, tri <= 0]
    dirs = range(2)
    gt, cum, tot, gt_t, cum_t = [], [], [], [], []
    for d in dirs:
        gt.append(refs[d][3][...])
        cum.append(jnp.dot(valid[d].astype(F32), gt[d], precision=HI, preferred_element_type=F32))
        tot.append(jnp.sum(gt[d], axis=0, keepdims=True))
        gt_t.append(gt[d].T)
        cum_t.append(cum[d].T)
    heads = [(d, h) for d in dirs for h in range(ML_HEADS)]
    sls = [slice(h * ML_DH, (h + 1) * ML_DH) for h in range(ML_HEADS)]
    qs = {k: refs[k[0]][0][:, sls[k[1]]] for k in heads}
    ks = {k: refs[k[0]][1][:, sls[k[1]]] for k in heads}
    vs = {k: refs[k[0]][2][:, sls[k[1]]] for k in heads}
    ct = {k: c_ref[k[0], k[1]] for k in heads}
    m_prev = {k: nm_ref[k[0], ML_HEADS + k[1]:ML_HEADS + k[1] + 1, 0:1] for k in heads}
    n_prev = {k: nm_ref[k[0], k[1]:k[1] + 1, :] for k in heads}
    qk = {k: lax.dot_general(qs[k], ks[k], NT, preferred_element_type=F32) for k in heads}
    qc = {k: _mm(qs[k], ct[k].astype(BF16)) for k in heads}
    ig_lane = {k: 2 * ML_HEADS * k[0] + k[1] for k in heads}
    lf_lane = {k: 2 * ML_HEADS * k[0] + ML_HEADS + k[1] for k in heads}
    b_col = {k: cum[k[0]][:, lf_lane[k]:lf_lane[k] + 1] for k in heads}
    for k in heads:
        d, h = k
        b_tot = tot[d][:, lf_lane[k]:lf_lane[k] + 1]
        gj = b_tot - b_col[k] + gt[d][:, ig_lane[k]:ig_lane[k] + 1]
        m_end = b_tot + m_prev[k]
        m_new = jnp.maximum(m_end, jnp.max(gj, axis=0, keepdims=True))
        carry = jnp.exp(m_end - m_new)
        wk = jnp.exp(gj - m_new) * ks[k].astype(F32)
        c_ref[d, h] = carry * ct[k] + lax.dot_general(wk.astype(BF16), vs[k], TN, preferred_element_type=F32)
        nm_ref[d, h:h + 1, :] = carry * n_prev[k] + jnp.sum(wk, axis=0, keepdims=True)
        nm_ref[d, ML_HEADS + h:ML_HEADS + h + 1, :] = jnp.broadcast_to(m_new, (1, ML_DH))
    s, m_t, inter = {}, {}, {}
    for k in heads:
        d = k[0]
        b_row = cum_t[d][lf_lane[k]:lf_lane[k] + 1]
        ig_row = gt_t[d][ig_lane[k]:ig_lane[k] + 1]
        logd = jnp.where(valid[d], b_col[k] - b_row + ig_row, -jnp.inf)
        m_inter = b_col[k] + m_prev[k]
        m_t[k] = jnp.maximum(m_inter, jnp.max(logd, axis=-1, keepdims=True))
        s[k] = qk[k] * jnp.exp(logd - m_t[k])
        inter[k] = jnp.exp(m_inter - m_t[k])
    sv = {k: _mm(s[k], vs[k]) for k in heads}
    for k in heads:
        d, h = k
        num = sv[k] + inter[k] * qc[k]
        den = (jnp.sum(s[k], axis=-1, keepdims=True)
               + inter[k] * jnp.sum(qs[k].astype(F32) * n_prev[k], axis=-1, keepdims=True))
        refs[d][4][:, sls[h]] = (num / jnp.maximum(jnp.abs(den), jnp.exp(-m_t[k]))).astype(BF16)


def _ml_scan_call(q, k, v, gt, c0, nm0):
    b, t, _ = q.shape
    n = ML_CHUNK
    nc = t // n

    def fwd(width):
        return pl.BlockSpec((None, n, width), lambda bi, c: (bi, c, 0))

    def bwd(width):
        return pl.BlockSpec((None, n, width), lambda bi, c: (bi, nc - 1 - c, 0))

    cst = pl.BlockSpec((2, None, ML_HEADS, ML_DH, ML_DH), lambda bi, c: (0, bi, 0, 0, 0),
                       pipeline_mode=pl.Buffered(1))
    nst = pl.BlockSpec((2, None, 2 * ML_HEADS, ML_DH), lambda bi, c: (0, bi, 0, 0))
    seq = jax.ShapeDtypeStruct((b, t, ML_INNER), BF16)
    w = ML_INNER
    return pl.pallas_call(
        _ml_scan_body,
        grid=(b, nc),
        in_specs=[fwd(w), bwd(w), fwd(w), bwd(w), fwd(w), bwd(w), fwd(LANE), bwd(LANE), cst, nst],
        out_specs=[fwd(w), bwd(w), cst, nst],
        out_shape=[seq, seq, jax.ShapeDtypeStruct(c0.shape, F32), jax.ShapeDtypeStruct(nm0.shape, F32)],
        compiler_params=_params("parallel", "arbitrary"),
        name="mlstm_scan",
    )(q, q, k, k, v, v, gt, gt, c0, nm0)


def _ml_out_body(hf_ref, hb_ref, xc_ref, z_ref, x_ref, mod_ref, vec_ref, wo_ref, o_ref):
    y = hf_ref[...].astype(F32) + hb_ref[...].astype(F32)
    parts = []
    for h in range(ML_HEADS):
        yh = y[:, h * ML_DH:(h + 1) * ML_DH]
        parts.append(_rms(yh - jnp.mean(yh, axis=-1, keepdims=True), ML_NORM_EPS))
    yn = jnp.concatenate(parts, axis=-1) * vec_ref[0:1]
    o = _mm((yn + vec_ref[1:2] * xc_ref[...].astype(F32)) * _silu(z_ref[...].astype(F32)), wo_ref[...])
    o_ref[...] = x_ref[...] + mod_ref[5:6] * o


def _ml_out_call(hf, hb, xc, z, x, mod, w):
    b, t, _ = x.shape
    tm = min(ROW_TILE, t)
    return pl.pallas_call(
        _ml_out_body,
        grid=(b, t // tm),
        in_specs=[_tile(ML_INNER, tm)] * 4 + [_tile(D, tm), _MOD_SPEC, _full((SUB, ML_INNER)), _full((ML_INNER, D))],
        out_specs=_tile(D, tm),
        out_shape=jax.ShapeDtypeStruct((b, t, D), F32),
        compiler_params=_params("parallel", "parallel"),
        name="mlstm_readout",
    )(hf, hb, xc, z, x, mod, w["ovec"], w["wo"])


def _mlstm_weights(w_in, conv_w, conv_b, w_q, w_k, w_v, w_if, b_if, skip, norm_g, w_o):
    per_tile = LANE // ML_BLOCK
    eye = jnp.eye(per_tile, dtype=F32)

    def block_diag(w):
        wt = w.reshape(ML_TILES, per_tile, ML_BLOCK, ML_BLOCK)
        return jnp.einsum('tgio,gh->tgiho', wt, eye).reshape(ML_TILES, LANE, LANE)

    wqk = jnp.concatenate([block_diag(w_q), block_diag(w_k)], axis=-1).astype(BF16)
    n_gate = 2 * 2 * ML_HEADS
    wif = jnp.transpose(w_if, (1, 0, 2)).reshape(3 * ML_INNER, n_gate)
    wif = jnp.pad(wif, ((0, 0), (0, LANE - n_gate))).astype(BF16)
    bif = jnp.pad(b_if.reshape(1, n_gate), ((0, 0), (0, LANE - n_gate)))
    cv = jnp.concatenate([conv_w, conv_b[None], jnp.zeros((SUB - 4, ML_INNER), F32)], axis=0)
    ovec = jnp.concatenate([norm_g[None], skip[None], jnp.zeros((SUB - 2, ML_INNER), F32)], axis=0)
    return {"win": w_in.astype(BF16), "cv": cv, "wqk": wqk, "wv": block_diag(w_v).astype(BF16), "wif": wif,
            "bif": bif, "ovec": ovec, "wo": w_o.astype(BF16)}


def _mlstm_mixer(xc, xl, mc, ml, ng, w, ctx_out):
    b = xl.shape[0]
    pc = _ml_proj_call(xc, mc, ng, w)
    pl_ = _ml_proj_call(xl, ml, ng, w)
    c0 = jnp.zeros((2, b, ML_HEADS, ML_DH, ML_DH), F32)
    nm0 = jnp.zeros((2, b, 2 * ML_HEADS, ML_DH), F32)

    def scan(p, cs, nms):
        q, k, v, _, _, gt = p
        return _ml_scan_call(q, k, v, gt, cs, nms)

    def out(hf, hb, p, x, mod):
        return _ml_out_call(hf, hb, p[3], p[4], x, mod, w)

    hcf, hcb, cc, nmc = scan(pc, c0, nm0)
    hlf, hlb, _, _ = scan(pl_, cc, nmc)
    return (out(hcf, hcb, pc, xc, mc) if ctx_out else None), out(hlf, hlb, pl_, xl, ml)


def kernel(x, c, ctx, c_ctx, mod_w, mod_b, norm_g, ffn_w13, ffn_w2, final_g, rw_mu, rw_wrkv, rw_w0, rw_w1, rw_w2, rw_a0, rw_a1, rw_a2, rw_g1, rw_g2, rw_kk, rw_ka, rw_rk, rw_lnx_g, rw_lnx_b, rw_wo, rt_win, rt_log_gamma, rt_wo, ml_win, ml_conv_w, ml_conv_b, ml_wq, ml_wk, ml_wv, ml_wif, ml_bif, ml_skip, ml_norm_g, ml_wo):
    b = x.shape[0]
    depth = mod_w.shape[0]
    assert b < SUB
    cvec = jnp.concatenate([c, c_ctx[None], jnp.zeros((SUB - b - 1, D), F32)], axis=0)
    mods = _mod_call(cvec, mod_w, mod_b)
    w13 = ffn_w13.astype(BF16)
    w2 = ffn_w2.astype(BF16)
    xl, xc = x, ctx
    for i in range(depth):
        kind, s = i % 3, i // 3
        last = i == depth - 1
        ml = mods[i, :b].reshape(b, N_MOD, D)
        mc = jnp.broadcast_to(mods[i, b].reshape(1, N_MOD, D), (b, N_MOD, D))
        ng = norm_g[i]
        xl = _ffn_call(xl, ml, ng, w13, w2, i, 0)
        xc = _ffn_call(xc, mc, ng, w13, w2, i, 0)
        if kind == 0:
            w = _rwkv_weights(rw_mu[s], rw_wrkv[s], rw_w0[s], rw_w1[s], rw_w2[s], rw_a0[s], rw_a1[s], rw_a2[s],
                              rw_g1[s], rw_g2[s], rw_kk[s], rw_ka[s], rw_rk[s], rw_lnx_g[s], rw_lnx_b[s], rw_wo[s])
            xc, xl = _rwkv_mixer(xc, xl, mc, ml, ng, w, not last)
        elif kind == 1:
            xc, xl = _retention_mixer(xc, xl, mc, ml, ng, rt_win[s], rt_log_gamma[s], rt_wo[s], not last)
        else:
            w = _mlstm_weights(ml_win[s], ml_conv_w[s], ml_conv_b[s], ml_wq[s], ml_wk[s], ml_wv[s], ml_wif[s],
                               ml_bif[s], ml_skip[s], ml_norm_g[s], ml_wo[s])
            xc, xl = _mlstm_mixer(xc, xl, mc, ml, ng, w, not last)
        xl = _ffn_call(xl, ml, ng, w13, w2, i, 2, final_g if last else None)
        if not last:
            xc = _ffn_call(xc, mc, ng, w13, w2, i, 2)
    return xl
```

```python
import functools
import math

import jax
import jax.numpy as jnp
from jax import lax
from jax.experimental import pallas as pl
from jax.experimental.pallas import tpu as pltpu

F32 = jnp.float32
BF16 = jnp.bfloat16
HI = lax.Precision.HIGHEST

LANE = 128
SUB = 8
VMEM_LIMIT = 56 * 1024 * 1024

D = 1024
D_FF = 2816
N_MOD = 9
NORM_EPS = 1e-6
GRID_W = 64
RW_HEAD = 64
RW_LNX_EPS = 64e-5
RW_CHUNK = 64
RW_CHUNKS_PER_STEP = 4
RW_GROUP = 4
RW_GROUPS = D // (RW_GROUP * RW_HEAD)
RW_STATE = (RW_GROUPS, RW_GROUP * RW_HEAD, RW_GROUP * RW_HEAD)
RT_HEADS = 4
RT_DK = D // RT_HEADS
RT_DV = 2 * D // RT_HEADS
ROPE_BASE = 10000.0
RT_CHUNK = 256
ML_INNER = 2 * D
ML_HEADS = 4
ML_DH = ML_INNER // ML_HEADS
ML_BLOCK = 4
ML_NORM_EPS = 1e-5
ML_CHUNK = 256
ML_TILES = ML_INNER // LANE

ROW_TILE = 256
FFN_ROW_TILE = 512

NT = (((1,), (1,)), ((), ()))
TN = (((0,), (0,)), ((), ()))


def _params(*sem):
    return pltpu.CompilerParams(dimension_semantics=sem, vmem_limit_bytes=VMEM_LIMIT)


def _sigmoid(x):
    return 1.0 / (1.0 + jnp.exp(-x))


def _silu(x):
    return x * _sigmoid(x)


def _rms(x, eps):
    return x * lax.rsqrt(jnp.mean(x * x, axis=-1, keepdims=True) + eps)


def _modnorm(x, g, shift, scale):
    return _rms(x, NORM_EPS) * g * (1.0 + scale) + shift


def _mm(a, b):
    return jnp.dot(a.astype(BF16), b, preferred_element_type=F32)


def _iota(shape, dim):
    return lax.broadcasted_iota(jnp.int32, shape, dim)


def _masked_cumsum(mask01, a, parts):
    out = 0.0
    for _ in range(parts):
        piece = a.astype(BF16)
        out = out + jnp.dot(mask01, piece, preferred_element_type=F32)
        a = a - piece.astype(F32)
    return out


def _mod_body(c_ref, w_ref, b_ref, o_ref):
    s = _silu(c_ref[...])
    o_ref[...] = jnp.dot(s, w_ref[...], precision=HI, preferred_element_type=F32) + b_ref[...]


def _mod_call(cvec, mod_w, mod_b):
    depth, _, n = mod_w.shape
    tn = D
    return pl.pallas_call(
        _mod_body,
        grid=(depth, n // tn),
        in_specs=[pl.BlockSpec((SUB, D), lambda l, j: (0, 0)),
                  pl.BlockSpec((None, D, tn), lambda l, j: (l, 0, j)),
                  pl.BlockSpec((None, 1, tn), lambda l, j: (l, 0, j))],
        out_specs=pl.BlockSpec((None, SUB, tn), lambda l, j: (l, 0, j)),
        out_shape=jax.ShapeDtypeStruct((depth, SUB, n), F32),
        compiler_params=_params("parallel", "parallel"),
        name="adaln_mod",
    )(cvec, mod_w, mod_b.reshape(depth, 1, n))


def _ffn_body(x_ref, mod_ref, ng_ref, w13_ref, w2_ref, *rest, sub, final):
    if final:
        fg_ref, o_ref = rest
    else:
        (o_ref,) = rest
    x = x_ref[...]
    y = _modnorm(x, ng_ref[sub:sub + 1], mod_ref[3 * sub:3 * sub + 1], mod_ref[3 * sub + 1:3 * sub + 2])
    y = y.astype(BF16)
    a = jnp.dot(y, w13_ref[:, :D_FF], preferred_element_type=F32)
    b = jnp.dot(y, w13_ref[:, D_FF:], preferred_element_type=F32)
    out = x + 0.5 * mod_ref[3 * sub + 2:3 * sub + 3] * _mm(_silu(a) * b, w2_ref[...])
    if final:
        out = _rms(out, NORM_EPS) * fg_ref[...]
    o_ref[...] = out


def _ffn_call(x, mod, ng, w13, w2, layer, sub, final_g=None):
    b, t, _ = x.shape
    tm = min(FFN_ROW_TILE, t)
    half = sub // 2
    in_specs = [pl.BlockSpec((None, tm, D), lambda bi, i: (bi, i, 0)),
                pl.BlockSpec((None, N_MOD, D), lambda bi, i: (bi, 0, 0)),
                pl.BlockSpec((3, D), lambda bi, i: (0, 0)),
                pl.BlockSpec((None, None, D, 2 * D_FF), lambda bi, i: (layer, half, 0, 0),
                             pipeline_mode=pl.Buffered(1)),
                pl.BlockSpec((None, None, D_FF, D), lambda bi, i: (layer, half, 0, 0),
                             pipeline_mode=pl.Buffered(1))]
    args = [x, mod, ng, w13, w2]
    if final_g is not None:
        in_specs.append(pl.BlockSpec((1, D), lambda bi, i: (0, 0)))
        args.append(final_g.reshape(1, D))
    return pl.pallas_call(
        functools.partial(_ffn_body, sub=sub, final=final_g is not None),
        grid=(b, t // tm),
        in_specs=in_specs,
        out_specs=pl.BlockSpec((None, tm, D), lambda bi, i: (bi, i, 0)),
        out_shape=jax.ShapeDtypeStruct((b, t, D), F32),
        compiler_params=_params("parallel", "parallel"),
        name="half_ffn",
    )(*args)


def _row_specs(t, tm):
    nb = tm // SUB
    last = t // SUB - 1
    return [pl.BlockSpec((None, tm, D), lambda bi, i: (bi, i, 0)),
            pl.BlockSpec((None, SUB, D), lambda bi, i: (bi, jnp.maximum(i * nb - 1, 0), 0)),
            pl.BlockSpec((None, SUB, D), lambda bi, i: (bi, jnp.minimum((i + 1) * nb, last), 0))]


def _full(shape):
    nd = len(shape)
    return pl.BlockSpec(shape, lambda bi, i: (0,) * nd)


def _tile(width, tm):
    return pl.BlockSpec((None, tm, width), lambda bi, i: (bi, i, 0))


def _tile2(width, tm):
    return pl.BlockSpec((2, None, tm, width), lambda bi, i: (0, bi, i, 0))


_MOD_SPEC = pl.BlockSpec((None, N_MOD, D), lambda bi, i: (bi, 0, 0))


def _chunk_index(d, c, nc):
    return c + d * (nc - 1 - 2 * c)


def _group_ones():
    r = jnp.right_shift(_iota((LANE, LANE), 0), 6)
    c = jnp.right_shift(_iota((LANE, LANE), 1), 6)
    return (r == c).astype(BF16)


def _head_sum64(x, ones):
    parts = [_mm(x[:, t * LANE:(t + 1) * LANE], ones) for t in range(x.shape[1] // LANE)]
    return jnp.concatenate(parts, axis=-1)


def _rw_proj_body(x_ref, xp_ref, xn_ref, mod_ref, ng_ref, vec_ref, wrkv_ref, wd_ref, w2_ref, a2_ref, g2_ref,
                  r_ref, v_ref, g_ref, kk_ref, bv_ref, lw_ref, kd_ref, b_ref):
    i = pl.program_id(1)
    tm = x_ref.shape[0]
    g, shift, scale = ng_ref[1:2], mod_ref[3:4], mod_ref[4:5]
    h = _modnorm(x_ref[...], g, shift, scale)
    hp = jnp.where(i > 0, _modnorm(xp_ref[SUB - 1:SUB], g, shift, scale), 0.0)
    hn = jnp.where(i < pl.num_programs(1) - 1, _modnorm(xn_ref[0:1], g, shift, scale), 0.0)
    row = _iota((tm, 1), 0)
    h_prev = jnp.where(row == 0, hp, pltpu.roll(h, 1, axis=0))
    h_next = jnp.where(row == tm - 1, hn, pltpu.roll(h, tm - 1, axis=0))
    xx = 0.5 * (h_prev + h_next) - h

    def mix(n):
        return (h + xx * vec_ref[n:n + 1]).astype(BF16)

    wl = _mm(jnp.tanh(jnp.dot(mix(1), wd_ref[0], preferred_element_type=F32)), w2_ref[...])
    al = _mm(jnp.dot(mix(4), wd_ref[1], preferred_element_type=F32), a2_ref[...])
    k = jnp.dot(mix(2), wrkv_ref[1], preferred_element_type=F32)
    gg = _mm(_sigmoid(jnp.dot(mix(5), wd_ref[2], preferred_element_type=F32)), g2_ref[...])
    r = jnp.dot(mix(0), wrkv_ref[0], preferred_element_type=F32)
    v = jnp.dot(mix(3), wrkv_ref[2], preferred_element_type=F32)

    ones = _group_ones()
    kkr = k * vec_ref[10:11]
    n2 = _head_sum64(kkr * kkr, ones)
    kk = kkr / jnp.maximum(jnp.sqrt(n2), 1e-12)
    kd_sum = 0.0
    for d in range(2):
        lw_ref[d] = -math.exp(-0.5) * _sigmoid(vec_ref[6 + d:7 + d] + wl[:, d * D:(d + 1) * D])
        a = _sigmoid(vec_ref[8 + d:9 + d] + al[:, d * D:(d + 1) * D])
        kd = k * (1.0 + (a - 1.0) * vec_ref[11:12])
        kd_ref[d] = kd
        b_ref[d] = kk * a
        kd_sum = kd_sum + kd
    bv_ref[...] = (_head_sum64(r * kd_sum * vec_ref[12:13], ones) * v).astype(BF16)
    r_ref[...] = r
    v_ref[...] = v.astype(BF16)
    g_ref[...] = gg.astype(BF16)
    kk_ref[...] = kk


def _rw_proj_call(x, mod, ng, w):
    b, t, _ = x.shape
    tm = min(ROW_TILE, t)
    one = jax.ShapeDtypeStruct((b, t, D), F32)
    half = jax.ShapeDtypeStruct((b, t, D), BF16)
    two = jax.ShapeDtypeStruct((2, b, t, D), F32)
    return pl.pallas_call(
        _rw_proj_body,
        grid=(b, t // tm),
        in_specs=_row_specs(t, tm) + [_MOD_SPEC, _full((3, D)), _full((16, D)), _full((3, D, D)),
                                      _full((3, D, LANE)), _full((LANE, 2 * D)), _full((LANE, 2 * D)),
                                      _full((LANE, D))],
        out_specs=[_tile(D, tm)] * 5 + [_tile2(D, tm)] * 3,
        out_shape=[one, half, half, one, half] + [two] * 3,
        compiler_params=_params("parallel", "parallel"),
        name="rwkv_proj",
    )(x, x, x, mod, ng, w["vec"], w["wrkv"], w["wd"], w["w2"], w["a2"], w["g2"])


def _rw_scan_body(rf_ref, rb_ref, vf_ref, vb_ref, kkf_ref, kkb_ref, lwf_ref, lwb_ref, kdf_ref, kdb_ref,
                  bf_ref, bb_ref, s0_ref, yf_ref, yb_ref, s_ref):
    c = pl.program_id(1)
    n = RW_CHUNK
    subs = rf_ref.shape[0] // n
    refs = ((rf_ref, vf_ref, kkf_ref, lwf_ref, kdf_ref, bf_ref, yf_ref),
            (rb_ref, vb_ref, kkb_ref, lwb_ref, kdb_ref, bb_ref, yb_ref))

    @pl.when(c == 0)
    def _():
        s_ref[...] = s0_ref[...]

    gl = RW_GROUP * RW_HEAD
    shift = int(math.log2(RW_HEAD))
    tri = _iota((n, n), 0) - _iota((n, n), 1)
    rel = _iota((n, gl), 0) - jnp.bitwise_and(_iota((n, gl), 1), RW_HEAD - 1)
    before = [(tri >= 0).astype(BF16), (tri <= 0).astype(BF16)]
    strict = [rel > 0, rel < 0]
    incl = [rel >= 0, rel <= 0]
    eye = (rel == 0).astype(F32)
    row_head = jnp.right_shift(_iota((gl, 2 * gl), 0), shift)
    col_head = jnp.right_shift(jnp.bitwise_and(_iota((gl, 2 * gl), 1), gl - 1), shift)
    same2 = (row_head == col_head).astype(BF16)
    same = same2[:, :gl]
    same32 = same.astype(F32)
    low = _iota((1, LANE), 1) < RW_HEAD

    def bdiag(m):
        return jnp.concatenate([m] * RW_GROUP, axis=0) * same

    groups = range(D // gl)
    sls = [slice(q * gl, (q + 1) * gl) for q in groups]
    dirs = range(2)
    rows, ab, rb, vb, rhs, hk_t, keep = {}, {}, {}, {}, {}, {}, {}
    a_rb, akv, inv = {}, {}, {}

    def prepare(j):
        for d in dirs:
            r_ref, v_ref, kk_ref, lw_ref, kd_ref, b_ref, _ = refs[d]
            rows[d, j] = pl.ds((subs - 1 - j if d else j) * n, n)
            rw = rows[d, j]
            lw = lw_ref[rw, :]
            cum = _masked_cumsum(before[d], lw, 3)
            tot = jnp.sum(lw, axis=0, keepdims=True)
            kd, bb = kd_ref[rw, :], b_ref[rw, :]
            rt = r_ref[rw, :] * jnp.exp(cum)
            at = -kk_ref[rw, :] * jnp.exp(cum - lw)
            e_out = jnp.exp(-cum)
            kt, bt = kd * e_out, bb * e_out
            e_tail = jnp.exp(tot - cum)
            kh, bh = kd * e_tail, bb * e_tail
            v = v_ref[rw, :]
            etot = jnp.exp(tot)
            yield
            for q in groups:
                sl = sls[q]
                t = jnp.concatenate([bt[:, sl], kt[:, sl]], axis=0).T
                sw = pltpu.roll(t, RW_HEAD, axis=1)
                b2 = jnp.where(low, t, sw).astype(BF16)
                k2 = jnp.where(low, sw, t).astype(BF16)
                rhs[d, j, q] = jnp.concatenate([b2, b2, k2, k2], axis=1) * same2
                ab[d, j, q] = at[:, sl].astype(BF16)
                rb[d, j, q] = rt[:, sl].astype(BF16)
                vb[d, j, q] = v[:, sl].astype(BF16)
                hk_t[d, j, q] = jnp.concatenate([bh[:, sl], kh[:, sl]], axis=0).T.astype(BF16)
                keep[d, j, q] = jnp.concatenate([jnp.broadcast_to(etot[:, sl], (LANE, gl)).T] * 2, axis=1)
                if q % 2:
                    yield

    def invert(j):
        chains = [(d, j, q) for d in dirs for q in groups]
        prod = {k: jnp.dot(jnp.concatenate([ab[k], rb[k]], axis=0), rhs[k], preferred_element_type=F32)
                for k in chains}
        yield
        a_ab = {k: jnp.where(strict[k[0]], prod[k][:n, :gl], 0.0) for k in chains}
        for k in chains:
            a_rb[k] = jnp.where(incl[k[0]], prod[k][n:, :gl], 0.0).astype(BF16)
            a_k = jnp.concatenate([jnp.where(strict[k[0]], prod[k][:n, gl:], 0.0),
                                   jnp.where(incl[k[0]], prod[k][n:, gl:], 0.0)], axis=0)
            akv[k] = _mm(a_k, bdiag(vb[k]))
        acc = {k: eye + a_ab[k] for k in chains}
        pw = {k: a_ab[k].astype(BF16) for k in chains}
        pw = {k: jnp.dot(pw[k], bdiag(pw[k]), preferred_element_type=F32).astype(BF16) for k in chains}
        yield
        for _ in range(int(math.log2(n)) - 2):
            both = {k: jnp.dot(jnp.concatenate([pw[k], acc[k].astype(BF16)], axis=0), bdiag(pw[k]),
                               preferred_element_type=F32) for k in chains}
            yield
            acc = {k: acc[k] + both[k][n:] for k in chains}
            pw = {k: both[k][:n].astype(BF16) for k in chains}
        for k in chains:
            inv[k] = (acc[k] + _mm(acc[k], bdiag(pw[k]))).astype(BF16)
        yield

    heads = [(d, q) for d in dirs for q in groups]
    st = {k: s_ref[k[0], k[1]] for k in heads}

    def advance(j):
        at_j = lambda tab, k: tab[k[0], j, k[1]]
        ss = {k: _mm(jnp.concatenate([at_j(ab, k), at_j(rb, k)], axis=0), st[k].astype(BF16)) + at_j(akv, k)
              for k in heads}
        yield
        u = {k: jnp.dot(at_j(inv, k), bdiag(ss[k][:n].astype(BF16)), preferred_element_type=F32).astype(BF16)
             for k in heads}
        yield
        upd = {k: jnp.dot(at_j(hk_t, k), jnp.concatenate([u[k], at_j(vb, k)], axis=0), preferred_element_type=F32)
               for k in heads}
        yield
        y = {k: ss[k][n:] + jnp.dot(at_j(a_rb, k), bdiag(u[k]), preferred_element_type=F32) for k in heads}
        yield
        for k in heads:
            st[k] = st[k] * at_j(keep, k) + upd[k] * same32
        for d in dirs:
            refs[d][-1][rows[d, j], :] = jnp.concatenate([y[d, q] for q in groups], axis=-1).astype(BF16)
        yield

    chunks = range(subs)
    for phase in [[prepare(j)] for j in chunks] + [[invert(j) for j in chunks]] + [[advance(j)] for j in chunks]:
        while phase:
            phase = [g for g in phase if next(g, phase) is not phase]
    for k in heads:
        s_ref[k[0], k[1]] = st[k]


def _rw_scan_call(r, v, kk, lw, kd, bb, s0):
    b, t, _ = r.shape
    n = RW_CHUNK * RW_CHUNKS_PER_STEP
    nc = t // n
    fwd = pl.BlockSpec((None, n, D), lambda bi, c: (bi, c, 0))
    bwd = pl.BlockSpec((None, n, D), lambda bi, c: (bi, nc - 1 - c, 0))
    fwd2 = pl.BlockSpec((None, None, n, D), lambda bi, c: (0, bi, c, 0))
    bwd2 = pl.BlockSpec((None, None, n, D), lambda bi, c: (1, bi, nc - 1 - c, 0))
    st = pl.BlockSpec((2, None) + RW_STATE, lambda bi, c: (0, bi, 0, 0, 0))
    seq = jax.ShapeDtypeStruct((b, t, D), BF16)
    return pl.pallas_call(
        _rw_scan_body,
        grid=(b, nc),
        in_specs=[fwd, bwd, fwd, bwd, fwd, bwd, fwd2, bwd2, fwd2, bwd2, fwd2, bwd2, st],
        out_specs=[fwd, bwd, st],
        out_shape=[seq, seq, jax.ShapeDtypeStruct(s0.shape, F32)],
        compiler_params=_params("parallel", "arbitrary"),
        name="rwkv_scan",
    )(r, r, v, v, kk, kk, lw, lw, kd, kd, bb, bb, s0)


def _rw_out_body(yf_ref, yb_ref, bv_ref, g_ref, x_ref, mod_ref, vec_ref, wo_ref, o_ref):
    ones = _group_ones()
    y = yf_ref[...].astype(F32) + yb_ref[...].astype(F32)
    yc = y - _head_sum64(y, ones) * (1.0 / RW_HEAD)
    var = _head_sum64(yc * yc, ones) * (1.0 / RW_HEAD)
    yh = yc * lax.rsqrt(var + RW_LNX_EPS) * vec_ref[0:1] + vec_ref[1:2]
    o = _mm((yh + bv_ref[...]) * g_ref[...], wo_ref[...])
    o_ref[...] = x_ref[...] + mod_ref[5:6] * o


def _rw_out_call(yf, yb, bv, g, x, mod, w):
    b, t, _ = x.shape
    tm = min(ROW_TILE, t)
    return pl.pallas_call(
        _rw_out_body,
        grid=(b, t // tm),
        in_specs=[_tile(D, tm)] * 5 + [_MOD_SPEC, _full((SUB, D)), _full((D, D))],
        out_specs=_tile(D, tm),
        out_shape=jax.ShapeDtypeStruct((b, t, D), F32),
        compiler_params=_params("parallel", "parallel"),
        name="rwkv_readout",
    )(yf, yb, bv, g, x, mod, w["ovec"], w["wo"])


def _rwkv_mixer(xc, xl, mc, ml, ng, w, ctx_out):
    b = xl.shape[0]
    pc = _rw_proj_call(xc, mc, ng, w)
    pl_ = _rw_proj_call(xl, ml, ng, w)
    s0 = jnp.zeros((2, b) + RW_STATE, F32)

    def scan(p, s):
        r, v, _, kk, _, lw, kd, bb = p
        return _rw_scan_call(r, v, kk, lw, kd, bb, s)

    def out(yf, yb, p, x, mod):
        return _rw_out_call(yf, yb, p[4], p[2], x, mod, w)

    ycf, ycb, sc = scan(pc, s0)
    ylf, ylb, _ = scan(pl_, sc)
    return (out(ycf, ycb, pc, xc, mc) if ctx_out else None), out(ylf, ylb, pl_, xl, ml)


def _rwkv_weights(mu, w_rkv, w0, w1, w2, a0, a1, a2, g1, g2, k_k, k_a, r_k, lnx_g, lnx_b, w_o):
    zeros = jnp.zeros((RW_HEAD, D), F32)

    def two_dir(m):
        return jnp.concatenate([jnp.concatenate([m[0], zeros], axis=1),
                                jnp.concatenate([zeros, m[1]], axis=1)], axis=0).astype(BF16)

    vec = jnp.concatenate([mu, w0, a0, k_k[None], k_a[None], r_k.reshape(1, D), jnp.zeros((3, D), F32)], axis=0)
    ovec = jnp.concatenate([lnx_g[None], lnx_b[None], jnp.zeros((SUB - 2, D), F32)], axis=0)
    wd = jnp.stack([jnp.concatenate([w1[0], w1[1]], axis=1), jnp.concatenate([a1[0], a1[1]], axis=1), g1])
    return {"vec": vec, "ovec": ovec, "wrkv": w_rkv.astype(BF16), "wd": wd.astype(BF16), "w2": two_dir(w2),
            "a2": two_dir(a2), "g2": g2.astype(BF16), "wo": w_o.astype(BF16)}


def _rt_proj_body(x_ref, mod_ref, ng_ref, win_ref, *rest, rope):
    if rope:
        cos_ref, sin_ref, q_ref, k_ref, v_ref, g_ref = rest
    else:
        q_ref, k_ref, v_ref, g_ref = rest
    h = _modnorm(x_ref[...], ng_ref[1:2], mod_ref[3:4], mod_ref[4:5]).astype(BF16)
    q = jnp.dot(h, win_ref[:, 0:D], preferred_element_type=F32)
    k = jnp.dot(h, win_ref[:, D:2 * D], preferred_element_type=F32) * RT_DK ** -0.5
    if rope:
        cos = jnp.concatenate([cos_ref[...]] * RT_HEADS, axis=-1)
        sin = jnp.concatenate([sin_ref[...]] * RT_HEADS, axis=-1)
        low = jnp.bitwise_and(_iota((1, D), 1), LANE - 1) < LANE // 2

        def rot(t):
            partner = jnp.where(low, pltpu.roll(t, D - LANE // 2, axis=1), pltpu.roll(t, LANE // 2, axis=1))
            return t * cos + partner * sin

        q, k = rot(q), rot(k)
    q_ref[...] = q.astype(BF16)
    k_ref[...] = k.astype(BF16)
    v_ref[...] = jnp.dot(h, win_ref[:, 2 * D:4 * D], preferred_element_type=F32).astype(BF16)
    g_ref[...] = jnp.dot(h, win_ref[:, 4 * D:6 * D], preferred_element_type=F32).astype(BF16)


def _rt_proj_call(x, mod, ng, win, tables):
    b, t, _ = x.shape
    tm = min(ROW_TILE, t)
    in_specs = [_tile(D, tm), _MOD_SPEC, _full((3, D)), _full((D, 6 * D))]
    args = [x, mod, ng, win]
    if tables is not None:
        in_specs += [pl.BlockSpec((tm, RT_DK), lambda bi, i: (i, 0))] * 2
        args += list(tables)
    return pl.pallas_call(
        functools.partial(_rt_proj_body, rope=tables is not None),
        grid=(b, t // tm),
        in_specs=in_specs,
        out_specs=[_tile(D, tm), _tile(D, tm), _tile(2 * D, tm), _tile(2 * D, tm)],
        out_shape=[jax.ShapeDtypeStruct((b, t, D), BF16), jax.ShapeDtypeStruct((b, t, D), BF16),
                   jax.ShapeDtypeStruct((b, t, 2 * D), BF16), jax.ShapeDtypeStruct((b, t, 2 * D), BF16)],
        compiler_params=_params("parallel", "parallel"),
        name="retention_proj",
    )(*args)


def _rt_scan_body(lg_ref, qf_ref, qb_ref, kf_ref, kb_ref, vf_ref, vb_ref, s0_ref, of_ref, ob_ref, s_ref):
    c = pl.program_id(1)
    n = qf_ref.shape[0]
    refs = ((qf_ref, kf_ref, vf_ref, of_ref), (qb_ref, kb_ref, vb_ref, ob_ref))

    @pl.when(c == 0)
    def _():
        s_ref[...] = s0_ref[...]

    tri = _iota((n, n), 0) - _iota((n, n), 1)
    row = _iota((n, 1), 0)
    rel = [tri, -tri]
    pos = [row.astype(F32), (n - 1 - row).astype(F32)]
    heads = [(d, h) for d in range(2) for h in range(RT_HEADS)]
    lg = {k: -jnp.abs(lg_ref[k[0], k[1]]) for k in heads}
    qs = {k: refs[k[0]][0][:, k[1] * RT_DK:(k[1] + 1) * RT_DK] for k in heads}
    ks = {k: refs[k[0]][1][:, k[1] * RT_DK:(k[1] + 1) * RT_DK] for k in heads}
    vs = {k: refs[k[0]][2][:, k[1] * RT_DV:(k[1] + 1) * RT_DV] for k in heads}
    st = {k: s_ref[k[0], k[1]] for k in heads}
    qk = {k: lax.dot_general(qs[k], ks[k], NT, preferred_element_type=F32) for k in heads}
    qr = {k: _mm(qs[k], st[k].astype(BF16)) for k in heads}
    kv = {k: lax.dot_general(ks[k], (jnp.exp((n - 1.0 - pos[k[0]]) * lg[k]) * vs[k].astype(F32)).astype(BF16), TN,
                             preferred_element_type=F32) for k in heads}
    for k in heads:
        s_ref[k[0], k[1]] = jnp.exp(jnp.full((1, 1), float(n), F32) * lg[k]) * st[k] + kv[k]
    for k in heads:
        d, h = k
        dist = jnp.maximum(rel[d], 0).astype(F32)
        s = qk[k] * jnp.where(rel[d] >= 0, jnp.exp(dist * lg[k]), 0.0)
        o = _mm(s, vs[k]) + jnp.exp((pos[d] + 1.0) * lg[k]) * qr[k]
        refs[d][3][:, h * RT_DV:(h + 1) * RT_DV] = o.astype(BF16)


def _rt_scan_call(lg, q, k, v, s0):
    b, t, _ = q.shape
    n = RT_CHUNK
    nc = t // n

    def fwd(width):
        return pl.BlockSpec((None, n, width), lambda bi, c: (bi, c, 0))

    def bwd(width):
        return pl.BlockSpec((None, n, width), lambda bi, c: (bi, nc - 1 - c, 0))

    st = pl.BlockSpec((2, None, RT_HEADS, RT_DK, RT_DV), lambda bi, c: (0, bi, 0, 0, 0))
    seq = jax.ShapeDtypeStruct((b, t, 2 * D), BF16)
    return pl.pallas_call(
        _rt_scan_body,
        grid=(b, nc),
        in_specs=[pl.BlockSpec(memory_space=pltpu.SMEM), fwd(D), bwd(D), fwd(D), bwd(D), fwd(2 * D), bwd(2 * D), st],
        out_specs=[fwd(2 * D), bwd(2 * D), st],
        out_shape=[seq, seq, jax.ShapeDtypeStruct(s0.shape, F32)],
        compiler_params=_params("parallel", "arbitrary"),
        name="retention_scan",
    )(lg, q, q, k, k, v, v, s0)


def _rt_out_body(of_ref, ob_ref, g_ref, x_ref, mod_ref, wo_ref, out_ref):
    o = of_ref[...].astype(F32) + ob_ref[...].astype(F32)
    on = jnp.concatenate([_rms(o[:, h * RT_DV:(h + 1) * RT_DV], NORM_EPS) for h in range(RT_HEADS)], axis=-1)
    out_ref[...] = x_ref[...] + mod_ref[5:6] * _mm(_silu(g_ref[...].astype(F32)) * on, wo_ref[...])


def _rt_out_call(of, ob, g, x, mod, wo):
    b, t, _ = x.shape
    tm = min(ROW_TILE, t)
    return pl.pallas_call(
        _rt_out_body,
        grid=(b, t // tm),
        in_specs=[_tile(2 * D, tm)] * 3 + [_tile(D, tm), _MOD_SPEC, _full((2 * D, D))],
        out_specs=_tile(D, tm),
        out_shape=jax.ShapeDtypeStruct((b, t, D), F32),
        compiler_params=_params("parallel", "parallel"),
        name="retention_readout",
    )(of, ob, g, x, mod, wo)


def _rope_tables(t):
    pos = jnp.arange(t, dtype=jnp.int32)
    quarter = RT_DK // 4
    freqs = ROPE_BASE ** (-jnp.arange(quarter, dtype=F32) / quarter)
    ar = (pos // GRID_W).astype(F32)[:, None] * freqs
    ac = (pos % GRID_W).astype(F32)[:, None] * freqs
    cos = jnp.concatenate([jnp.cos(ar), jnp.cos(ar), jnp.cos(ac), jnp.cos(ac)], axis=-1)
    sin = jnp.concatenate([-jnp.sin(ar), jnp.sin(ar), -jnp.sin(ac), jnp.sin(ac)], axis=-1)
    return cos, sin


def _retention_mixer(xc, xl, mc, ml, ng, w_in, log_gamma, w_o, ctx_out):
    b, t, _ = xl.shape
    win, wo = w_in.astype(BF16), w_o.astype(BF16)
    qc, kc, vc, gc = _rt_proj_call(xc, mc, ng, win, None)
    ql, kl, vl, gl = _rt_proj_call(xl, ml, ng, win, _rope_tables(t))
    s0 = jnp.zeros((2, b, RT_HEADS, RT_DK, RT_DV), F32)
    ocf, ocb, sc = _rt_scan_call(log_gamma, qc, kc, vc, s0)
    olf, olb, _ = _rt_scan_call(log_gamma, ql, kl, vl, sc)
    return ((_rt_out_call(ocf, ocb, gc, xc, mc, wo) if ctx_out else None),
            _rt_out_call(olf, olb, gl, xl, ml, wo))


def _ml_proj_body(x_ref, xp_ref, xn_ref, mod_ref, ng_ref, win_ref, cv_ref, wqk_ref, wv_ref, wif_ref, bif_ref,
                  q_ref, k_ref, v_ref, xc_ref, z_ref, gt_ref, y_scr, xm_scr):
    i = pl.program_id(1)
    tm = x_ref.shape[0]
    g, shift, scale = ng_ref[1:2], mod_ref[3:4], mod_ref[4:5]
    y_scr[0:SUB] = _modnorm(xp_ref[...], g, shift, scale).astype(BF16)
    y_scr[SUB:SUB + tm] = _modnorm(x_ref[...], g, shift, scale).astype(BF16)
    y_scr[SUB + tm:] = _modnorm(xn_ref[...], g, shift, scale).astype(BF16)
    xm = jnp.dot(y_scr[...], win_ref[:, 0:ML_INNER], preferred_element_type=F32)
    row = _iota((tm + 2 * SUB, 1), 0)
    lo = jnp.where(i > 0, 0, SUB)
    hi = jnp.where(i < pl.num_programs(1) - 1, tm + 2 * SUB, tm + SUB)
    xm_scr[...] = jnp.where((row >= lo) & (row < hi), xm, 0.0)
    xm_mid = xm_scr[pl.ds(SUB, tm)]
    conv = (xm_scr[pl.ds(SUB - 1, tm)] * cv_ref[0:1] + xm_mid * cv_ref[1:2]
            + xm_scr[pl.ds(SUB + 1, tm)] * cv_ref[2:3] + cv_ref[3:4])
    xc = _silu(conv)
    xc_ref[...] = xc.astype(BF16)
    z_ref[...] = jnp.dot(y_scr[SUB:SUB + tm], win_ref[:, ML_INNER:], preferred_element_type=F32).astype(BF16)
    qs, ks, vs = [], [], []
    for t in range(ML_TILES):
        sl = slice(t * LANE, (t + 1) * LANE)
        qk = _mm(xc[:, sl], wqk_ref[t])
        qs.append(qk[:, :LANE])
        ks.append(qk[:, LANE:])
        vs.append(_mm(xm_mid[:, sl], wv_ref[t]))
    q = jnp.concatenate(qs, axis=-1)
    k = jnp.concatenate(ks, axis=-1)
    v = jnp.concatenate(vs, axis=-1)
    pre = (_mm(q, wif_ref[0:ML_INNER]) + _mm(k, wif_ref[ML_INNER:2 * ML_INNER])
           + _mm(v, wif_ref[2 * ML_INNER:]) + bif_ref[...])
    is_forget = jnp.bitwise_and(_iota((1, LANE), 1), 2 * ML_HEADS - 1) >= ML_HEADS
    log_sig = jnp.minimum(pre, 0.0) - jnp.log(1.0 + jnp.exp(-jnp.abs(pre)))
    gt_ref[...] = jnp.where(is_forget, log_sig, pre)
    q_ref[...] = q.astype(BF16)
    k_ref[...] = (k * ML_DH ** -0.5).astype(BF16)
    v_ref[...] = v.astype(BF16)


def _ml_proj_call(x, mod, ng, w):
    b, t, _ = x.shape
    tm = min(ROW_TILE, t)
    wide = jax.ShapeDtypeStruct((b, t, ML_INNER), BF16)
    return pl.pallas_call(
        _ml_proj_body,
        grid=(b, t // tm),
        in_specs=_row_specs(t, tm) + [_MOD_SPEC, _full((3, D)), _full((D, 2 * ML_INNER)), _full((SUB, ML_INNER)),
                                      _full((ML_TILES, LANE, 2 * LANE)), _full((ML_TILES, LANE, LANE)),
                                      _full((3 * ML_INNER, LANE)), _full((1, LANE))],
        out_specs=[_tile(ML_INNER, tm)] * 5 + [_tile(LANE, tm)],
        out_shape=[wide] * 5 + [jax.ShapeDtypeStruct((b, t, LANE), F32)],
        scratch_shapes=[pltpu.VMEM((tm + 2 * SUB, D), BF16), pltpu.VMEM((tm + 2 * SUB, ML_INNER), F32)],
        compiler_params=_params("parallel", "parallel"),
        name="mlstm_proj",
    )(x, x, x, mod, ng, w["win"], w["cv"], w["wqk"], w["wv"], w["wif"], w["bif"])


def _ml_scan_body(qf_ref, qb_ref, kf_ref, kb_ref, vf_ref, vb_ref, gf_ref, gb_ref, c0_ref, nm0_ref,
                  hf_ref, hb_ref, c_ref, nm_ref):
    c = pl.program_id(1)
    n = qf_ref.shape[0]
    refs = ((qf_ref, kf_ref, vf_ref, gf_ref, hf_ref), (qb_ref, kb_ref, vb_ref, gb_ref, hb_ref))

    @pl.when(c == 0)
    def _():
        c_ref[...] = c0_ref[...]
        nm_ref[...] = nm0_ref[...]

    tri = _iota((n, n), 0) - _iota((n, n), 1)
    valid = [tri >= 0, tri <= 0]
    dirs = range(2)
    gt, cum, tot, gt_t, cum_t = [], [], [], [], []
    for d in dirs:
        gt.append(refs[d][3][...])
        cum.append(_masked_cumsum(valid[d].astype(BF16), gt[d], 3))
        tot.append(jnp.sum(gt[d], axis=0, keepdims=True))
        gt_t.append(gt[d].T)
        cum_t.append(cum[d].T)
    heads = [(d, h) for d in dirs for h in range(ML_HEADS)]
    sls = [slice(h * ML_DH, (h + 1) * ML_DH) for h in range(ML_HEADS)]
    qs = {k: refs[k[0]][0][:, sls[k[1]]] for k in heads}
    ks = {k: refs[k[0]][1][:, sls[k[1]]] for k in heads}
    vs = {k: refs[k[0]][2][:, sls[k[1]]] for k in heads}
    ct = {k: c_ref[k[0], k[1]] for k in heads}
    m_prev = {k: nm_ref[k[0], ML_HEADS + k[1]:ML_HEADS + k[1] + 1, 0:1] for k in heads}
    n_prev = {k: nm_ref[k[0], k[1]:k[1] + 1, :] for k in heads}
    ig_lane = {k: 2 * ML_HEADS * k[0] + k[1] for k in heads}
    lf_lane = {k: 2 * ML_HEADS * k[0] + ML_HEADS + k[1] for k in heads}
    b_col = {k: cum[k[0]][:, lf_lane[k]:lf_lane[k] + 1] for k in heads}
    decay, m_t, inter = {}, {}, {}
    for k in heads:
        d = k[0]
        b_row = cum_t[d][lf_lane[k]:lf_lane[k] + 1]
        ig_row = gt_t[d][ig_lane[k]:ig_lane[k] + 1]
        logd = jnp.where(valid[d], b_col[k] - b_row + ig_row, -jnp.inf)
        m_inter = b_col[k] + m_prev[k]
        m_t[k] = jnp.maximum(m_inter, jnp.max(logd, axis=-1, keepdims=True))
        decay[k] = jnp.exp(logd - m_t[k])
        inter[k] = jnp.exp(m_inter - m_t[k])
    qk = {k: lax.dot_general(qs[k], ks[k], NT, preferred_element_type=F32) for k in heads}
    qc = {k: _mm(qs[k], ct[k].astype(BF16)) for k in heads}
    for k in heads:
        d, h = k
        b_tot = tot[d][:, lf_lane[k]:lf_lane[k] + 1]
        gj = b_tot - b_col[k] + gt[d][:, ig_lane[k]:ig_lane[k] + 1]
        m_end = b_tot + m_prev[k]
        m_new = jnp.maximum(m_end, jnp.max(gj, axis=0, keepdims=True))
        carry = jnp.exp(m_end - m_new)
        wk = jnp.exp(gj - m_new) * ks[k].astype(F32)
        c_ref[d, h] = carry * ct[k] + lax.dot_general(wk.astype(BF16), vs[k], TN, preferred_element_type=F32)
        nm_ref[d, h:h + 1, :] = carry * n_prev[k] + jnp.sum(wk, axis=0, keepdims=True)
        nm_ref[d, ML_HEADS + h:ML_HEADS + h + 1, :] = jnp.broadcast_to(m_new, (1, ML_DH))
    s = {k: qk[k] * decay[k] for k in heads}
    sv = {k: _mm(s[k], vs[k]) for k in heads}
    for k in heads:
        d, h = k
        num = sv[k] + inter[k] * qc[k]
        den = (jnp.sum(s[k], axis=-1, keepdims=True)
               + inter[k] * jnp.sum(qs[k].astype(F32) * n_prev[k], axis=-1, keepdims=True))
        refs[d][4][:, sls[h]] = (num / jnp.maximum(jnp.abs(den), jnp.exp(-m_t[k]))).astype(BF16)


def _ml_scan_call(q, k, v, gt, c0, nm0):
    b, t, _ = q.shape
    n = ML_CHUNK
    nc = t // n

    def fwd(width):
        return pl.BlockSpec((None, n, width), lambda bi, c: (bi, c, 0))

    def bwd(width):
        return pl.BlockSpec((None, n, width), lambda bi, c: (bi, nc - 1 - c, 0))

    cst = pl.BlockSpec((2, None, ML_HEADS, ML_DH, ML_DH), lambda bi, c: (0, bi, 0, 0, 0),
                       pipeline_mode=pl.Buffered(1))
    nst = pl.BlockSpec((2, None, 2 * ML_HEADS, ML_DH), lambda bi, c: (0, bi, 0, 0))
    seq = jax.ShapeDtypeStruct((b, t, ML_INNER), BF16)
    w = ML_INNER
    return pl.pallas_call(
        _ml_scan_body,
        grid=(b, nc),
        in_specs=[fwd(w), bwd(w), fwd(w), bwd(w), fwd(w), bwd(w), fwd(LANE), bwd(LANE), cst, nst],
        out_specs=[fwd(w), bwd(w), cst, nst],
        out_shape=[seq, seq, jax.ShapeDtypeStruct(c0.shape, F32), jax.ShapeDtypeStruct(nm0.shape, F32)],
        compiler_params=_params("parallel", "arbitrary"),
        name="mlstm_scan",
    )(q, q, k, k, v, v, gt, gt, c0, nm0)


def _ml_out_body(hf_ref, hb_ref, xc_ref, z_ref, x_ref, mod_ref, vec_ref, wo_ref, o_ref):
    y = hf_ref[...].astype(F32) + hb_ref[...].astype(F32)
    parts = []
    for h in range(ML_HEADS):
        yh = y[:, h * ML_DH:(h + 1) * ML_DH]
        parts.append(_rms(yh - jnp.mean(yh, axis=-1, keepdims=True), ML_NORM_EPS))
    yn = jnp.concatenate(parts, axis=-1) * vec_ref[0:1]
    o = _mm((yn + vec_ref[1:2] * xc_ref[...].astype(F32)) * _silu(z_ref[...].astype(F32)), wo_ref[...])
    o_ref[...] = x_ref[...] + mod_ref[5:6] * o


def _ml_out_call(hf, hb, xc, z, x, mod, w):
    b, t, _ = x.shape
    tm = min(ROW_TILE, t)
    return pl.pallas_call(
        _ml_out_body,
        grid=(b, t // tm),
        in_specs=[_tile(ML_INNER, tm)] * 4 + [_tile(D, tm), _MOD_SPEC, _full((SUB, ML_INNER)), _full((ML_INNER, D))],
        out_specs=_tile(D, tm),
        out_shape=jax.ShapeDtypeStruct((b, t, D), F32),
        compiler_params=_params("parallel", "parallel"),
        name="mlstm_readout",
    )(hf, hb, xc, z, x, mod, w["ovec"], w["wo"])


def _mlstm_weights(w_in, conv_w, conv_b, w_q, w_k, w_v, w_if, b_if, skip, norm_g, w_o):
    per_tile = LANE // ML_BLOCK
    eye = jnp.eye(per_tile, dtype=F32)

    def block_diag(w):
        wt = w.reshape(ML_TILES, per_tile, ML_BLOCK, ML_BLOCK)
        return jnp.einsum('tgio,gh->tgiho', wt, eye).reshape(ML_TILES, LANE, LANE)

    wqk = jnp.concatenate([block_diag(w_q), block_diag(w_k)], axis=-1).astype(BF16)
    n_gate = 2 * 2 * ML_HEADS
    wif = jnp.transpose(w_if, (1, 0, 2)).reshape(3 * ML_INNER, n_gate)
    wif = jnp.pad(wif, ((0, 0), (0, LANE - n_gate))).astype(BF16)
    bif = jnp.pad(b_if.reshape(1, n_gate), ((0, 0), (0, LANE - n_gate)))
    cv = jnp.concatenate([conv_w, conv_b[None], jnp.zeros((SUB - 4, ML_INNER), F32)], axis=0)
    ovec = jnp.concatenate([norm_g[None], skip[None], jnp.zeros((SUB - 2, ML_INNER), F32)], axis=0)
    return {"win": w_in.astype(BF16), "cv": cv, "wqk": wqk, "wv": block_diag(w_v).astype(BF16), "wif": wif,
            "bif": bif, "ovec": ovec, "wo": w_o.astype(BF16)}


def _mlstm_mixer(xc, xl, mc, ml, ng, w, ctx_out):
    b = xl.shape[0]
    pc = _ml_proj_call(xc, mc, ng, w)
    pl_ = _ml_proj_call(xl, ml, ng, w)
    c0 = jnp.zeros((2, b, ML_HEADS, ML_DH, ML_DH), F32)
    nm0 = jnp.zeros((2, b, 2 * ML_HEADS, ML_DH), F32)

    def scan(p, cs, nms):
        q, k, v, _, _, gt = p
        return _ml_scan_call(q, k, v, gt, cs, nms)

    def out(hf, hb, p, x, mod):
        return _ml_out_call(hf, hb, p[3], p[4], x, mod, w)

    hcf, hcb, cc, nmc = scan(pc, c0, nm0)
    hlf, hlb, _, _ = scan(pl_, cc, nmc)
    return (out(hcf, hcb, pc, xc, mc) if ctx_out else None), out(hlf, hlb, pl_, xl, ml)


def kernel(x, c, ctx, c_ctx, mod_w, mod_b, norm_g, ffn_w13, ffn_w2, final_g, rw_mu, rw_wrkv, rw_w0, rw_w1, rw_w2, rw_a0, rw_a1, rw_a2, rw_g1, rw_g2, rw_kk, rw_ka, rw_rk, rw_lnx_g, rw_lnx_b, rw_wo, rt_win, rt_log_gamma, rt_wo, ml_win, ml_conv_w, ml_conv_b, ml_wq, ml_wk, ml_wv, ml_wif, ml_bif, ml_skip, ml_norm_g, ml_wo):
    b = x.shape[0]
    depth = mod_w.shape[0]
    assert b < SUB
    cvec = jnp.concatenate([c, c_ctx[None], jnp.zeros((SUB - b - 1, D), F32)], axis=0)
    mods = _mod_call(cvec, mod_w, mod_b)
    w13 = ffn_w13.astype(BF16)
    w2 = ffn_w2.astype(BF16)
    xl, xc = x, ctx
    for i in range(depth):
        kind, s = i % 3, i // 3
        last = i == depth - 1
        ml = mods[i, :b].reshape(b, N_MOD, D)
        mc = jnp.broadcast_to(mods[i, b].reshape(1, N_MOD, D), (b, N_MOD, D))
        ng = norm_g[i]
        xl = _ffn_call(xl, ml, ng, w13, w2, i, 0)
        xc = _ffn_call(xc, mc, ng, w13, w2, i, 0)
        if kind == 0:
            w = _rwkv_weights(rw_mu[s], rw_wrkv[s], rw_w0[s], rw_w1[s], rw_w2[s], rw_a0[s], rw_a1[s], rw_a2[s],
                              rw_g1[s], rw_g2[s], rw_kk[s], rw_ka[s], rw_rk[s], rw_lnx_g[s], rw_lnx_b[s], rw_wo[s])
            xc, xl = _rwkv_mixer(xc, xl, mc, ml, ng, w, not last)
        elif kind == 1:
            xc, xl = _retention_mixer(xc, xl, mc, ml, ng, rt_win[s], rt_log_gamma[s], rt_wo[s], not last)
        else:
            w = _mlstm_weights(ml_win[s], ml_conv_w[s], ml_conv_b[s], ml_wq[s], ml_wk[s], ml_wv[s], ml_wif[s],
                               ml_bif[s], ml_skip[s], ml_norm_g[s], ml_wo[s])
            xc, xl = _mlstm_mixer(xc, xl, mc, ml, ng, w, not last)
        xl = _ffn_call(xl, ml, ng, w13, w2, i, 2, final_g if last else None)
        if not last:
            xc = _ffn_call(xc, mc, ng, w13, w2, i, 2)
    return xl
```

```python
import functools
import math

import jax
import jax.numpy as jnp
from jax import lax
from jax.experimental import pallas as pl
from jax.experimental.pallas import tpu as pltpu

F32 = jnp.float32
BF16 = jnp.bfloat16
HI = lax.Precision.HIGHEST

LANE = 128
SUB = 8
V7X_VMEM_BYTES = 64 * 1024 * 1024
VMEM_LIMIT = V7X_VMEM_BYTES * 7 // 8

D = 1024
D_FF = 2816
N_MOD = 9
NORM_EPS = 1e-6
GRID_W = 64
RW_HEAD = 64
RW_LNX_EPS = 64e-5
RW_CHUNK = 64
RW_CHUNKS_PER_STEP = 4
RW_GROUP = 4
RW_GROUPS = D // (RW_GROUP * RW_HEAD)
RW_STATE = (RW_GROUPS, RW_GROUP * RW_HEAD, RW_GROUP * RW_HEAD)
RW_VEC_ROWS = 2 * SUB
RT_HEADS = 4
RT_DK = D // RT_HEADS
RT_DV = 2 * D // RT_HEADS
ROPE_BASE = 10000.0
RT_CHUNK = 256
ML_INNER = 2 * D
ML_HEADS = 4
ML_DH = ML_INNER // ML_HEADS
ML_BLOCK = 4
ML_NORM_EPS = 1e-5
ML_CHUNK = 256
ML_TILES = ML_INNER // LANE

ROW_TILE = 256
FFN_ROW_TILE = 512

NT = (((1,), (1,)), ((), ()))
TN = (((0,), (0,)), ((), ()))


def _params(*sem):
    return pltpu.CompilerParams(dimension_semantics=sem, vmem_limit_bytes=VMEM_LIMIT)


def _sigmoid(x):
    return 1.0 / (1.0 + jnp.exp(-x))


def _silu(x):
    return x * _sigmoid(x)


def _rms(x, eps):
    return x * lax.rsqrt(jnp.mean(x * x, axis=-1, keepdims=True) + eps)


def _modnorm(x, g, shift, scale):
    return _rms(x, NORM_EPS) * g * (1.0 + scale) + shift


def _mm(a, b):
    return jnp.dot(a.astype(BF16), b, preferred_element_type=F32)


def _iota(shape, dim):
    return lax.broadcasted_iota(jnp.int32, shape, dim)


def _masked_cumsum(mask01, a, parts):
    out = 0.0
    for _ in range(parts):
        piece = a.astype(BF16)
        out = out + jnp.dot(mask01, piece, preferred_element_type=F32)
        a = a - piece.astype(F32)
    return out


def _mod_body(c_ref, w_ref, b_ref, o_ref):
    s = _silu(c_ref[...])
    o_ref[...] = jnp.dot(s, w_ref[...], precision=HI, preferred_element_type=F32) + b_ref[...]


def _mod_call(cvec, mod_w, mod_b):
    depth, _, n = mod_w.shape
    tn = D
    return pl.pallas_call(
        _mod_body,
        grid=(depth, n // tn),
        in_specs=[pl.BlockSpec((SUB, D), lambda l, j: (0, 0)),
                  pl.BlockSpec((None, D, tn), lambda l, j: (l, 0, j)),
                  pl.BlockSpec((None, 1, tn), lambda l, j: (l, 0, j))],
        out_specs=pl.BlockSpec((None, SUB, tn), lambda l, j: (l, 0, j)),
        out_shape=jax.ShapeDtypeStruct((depth, SUB, n), F32),
        compiler_params=_params("parallel", "parallel"),
        name="adaln_mod",
    )(cvec, mod_w, mod_b.reshape(depth, 1, n))


def _ffn_body(x_ref, mod_ref, ng_ref, w13_ref, w2_ref, *rest, sub, final):
    if final:
        fg_ref, o_ref = rest
    else:
        (o_ref,) = rest
    x = x_ref[...]
    y = _modnorm(x, ng_ref[sub:sub + 1], mod_ref[3 * sub:3 * sub + 1], mod_ref[3 * sub + 1:3 * sub + 2])
    y = y.astype(BF16)
    a = jnp.dot(y, w13_ref[:, :D_FF], preferred_element_type=F32)
    b = jnp.dot(y, w13_ref[:, D_FF:], preferred_element_type=F32)
    out = x + 0.5 * mod_ref[3 * sub + 2:3 * sub + 3] * _mm(_silu(a) * b, w2_ref[...])
    if final:
        out = _rms(out, NORM_EPS) * fg_ref[...]
    o_ref[...] = out


def _ffn_call(x, mod, ng, w13, w2, layer, sub, final_g=None):
    b, t, _ = x.shape
    tm = min(FFN_ROW_TILE, t)
    half = sub // 2
    in_specs = [pl.BlockSpec((None, tm, D), lambda bi, i: (bi, i, 0)),
                pl.BlockSpec((None, N_MOD, D), lambda bi, i: (bi, 0, 0)),
                pl.BlockSpec((3, D), lambda bi, i: (0, 0)),
                pl.BlockSpec((None, None, D, 2 * D_FF), lambda bi, i: (layer, half, 0, 0),
                             pipeline_mode=pl.Buffered(1)),
                pl.BlockSpec((None, None, D_FF, D), lambda bi, i: (layer, half, 0, 0),
                             pipeline_mode=pl.Buffered(1))]
    args = [x, mod, ng, w13, w2]
    if final_g is not None:
        in_specs.append(pl.BlockSpec((1, D), lambda bi, i: (0, 0)))
        args.append(final_g.reshape(1, D))
    return pl.pallas_call(
        functools.partial(_ffn_body, sub=sub, final=final_g is not None),
        grid=(b, t // tm),
        in_specs=in_specs,
        out_specs=pl.BlockSpec((None, tm, D), lambda bi, i: (bi, i, 0)),
        out_shape=jax.ShapeDtypeStruct((b, t, D), F32),
        compiler_params=_params("parallel", "parallel"),
        name="half_ffn",
    )(*args)


def _row_specs(t, tm):
    nb = tm // SUB
    last = t // SUB - 1
    return [pl.BlockSpec((None, tm, D), lambda bi, i: (bi, i, 0)),
            pl.BlockSpec((None, SUB, D), lambda bi, i: (bi, jnp.maximum(i * nb - 1, 0), 0)),
            pl.BlockSpec((None, SUB, D), lambda bi, i: (bi, jnp.minimum((i + 1) * nb, last), 0))]


def _full(shape):
    nd = len(shape)
    return pl.BlockSpec(shape, lambda bi, i: (0,) * nd)


def _tile(width, tm):
    return pl.BlockSpec((None, tm, width), lambda bi, i: (bi, i, 0))


def _tile2(width, tm):
    return pl.BlockSpec((2, None, tm, width), lambda bi, i: (0, bi, i, 0))


_MOD_SPEC = pl.BlockSpec((None, N_MOD, D), lambda bi, i: (bi, 0, 0))


def _group_ones():
    r = jnp.right_shift(_iota((LANE, LANE), 0), 6)
    c = jnp.right_shift(_iota((LANE, LANE), 1), 6)
    return (r == c).astype(BF16)


def _head_sum64(x, ones):
    parts = [_mm(x[:, t * LANE:(t + 1) * LANE], ones) for t in range(x.shape[1] // LANE)]
    return jnp.concatenate(parts, axis=-1)


def _rw_proj_body(x_ref, xp_ref, xn_ref, mod_ref, ng_ref, vec_ref, wrkv_ref, wd_ref, w2_ref, a2_ref, g2_ref,
                  r_ref, v_ref, g_ref, kk_ref, bv_ref, lw_ref, kd_ref, b_ref):
    i = pl.program_id(1)
    tm = x_ref.shape[0]
    g, shift, scale = ng_ref[1:2], mod_ref[3:4], mod_ref[4:5]
    h = _modnorm(x_ref[...], g, shift, scale)
    hp = jnp.where(i > 0, _modnorm(xp_ref[SUB - 1:SUB], g, shift, scale), 0.0)
    hn = jnp.where(i < pl.num_programs(1) - 1, _modnorm(xn_ref[0:1], g, shift, scale), 0.0)
    row = _iota((tm, 1), 0)
    h_prev = jnp.where(row == 0, hp, pltpu.roll(h, 1, axis=0))
    h_next = jnp.where(row == tm - 1, hn, pltpu.roll(h, tm - 1, axis=0))
    xx = 0.5 * (h_prev + h_next) - h

    def mix(n):
        return (h + xx * vec_ref[n:n + 1]).astype(BF16)

    wl = _mm(jnp.tanh(jnp.dot(mix(1), wd_ref[0], preferred_element_type=F32)), w2_ref[...])
    al = _mm(jnp.dot(mix(4), wd_ref[1], preferred_element_type=F32), a2_ref[...])
    k = jnp.dot(mix(2), wrkv_ref[1], preferred_element_type=F32)
    gg = _mm(_sigmoid(jnp.dot(mix(5), wd_ref[2], preferred_element_type=F32)), g2_ref[...])
    r = jnp.dot(mix(0), wrkv_ref[0], preferred_element_type=F32)
    v = jnp.dot(mix(3), wrkv_ref[2], preferred_element_type=F32)

    ones = _group_ones()
    kkr = k * vec_ref[10:11]
    n2 = _head_sum64(kkr * kkr, ones)
    kk = kkr / jnp.maximum(jnp.sqrt(n2), 1e-12)
    kd_sum = 0.0
    for d in range(2):
        lw_ref[d] = -math.exp(-0.5) * _sigmoid(vec_ref[6 + d:7 + d] + wl[:, d * D:(d + 1) * D])
        a = _sigmoid(vec_ref[8 + d:9 + d] + al[:, d * D:(d + 1) * D])
        kd = k * (1.0 + (a - 1.0) * vec_ref[11:12])
        kd_ref[d] = kd
        b_ref[d] = kk * a
        kd_sum = kd_sum + kd
    bv_ref[...] = (_head_sum64(r * kd_sum * vec_ref[12:13], ones) * v).astype(BF16)
    r_ref[...] = r
    v_ref[...] = v.astype(BF16)
    g_ref[...] = gg.astype(BF16)
    kk_ref[...] = kk


def _rw_proj_call(x, mod, ng, w):
    b, t, _ = x.shape
    tm = min(ROW_TILE, t)
    one = jax.ShapeDtypeStruct((b, t, D), F32)
    half = jax.ShapeDtypeStruct((b, t, D), BF16)
    two = jax.ShapeDtypeStruct((2, b, t, D), F32)
    return pl.pallas_call(
        _rw_proj_body,
        grid=(b, t // tm),
        in_specs=_row_specs(t, tm) + [_MOD_SPEC, _full((3, D)), _full((RW_VEC_ROWS, D)), _full((3, D, D)),
                                      _full((3, D, LANE)), _full((LANE, 2 * D)), _full((LANE, 2 * D)),
                                      _full((LANE, D))],
        out_specs=[_tile(D, tm)] * 5 + [_tile2(D, tm)] * 3,
        out_shape=[one, half, half, one, half] + [two] * 3,
        compiler_params=_params("parallel", "parallel"),
        name="rwkv_proj",
    )(x, x, x, mod, ng, w["vec"], w["wrkv"], w["wd"], w["w2"], w["a2"], w["g2"])


def _rw_scan_body(rf_ref, rb_ref, vf_ref, vb_ref, kkf_ref, kkb_ref, lwf_ref, lwb_ref, kdf_ref, kdb_ref,
                  bf_ref, bb_ref, s0_ref, yf_ref, yb_ref, s_ref):
    c = pl.program_id(1)
    n = RW_CHUNK
    subs = rf_ref.shape[0] // n
    refs = ((rf_ref, vf_ref, kkf_ref, lwf_ref, kdf_ref, bf_ref, yf_ref),
            (rb_ref, vb_ref, kkb_ref, lwb_ref, kdb_ref, bb_ref, yb_ref))

    @pl.when(c == 0)
    def _():
        s_ref[...] = s0_ref[...]

    gl = RW_GROUP * RW_HEAD
    shift = int(math.log2(RW_HEAD))
    tri = _iota((n, n), 0) - _iota((n, n), 1)
    rel = _iota((n, gl), 0) - jnp.bitwise_and(_iota((n, gl), 1), RW_HEAD - 1)
    before = [(tri >= 0).astype(BF16), (tri <= 0).astype(BF16)]
    strict = [rel > 0, rel < 0]
    incl = [rel >= 0, rel <= 0]
    eye = (rel == 0).astype(F32)
    row_head = jnp.right_shift(_iota((gl, 2 * gl), 0), shift)
    col_head = jnp.right_shift(jnp.bitwise_and(_iota((gl, 2 * gl), 1), gl - 1), shift)
    same2 = (row_head == col_head).astype(BF16)
    same = same2[:, :gl]
    same32 = same.astype(F32)
    low = _iota((1, LANE), 1) < RW_HEAD

    def bdiag(m):
        return jnp.concatenate([m] * RW_GROUP, axis=0) * same

    groups = range(D // gl)
    sls = [slice(q * gl, (q + 1) * gl) for q in groups]
    dirs = range(2)
    rows, ab, rb, vb, rhs, hk_t, keep = {}, {}, {}, {}, {}, {}, {}
    a_rb, akv, inv = {}, {}, {}

    def prepare(j):
        for d in dirs:
            r_ref, v_ref, kk_ref, lw_ref, kd_ref, b_ref, _ = refs[d]
            rows[d, j] = pl.ds((subs - 1 - j if d else j) * n, n)
            rw = rows[d, j]
            lw = lw_ref[rw, :]
            cum = _masked_cumsum(before[d], lw, 3)
            tot = jnp.sum(lw, axis=0, keepdims=True)
            kd, bb = kd_ref[rw, :], b_ref[rw, :]
            rt = r_ref[rw, :] * jnp.exp(cum)
            at = -kk_ref[rw, :] * jnp.exp(cum - lw)
            e_out = jnp.exp(-cum)
            kt, bt = kd * e_out, bb * e_out
            e_tail = jnp.exp(tot - cum)
            kh, bh = kd * e_tail, bb * e_tail
            v = v_ref[rw, :]
            etot = jnp.exp(tot)
            yield
            for q in groups:
                sl = sls[q]
                t = jnp.concatenate([bt[:, sl], kt[:, sl]], axis=0).T
                sw = pltpu.roll(t, RW_HEAD, axis=1)
                b2 = jnp.where(low, t, sw).astype(BF16)
                k2 = jnp.where(low, sw, t).astype(BF16)
                rhs[d, j, q] = jnp.concatenate([b2, b2, k2, k2], axis=1) * same2
                ab[d, j, q] = at[:, sl].astype(BF16)
                rb[d, j, q] = rt[:, sl].astype(BF16)
                vb[d, j, q] = v[:, sl].astype(BF16)
                hk_t[d, j, q] = jnp.concatenate([bh[:, sl], kh[:, sl]], axis=0).T.astype(BF16)
                keep[d, j, q] = jnp.concatenate([jnp.broadcast_to(etot[:, sl], (LANE, gl)).T] * 2, axis=1)
                if q % 2:
                    yield

    def invert(j):
        chains = [(d, j, q) for d in dirs for q in groups]
        prod = {k: jnp.dot(jnp.concatenate([ab[k], rb[k]], axis=0), rhs[k], preferred_element_type=F32)
                for k in chains}
        yield
        a_ab = {k: jnp.where(strict[k[0]], prod[k][:n, :gl], 0.0) for k in chains}
        for k in chains:
            a_rb[k] = jnp.where(incl[k[0]], prod[k][n:, :gl], 0.0).astype(BF16)
            a_k = jnp.concatenate([jnp.where(strict[k[0]], prod[k][:n, gl:], 0.0),
                                   jnp.where(incl[k[0]], prod[k][n:, gl:], 0.0)], axis=0)
            akv[k] = _mm(a_k, bdiag(vb[k]))
        acc = {k: eye + a_ab[k] for k in chains}
        pw = {k: a_ab[k].astype(BF16) for k in chains}
        pw = {k: jnp.dot(pw[k], bdiag(pw[k]), preferred_element_type=F32).astype(BF16) for k in chains}
        yield
        for _ in range(int(math.log2(n)) - 2):
            both = {k: jnp.dot(jnp.concatenate([pw[k], acc[k].astype(BF16)], axis=0), bdiag(pw[k]),
                               preferred_element_type=F32) for k in chains}
            yield
            acc = {k: acc[k] + both[k][n:] for k in chains}
            pw = {k: both[k][:n].astype(BF16) for k in chains}
        for k in chains:
            inv[k] = (acc[k] + _mm(acc[k], bdiag(pw[k]))).astype(BF16)
        yield

    heads = [(d, q) for d in dirs for q in groups]
    st = {k: s_ref[k[0], k[1]] for k in heads}

    def advance(j):
        at_j = lambda tab, k: tab[k[0], j, k[1]]
        ss = {k: _mm(jnp.concatenate([at_j(ab, k), at_j(rb, k)], axis=0), st[k].astype(BF16)) + at_j(akv, k)
              for k in heads}
        yield
        u = {k: jnp.dot(at_j(inv, k), bdiag(ss[k][:n].astype(BF16)), preferred_element_type=F32).astype(BF16)
             for k in heads}
        yield
        upd = {k: jnp.dot(at_j(hk_t, k), jnp.concatenate([u[k], at_j(vb, k)], axis=0), preferred_element_type=F32)
               for k in heads}
        yield
        y = {k: ss[k][n:] + jnp.dot(at_j(a_rb, k), bdiag(u[k]), preferred_element_type=F32) for k in heads}
        yield
        for k in heads:
            st[k] = st[k] * at_j(keep, k) + upd[k] * same32
        for d in dirs:
            refs[d][-1][rows[d, j], :] = jnp.concatenate([y[d, q] for q in groups], axis=-1).astype(BF16)
        yield

    chunks = range(subs)
    for phase in [[prepare(j)] for j in chunks] + [[invert(j) for j in chunks]] + [[advance(j)] for j in chunks]:
        while phase:
            phase = [g for g in phase if next(g, phase) is not phase]
    for k in heads:
        s_ref[k[0], k[1]] = st[k]


def _rw_scan_call(r, v, kk, lw, kd, bb, s0):
    b, t, _ = r.shape
    n = RW_CHUNK * RW_CHUNKS_PER_STEP
    nc = t // n
    fwd = pl.BlockSpec((None, n, D), lambda bi, c: (bi, c, 0))
    bwd = pl.BlockSpec((None, n, D), lambda bi, c: (bi, nc - 1 - c, 0))
    fwd2 = pl.BlockSpec((None, None, n, D), lambda bi, c: (0, bi, c, 0))
    bwd2 = pl.BlockSpec((None, None, n, D), lambda bi, c: (1, bi, nc - 1 - c, 0))
    st = pl.BlockSpec((2, None) + RW_STATE, lambda bi, c: (0, bi, 0, 0, 0))
    seq = jax.ShapeDtypeStruct((b, t, D), BF16)
    return pl.pallas_call(
        _rw_scan_body,
        grid=(b, nc),
        in_specs=[fwd, bwd, fwd, bwd, fwd, bwd, fwd2, bwd2, fwd2, bwd2, fwd2, bwd2, st],
        out_specs=[fwd, bwd, st],
        out_shape=[seq, seq, jax.ShapeDtypeStruct(s0.shape, F32)],
        compiler_params=_params("parallel", "arbitrary"),
        name="rwkv_scan",
    )(r, r, v, v, kk, kk, lw, lw, kd, kd, bb, bb, s0)


def _rw_out_body(yf_ref, yb_ref, bv_ref, g_ref, x_ref, mod_ref, vec_ref, wo_ref, o_ref):
    ones = _group_ones()
    y = yf_ref[...].astype(F32) + yb_ref[...].astype(F32)
    yc = y - _head_sum64(y, ones) * (1.0 / RW_HEAD)
    var = _head_sum64(yc * yc, ones) * (1.0 / RW_HEAD)
    yh = yc * lax.rsqrt(var + RW_LNX_EPS) * vec_ref[0:1] + vec_ref[1:2]
    o = _mm((yh + bv_ref[...]) * g_ref[...], wo_ref[...])
    o_ref[...] = x_ref[...] + mod_ref[5:6] * o


def _rw_out_call(yf, yb, bv, g, x, mod, w):
    b, t, _ = x.shape
    tm = min(ROW_TILE, t)
    return pl.pallas_call(
        _rw_out_body,
        grid=(b, t // tm),
        in_specs=[_tile(D, tm)] * 5 + [_MOD_SPEC, _full((SUB, D)), _full((D, D))],
        out_specs=_tile(D, tm),
        out_shape=jax.ShapeDtypeStruct((b, t, D), F32),
        compiler_params=_params("parallel", "parallel"),
        name="rwkv_readout",
    )(yf, yb, bv, g, x, mod, w["ovec"], w["wo"])


def _rwkv_mixer(xc, xl, mc, ml, ng, w, ctx_out):
    b = xl.shape[0]
    pc = _rw_proj_call(xc, mc, ng, w)
    pl_ = _rw_proj_call(xl, ml, ng, w)
    s0 = jnp.zeros((2, b) + RW_STATE, F32)

    def scan(p, s):
        r, v, _, kk, _, lw, kd, bb = p
        return _rw_scan_call(r, v, kk, lw, kd, bb, s)

    def out(yf, yb, p, x, mod):
        return _rw_out_call(yf, yb, p[4], p[2], x, mod, w)

    ycf, ycb, sc = scan(pc, s0)
    ylf, ylb, _ = scan(pl_, sc)
    return (out(ycf, ycb, pc, xc, mc) if ctx_out else None), out(ylf, ylb, pl_, xl, ml)


def _rwkv_weights(mu, w_rkv, w0, w1, w2, a0, a1, a2, g1, g2, k_k, k_a, r_k, lnx_g, lnx_b, w_o):
    zeros = jnp.zeros((RW_HEAD, D), F32)

    def two_dir(m):
        return jnp.concatenate([jnp.concatenate([m[0], zeros], axis=1),
                                jnp.concatenate([zeros, m[1]], axis=1)], axis=0).astype(BF16)

    rows = [mu, w0, a0, k_k[None], k_a[None], r_k.reshape(1, D)]
    vec = jnp.concatenate(rows + [jnp.zeros((RW_VEC_ROWS - 13, D), F32)], axis=0)
    ovec = jnp.concatenate([lnx_g[None], lnx_b[None], jnp.zeros((SUB - 2, D), F32)], axis=0)
    wd = jnp.stack([jnp.concatenate([w1[0], w1[1]], axis=1), jnp.concatenate([a1[0], a1[1]], axis=1), g1])
    return {"vec": vec, "ovec": ovec, "wrkv": w_rkv.astype(BF16), "wd": wd.astype(BF16), "w2": two_dir(w2),
            "a2": two_dir(a2), "g2": g2.astype(BF16), "wo": w_o.astype(BF16)}


def _rt_proj_body(x_ref, mod_ref, ng_ref, win_ref, *rest, rope):
    if rope:
        cos_ref, sin_ref, q_ref, k_ref, v_ref, g_ref = rest
    else:
        q_ref, k_ref, v_ref, g_ref = rest
    h = _modnorm(x_ref[...], ng_ref[1:2], mod_ref[3:4], mod_ref[4:5]).astype(BF16)
    q = jnp.dot(h, win_ref[:, 0:D], preferred_element_type=F32)
    k = jnp.dot(h, win_ref[:, D:2 * D], preferred_element_type=F32) * RT_DK ** -0.5
    if rope:
        cos = jnp.concatenate([cos_ref[...]] * RT_HEADS, axis=-1)
        sin = jnp.concatenate([sin_ref[...]] * RT_HEADS, axis=-1)
        low = jnp.bitwise_and(_iota((1, D), 1), LANE - 1) < LANE // 2

        def rot(t):
            partner = jnp.where(low, pltpu.roll(t, D - LANE // 2, axis=1), pltpu.roll(t, LANE // 2, axis=1))
            return t * cos + partner * sin

        q, k = rot(q), rot(k)
    q_ref[...] = q.astype(BF16)
    k_ref[...] = k.astype(BF16)
    v_ref[...] = jnp.dot(h, win_ref[:, 2 * D:4 * D], preferred_element_type=F32).astype(BF16)
    g_ref[...] = jnp.dot(h, win_ref[:, 4 * D:6 * D], preferred_element_type=F32).astype(BF16)


def _rt_proj_call(x, mod, ng, win, tables):
    b, t, _ = x.shape
    tm = min(ROW_TILE, t)
    in_specs = [_tile(D, tm), _MOD_SPEC, _full((3, D)), _full((D, 6 * D))]
    args = [x, mod, ng, win]
    if tables is not None:
        in_specs += [pl.BlockSpec((tm, RT_DK), lambda bi, i: (i, 0))] * 2
        args += list(tables)
    return pl.pallas_call(
        functools.partial(_rt_proj_body, rope=tables is not None),
        grid=(b, t // tm),
        in_specs=in_specs,
        out_specs=[_tile(D, tm), _tile(D, tm), _tile(2 * D, tm), _tile(2 * D, tm)],
        out_shape=[jax.ShapeDtypeStruct((b, t, D), BF16), jax.ShapeDtypeStruct((b, t, D), BF16),
                   jax.ShapeDtypeStruct((b, t, 2 * D), BF16), jax.ShapeDtypeStruct((b, t, 2 * D), BF16)],
        compiler_params=_params("parallel", "parallel"),
        name="retention_proj",
    )(*args)


def _rt_scan_body(lg_ref, qf_ref, qb_ref, kf_ref, kb_ref, vf_ref, vb_ref, s0_ref, of_ref, ob_ref, s_ref):
    c = pl.program_id(1)
    n = qf_ref.shape[0]
    refs = ((qf_ref, kf_ref, vf_ref, of_ref), (qb_ref, kb_ref, vb_ref, ob_ref))

    @pl.when(c == 0)
    def _():
        s_ref[...] = s0_ref[...]

    tri = _iota((n, n), 0) - _iota((n, n), 1)
    row = _iota((n, 1), 0)
    rel = [tri, -tri]
    pos = [row.astype(F32), (n - 1 - row).astype(F32)]
    heads = [(d, h) for d in range(2) for h in range(RT_HEADS)]
    lg = {k: -jnp.abs(lg_ref[k[0], k[1]]) for k in heads}
    qs = {k: refs[k[0]][0][:, k[1] * RT_DK:(k[1] + 1) * RT_DK] for k in heads}
    ks = {k: refs[k[0]][1][:, k[1] * RT_DK:(k[1] + 1) * RT_DK] for k in heads}
    vs = {k: refs[k[0]][2][:, k[1] * RT_DV:(k[1] + 1) * RT_DV] for k in heads}
    st = {k: s_ref[k[0], k[1]] for k in heads}
    qk = {k: lax.dot_general(qs[k], ks[k], NT, preferred_element_type=F32) for k in heads}
    qr = {k: _mm(qs[k], st[k].astype(BF16)) for k in heads}
    kv = {k: lax.dot_general(ks[k], (jnp.exp((n - 1.0 - pos[k[0]]) * lg[k]) * vs[k].astype(F32)).astype(BF16), TN,
                             preferred_element_type=F32) for k in heads}
    for k in heads:
        s_ref[k[0], k[1]] = jnp.exp(jnp.full((1, 1), float(n), F32) * lg[k]) * st[k] + kv[k]
    for k in heads:
        d, h = k
        dist = jnp.maximum(rel[d], 0).astype(F32)
        s = qk[k] * jnp.where(rel[d] >= 0, jnp.exp(dist * lg[k]), 0.0)
        o = _mm(s, vs[k]) + jnp.exp((pos[d] + 1.0) * lg[k]) * qr[k]
        refs[d][3][:, h * RT_DV:(h + 1) * RT_DV] = o.astype(BF16)


def _rt_scan_call(lg, q, k, v, s0):
    b, t, _ = q.shape
    n = RT_CHUNK
    nc = t // n

    def fwd(width):
        return pl.BlockSpec((None, n, width), lambda bi, c: (bi, c, 0))

    def bwd(width):
        return pl.BlockSpec((None, n, width), lambda bi, c: (bi, nc - 1 - c, 0))

    st = pl.BlockSpec((2, None, RT_HEADS, RT_DK, RT_DV), lambda bi, c: (0, bi, 0, 0, 0))
    seq = jax.ShapeDtypeStruct((b, t, 2 * D), BF16)
    return pl.pallas_call(
        _rt_scan_body,
        grid=(b, nc),
        in_specs=[pl.BlockSpec(memory_space=pltpu.SMEM), fwd(D), bwd(D), fwd(D), bwd(D), fwd(2 * D), bwd(2 * D), st],
        out_specs=[fwd(2 * D), bwd(2 * D), st],
        out_shape=[seq, seq, jax.ShapeDtypeStruct(s0.shape, F32)],
        compiler_params=_params("parallel", "arbitrary"),
        name="retention_scan",
    )(lg, q, q, k, k, v, v, s0)


def _rt_out_body(of_ref, ob_ref, g_ref, x_ref, mod_ref, wo_ref, out_ref):
    o = of_ref[...].astype(F32) + ob_ref[...].astype(F32)
    on = jnp.concatenate([_rms(o[:, h * RT_DV:(h + 1) * RT_DV], NORM_EPS) for h in range(RT_HEADS)], axis=-1)
    out_ref[...] = x_ref[...] + mod_ref[5:6] * _mm(_silu(g_ref[...].astype(F32)) * on, wo_ref[...])


def _rt_out_call(of, ob, g, x, mod, wo):
    b, t, _ = x.shape
    tm = min(ROW_TILE, t)
    return pl.pallas_call(
        _rt_out_body,
        grid=(b, t // tm),
        in_specs=[_tile(2 * D, tm)] * 3 + [_tile(D, tm), _MOD_SPEC, _full((2 * D, D))],
        out_specs=_tile(D, tm),
        out_shape=jax.ShapeDtypeStruct((b, t, D), F32),
        compiler_params=_params("parallel", "parallel"),
        name="retention_readout",
    )(of, ob, g, x, mod, wo)


def _rope_tables(t):
    pos = jnp.arange(t, dtype=jnp.int32)
    quarter = RT_DK // 4
    freqs = ROPE_BASE ** (-jnp.arange(quarter, dtype=F32) / quarter)
    ar = (pos // GRID_W).astype(F32)[:, None] * freqs
    ac = (pos % GRID_W).astype(F32)[:, None] * freqs
    cos = jnp.concatenate([jnp.cos(ar), jnp.cos(ar), jnp.cos(ac), jnp.cos(ac)], axis=-1)
    sin = jnp.concatenate([-jnp.sin(ar), jnp.sin(ar), -jnp.sin(ac), jnp.sin(ac)], axis=-1)
    return cos, sin


def _retention_mixer(xc, xl, mc, ml, ng, w_in, log_gamma, w_o, ctx_out):
    b, t, _ = xl.shape
    win, wo = w_in.astype(BF16), w_o.astype(BF16)
    qc, kc, vc, gc = _rt_proj_call(xc, mc, ng, win, None)
    ql, kl, vl, gl = _rt_proj_call(xl, ml, ng, win, _rope_tables(t))
    s0 = jnp.zeros((2, b, RT_HEADS, RT_DK, RT_DV), F32)
    ocf, ocb, sc = _rt_scan_call(log_gamma, qc, kc, vc, s0)
    olf, olb, _ = _rt_scan_call(log_gamma, ql, kl, vl, sc)
    return ((_rt_out_call(ocf, ocb, gc, xc, mc, wo) if ctx_out else None),
            _rt_out_call(olf, olb, gl, xl, ml, wo))


def _ml_proj_body(x_ref, xp_ref, xn_ref, mod_ref, ng_ref, win_ref, cv_ref, wqk_ref, wv_ref, wif_ref, bif_ref,
                  q_ref, k_ref, v_ref, xc_ref, z_ref, gt_ref, y_scr, xm_scr):
    i = pl.program_id(1)
    tm = x_ref.shape[0]
    g, shift, scale = ng_ref[1:2], mod_ref[3:4], mod_ref[4:5]
    y_scr[0:SUB] = _modnorm(xp_ref[...], g, shift, scale).astype(BF16)
    y_scr[SUB:SUB + tm] = _modnorm(x_ref[...], g, shift, scale).astype(BF16)
    y_scr[SUB + tm:] = _modnorm(xn_ref[...], g, shift, scale).astype(BF16)
    xm = jnp.dot(y_scr[...], win_ref[:, 0:ML_INNER], preferred_element_type=F32)
    row = _iota((tm + 2 * SUB, 1), 0)
    lo = jnp.where(i > 0, 0, SUB)
    hi = jnp.where(i < pl.num_programs(1) - 1, tm + 2 * SUB, tm + SUB)
    xm_scr[...] = jnp.where((row >= lo) & (row < hi), xm, 0.0)
    xm_mid = xm_scr[pl.ds(SUB, tm)]
    conv = (xm_scr[pl.ds(SUB - 1, tm)] * cv_ref[0:1] + xm_mid * cv_ref[1:2]
            + xm_scr[pl.ds(SUB + 1, tm)] * cv_ref[2:3] + cv_ref[3:4])
    xc = _silu(conv)
    xc_ref[...] = xc.astype(BF16)
    z_ref[...] = jnp.dot(y_scr[SUB:SUB + tm], win_ref[:, ML_INNER:], preferred_element_type=F32).astype(BF16)
    qs, ks, vs = [], [], []
    for t in range(ML_TILES):
        sl = slice(t * LANE, (t + 1) * LANE)
        qk = _mm(xc[:, sl], wqk_ref[t])
        qs.append(qk[:, :LANE])
        ks.append(qk[:, LANE:])
        vs.append(_mm(xm_mid[:, sl], wv_ref[t]))
    q = jnp.concatenate(qs, axis=-1)
    k = jnp.concatenate(ks, axis=-1)
    v = jnp.concatenate(vs, axis=-1)
    pre = (_mm(q, wif_ref[0:ML_INNER]) + _mm(k, wif_ref[ML_INNER:2 * ML_INNER])
           + _mm(v, wif_ref[2 * ML_INNER:]) + bif_ref[...])
    is_forget = jnp.bitwise_and(_iota((1, LANE), 1), 2 * ML_HEADS - 1) >= ML_HEADS
    log_sig = jnp.minimum(pre, 0.0) - jnp.log(1.0 + jnp.exp(-jnp.abs(pre)))
    gt_ref[...] = jnp.where(is_forget, log_sig, pre)
    q_ref[...] = q.astype(BF16)
    k_ref[...] = (k * ML_DH ** -0.5).astype(BF16)
    v_ref[...] = v.astype(BF16)


def _ml_proj_call(x, mod, ng, w):
    b, t, _ = x.shape
    tm = min(ROW_TILE, t)
    wide = jax.ShapeDtypeStruct((b, t, ML_INNER), BF16)
    return pl.pallas_call(
        _ml_proj_body,
        grid=(b, t // tm),
        in_specs=_row_specs(t, tm) + [_MOD_SPEC, _full((3, D)), _full((D, 2 * ML_INNER)), _full((SUB, ML_INNER)),
                                      _full((ML_TILES, LANE, 2 * LANE)), _full((ML_TILES, LANE, LANE)),
                                      _full((3 * ML_INNER, LANE)), _full((1, LANE))],
        out_specs=[_tile(ML_INNER, tm)] * 5 + [_tile(LANE, tm)],
        out_shape=[wide] * 5 + [jax.ShapeDtypeStruct((b, t, LANE), F32)],
        scratch_shapes=[pltpu.VMEM((tm + 2 * SUB, D), BF16), pltpu.VMEM((tm + 2 * SUB, ML_INNER), F32)],
        compiler_params=_params("parallel", "parallel"),
        name="mlstm_proj",
    )(x, x, x, mod, ng, w["win"], w["cv"], w["wqk"], w["wv"], w["wif"], w["bif"])


def _ml_scan_body(qf_ref, qb_ref, kf_ref, kb_ref, vf_ref, vb_ref, gf_ref, gb_ref, c0_ref, nm0_ref,
                  hf_ref, hb_ref, c_ref, nm_ref):
    c = pl.program_id(1)
    n = qf_ref.shape[0]
    refs = ((qf_ref, kf_ref, vf_ref, gf_ref, hf_ref), (qb_ref, kb_ref, vb_ref, gb_ref, hb_ref))

    @pl.when(c == 0)
    def _():
        c_ref[...] = c0_ref[...]
        nm_ref[...] = nm0_ref[...]

    tri = _iota((n, n), 0) - _iota((n, n), 1)
    valid = [tri >= 0, tri <= 0]
    dirs = range(2)
    gt, cum, tot, gt_t, cum_t = [], [], [], [], []
    for d in dirs:
        gt.append(refs[d][3][...])
        cum.append(_masked_cumsum(valid[d].astype(BF16), gt[d], 3))
        tot.append(jnp.sum(gt[d], axis=0, keepdims=True))
        gt_t.append(gt[d].T)
        cum_t.append(cum[d].T)
    heads = [(d, h) for d in dirs for h in range(ML_HEADS)]
    sls = [slice(h * ML_DH, (h + 1) * ML_DH) for h in range(ML_HEADS)]
    qs = {k: refs[k[0]][0][:, sls[k[1]]] for k in heads}
    ks = {k: refs[k[0]][1][:, sls[k[1]]] for k in heads}
    vs = {k: refs[k[0]][2][:, sls[k[1]]] for k in heads}
    ct = {k: c_ref[k[0], k[1]] for k in heads}
    m_prev = {k: nm_ref[k[0], ML_HEADS + k[1]:ML_HEADS + k[1] + 1, 0:1] for k in heads}
    n_prev = {k: nm_ref[k[0], k[1]:k[1] + 1, :] for k in heads}
    ig_lane = {k: 2 * ML_HEADS * k[0] + k[1] for k in heads}
    lf_lane = {k: 2 * ML_HEADS * k[0] + ML_HEADS + k[1] for k in heads}
    b_col = {k: cum[k[0]][:, lf_lane[k]:lf_lane[k] + 1] for k in heads}
    decay, m_t, inter = {}, {}, {}
    for k in heads:
        d = k[0]
        b_row = cum_t[d][lf_lane[k]:lf_lane[k] + 1]
        ig_row = gt_t[d][ig_lane[k]:ig_lane[k] + 1]
        logd = jnp.where(valid[d], b_col[k] - b_row + ig_row, -jnp.inf)
        m_inter = b_col[k] + m_prev[k]
        m_t[k] = jnp.maximum(m_inter, jnp.max(logd, axis=-1, keepdims=True))
        decay[k] = jnp.exp(logd - m_t[k])
        inter[k] = jnp.exp(m_inter - m_t[k])
    qk = {k: lax.dot_general(qs[k], ks[k], NT, preferred_element_type=F32) for k in heads}
    qc = {k: _mm(qs[k], ct[k].astype(BF16)) for k in heads}
    for k in heads:
        d, h = k
        b_tot = tot[d][:, lf_lane[k]:lf_lane[k] + 1]
        gj = b_tot - b_col[k] + gt[d][:, ig_lane[k]:ig_lane[k] + 1]
        m_end = b_tot + m_prev[k]
        m_new = jnp.maximum(m_end, jnp.max(gj, axis=0, keepdims=True))
        carry = jnp.exp(m_end - m_new)
        wk = jnp.exp(gj - m_new) * ks[k].astype(F32)
        c_ref[d, h] = carry * ct[k] + lax.dot_general(wk.astype(BF16), vs[k], TN, preferred_element_type=F32)
        nm_ref[d, h:h + 1, :] = carry * n_prev[k] + jnp.sum(wk, axis=0, keepdims=True)
        nm_ref[d, ML_HEADS + h:ML_HEADS + h + 1, :] = jnp.broadcast_to(m_new, (1, ML_DH))
    s = {k: qk[k] * decay[k] for k in heads}
    sv = {k: _mm(s[k], vs[k]) for k in heads}
    for k in heads:
        d, h = k
        num = sv[k] + inter[k] * qc[k]
        den = (jnp.sum(s[k], axis=-1, keepdims=True)
               + inter[k] * jnp.sum(qs[k].astype(F32) * n_prev[k], axis=-1, keepdims=True))
        refs[d][4][:, sls[h]] = (num / jnp.maximum(jnp.abs(den), jnp.exp(-m_t[k]))).astype(BF16)


def _ml_scan_call(q, k, v, gt, c0, nm0):
    b, t, _ = q.shape
    n = ML_CHUNK
    nc = t // n

    def fwd(width):
        return pl.BlockSpec((None, n, width), lambda bi, c: (bi, c, 0))

    def bwd(width):
        return pl.BlockSpec((None, n, width), lambda bi, c: (bi, nc - 1 - c, 0))

    cst = pl.BlockSpec((2, None, ML_HEADS, ML_DH, ML_DH), lambda bi, c: (0, bi, 0, 0, 0),
                       pipeline_mode=pl.Buffered(1))
    nst = pl.BlockSpec((2, None, 2 * ML_HEADS, ML_DH), lambda bi, c: (0, bi, 0, 0))
    seq = jax.ShapeDtypeStruct((b, t, ML_INNER), BF16)
    w = ML_INNER
    return pl.pallas_call(
        _ml_scan_body,
        grid=(b, nc),
        in_specs=[fwd(w), bwd(w), fwd(w), bwd(w), fwd(w), bwd(w), fwd(LANE), bwd(LANE), cst, nst],
        out_specs=[fwd(w), bwd(w), cst, nst],
        out_shape=[seq, seq, jax.ShapeDtypeStruct(c0.shape, F32), jax.ShapeDtypeStruct(nm0.shape, F32)],
        compiler_params=_params("parallel", "arbitrary"),
        name="mlstm_scan",
    )(q, q, k, k, v, v, gt, gt, c0, nm0)


def _ml_out_body(hf_ref, hb_ref, xc_ref, z_ref, x_ref, mod_ref, vec_ref, wo_ref, o_ref):
    y = hf_ref[...].astype(F32) + hb_ref[...].astype(F32)
    parts = []
    for h in range(ML_HEADS):
        yh = y[:, h * ML_DH:(h + 1) * ML_DH]
        parts.append(_rms(yh - jnp.mean(yh, axis=-1, keepdims=True), ML_NORM_EPS))
    yn = jnp.concatenate(parts, axis=-1) * vec_ref[0:1]
    o = _mm((yn + vec_ref[1:2] * xc_ref[...].astype(F32)) * _silu(z_ref[...].astype(F32)), wo_ref[...])
    o_ref[...] = x_ref[...] + mod_ref[5:6] * o


def _ml_out_call(hf, hb, xc, z, x, mod, w):
    b, t, _ = x.shape
    tm = min(ROW_TILE, t)
    return pl.pallas_call(
        _ml_out_body,
        grid=(b, t // tm),
        in_specs=[_tile(ML_INNER, tm)] * 4 + [_tile(D, tm), _MOD_SPEC, _full((SUB, ML_INNER)), _full((ML_INNER, D))],
        out_specs=_tile(D, tm),
        out_shape=jax.ShapeDtypeStruct((b, t, D), F32),
        compiler_params=_params("parallel", "parallel"),
        name="mlstm_readout",
    )(hf, hb, xc, z, x, mod, w["ovec"], w["wo"])


def _mlstm_weights(w_in, conv_w, conv_b, w_q, w_k, w_v, w_if, b_if, skip, norm_g, w_o):
    per_tile = LANE // ML_BLOCK
    eye = jnp.eye(per_tile, dtype=F32)

    def block_diag(w):
        wt = w.reshape(ML_TILES, per_tile, ML_BLOCK, ML_BLOCK)
        return jnp.einsum('tgio,gh->tgiho', wt, eye).reshape(ML_TILES, LANE, LANE)

    wqk = jnp.concatenate([block_diag(w_q), block_diag(w_k)], axis=-1).astype(BF16)
    n_gate = 2 * 2 * ML_HEADS
    wif = jnp.transpose(w_if, (1, 0, 2)).reshape(3 * ML_INNER, n_gate)
    wif = jnp.pad(wif, ((0, 0), (0, LANE - n_gate))).astype(BF16)
    bif = jnp.pad(b_if.reshape(1, n_gate), ((0, 0), (0, LANE - n_gate)))
    cv = jnp.concatenate([conv_w, conv_b[None], jnp.zeros((SUB - 4, ML_INNER), F32)], axis=0)
    ovec = jnp.concatenate([norm_g[None], skip[None], jnp.zeros((SUB - 2, ML_INNER), F32)], axis=0)
    return {"win": w_in.astype(BF16), "cv": cv, "wqk": wqk, "wv": block_diag(w_v).astype(BF16), "wif": wif,
            "bif": bif, "ovec": ovec, "wo": w_o.astype(BF16)}


def _mlstm_mixer(xc, xl, mc, ml, ng, w, ctx_out):
    b = xl.shape[0]
    pc = _ml_proj_call(xc, mc, ng, w)
    pl_ = _ml_proj_call(xl, ml, ng, w)
    c0 = jnp.zeros((2, b, ML_HEADS, ML_DH, ML_DH), F32)
    nm0 = jnp.zeros((2, b, 2 * ML_HEADS, ML_DH), F32)

    def scan(p, cs, nms):
        q, k, v, _, _, gt = p
        return _ml_scan_call(q, k, v, gt, cs, nms)

    def out(hf, hb, p, x, mod):
        return _ml_out_call(hf, hb, p[3], p[4], x, mod, w)

    hcf, hcb, cc, nmc = scan(pc, c0, nm0)
    hlf, hlb, _, _ = scan(pl_, cc, nmc)
    return (out(hcf, hcb, pc, xc, mc) if ctx_out else None), out(hlf, hlb, pl_, xl, ml)


def kernel(x, c, ctx, c_ctx, mod_w, mod_b, norm_g, ffn_w13, ffn_w2, final_g, rw_mu, rw_wrkv, rw_w0, rw_w1, rw_w2, rw_a0, rw_a1, rw_a2, rw_g1, rw_g2, rw_kk, rw_ka, rw_rk, rw_lnx_g, rw_lnx_b, rw_wo, rt_win, rt_log_gamma, rt_wo, ml_win, ml_conv_w, ml_conv_b, ml_wq, ml_wk, ml_wv, ml_wif, ml_bif, ml_skip, ml_norm_g, ml_wo):
    b = x.shape[0]
    depth = mod_w.shape[0]
    assert b < SUB
    cvec = jnp.concatenate([c, c_ctx[None], jnp.zeros((SUB - b - 1, D), F32)], axis=0)
    mods = _mod_call(cvec, mod_w, mod_b)
    w13 = ffn_w13.astype(BF16)
    w2 = ffn_w2.astype(BF16)
    xl, xc = x, ctx
    for i in range(depth):
        kind, s = i % 3, i // 3
        last = i == depth - 1
        ml = mods[i, :b].reshape(b, N_MOD, D)
        mc = jnp.broadcast_to(mods[i, b].reshape(1, N_MOD, D), (b, N_MOD, D))
        ng = norm_g[i]
        xl = _ffn_call(xl, ml, ng, w13, w2, i, 0)
        xc = _ffn_call(xc, mc, ng, w13, w2, i, 0)
        if kind == 0:
            w = _rwkv_weights(rw_mu[s], rw_wrkv[s], rw_w0[s], rw_w1[s], rw_w2[s], rw_a0[s], rw_a1[s], rw_a2[s],
                              rw_g1[s], rw_g2[s], rw_kk[s], rw_ka[s], rw_rk[s], rw_lnx_g[s], rw_lnx_b[s], rw_wo[s])
            xc, xl = _rwkv_mixer(xc, xl, mc, ml, ng, w, not last)
        elif kind == 1:
            xc, xl = _retention_mixer(xc, xl, mc, ml, ng, rt_win[s], rt_log_gamma[s], rt_wo[s], not last)
        else:
            w = _mlstm_weights(ml_win[s], ml_conv_w[s], ml_conv_b[s], ml_wq[s], ml_wk[s], ml_wv[s], ml_wif[s],
                               ml_bif[s], ml_skip[s], ml_norm_g[s], ml_wo[s])
            xc, xl = _mlstm_mixer(xc, xl, mc, ml, ng, w, not last)
        xl = _ffn_call(xl, ml, ng, w13, w2, i, 2, final_g if last else None)
        if not last:
            xc = _ffn_call(xc, mc, ng, w13, w2, i, 2)
    return xl
```

```python
import functools
import math

import jax
import jax.numpy as jnp
from jax import lax
from jax.experimental import pallas as pl
from jax.experimental.pallas import tpu as pltpu

F32 = jnp.float32
BF16 = jnp.bfloat16
HI = lax.Precision.HIGHEST

LANE = 128
SUB = 8
V7X_VMEM_BYTES = 64 * 1024 * 1024
VMEM_LIMIT = V7X_VMEM_BYTES * 7 // 8

D = 1024
D_FF = 2816
N_MOD = 9
NORM_EPS = 1e-6
GRID_W = 64
RW_HEAD = 64
RW_LNX_EPS = 64e-5
RW_CHUNK = 64
RW_CHUNKS_PER_STEP = 4
RW_GROUP = 4
RW_GROUPS = D // (RW_GROUP * RW_HEAD)
RW_STATE = (RW_GROUPS, RW_GROUP * RW_HEAD, RW_GROUP * RW_HEAD)
RW_VEC_ROWS = 2 * SUB
RT_HEADS = 4
RT_DK = D // RT_HEADS
RT_DV = 2 * D // RT_HEADS
ROPE_BASE = 10000.0
RT_CHUNK = 256
ML_INNER = 2 * D
ML_HEADS = 4
ML_DH = ML_INNER // ML_HEADS
ML_BLOCK = 4
ML_NORM_EPS = 1e-5
ML_CHUNK = 256
ML_TILES = ML_INNER // LANE

ROW_TILE = 256
FFN_ROW_TILE = 512
FFN_ROW_PARTS = 4

NT = (((1,), (1,)), ((), ()))
TN = (((0,), (0,)), ((), ()))


def _params(*sem):
    return pltpu.CompilerParams(dimension_semantics=sem, vmem_limit_bytes=VMEM_LIMIT)


def _sigmoid(x):
    return 1.0 / (1.0 + jnp.exp(-x))


def _silu(x):
    return x * _sigmoid(x)


def _rms(x, eps):
    return x * lax.rsqrt(jnp.mean(x * x, axis=-1, keepdims=True) + eps)


def _modnorm(x, g, shift, scale):
    return _rms(x, NORM_EPS) * g * (1.0 + scale) + shift


def _mm(a, b):
    return jnp.dot(a.astype(BF16), b, preferred_element_type=F32)


def _iota(shape, dim):
    return lax.broadcasted_iota(jnp.int32, shape, dim)


def _masked_cumsum(mask01, a, parts):
    out = 0.0
    for _ in range(parts):
        piece = a.astype(BF16)
        out = out + jnp.dot(mask01, piece, preferred_element_type=F32)
        a = a - piece.astype(F32)
    return out


def _mod_body(c_ref, w_ref, b_ref, o_ref):
    s = _silu(c_ref[...])
    o_ref[...] = jnp.dot(s, w_ref[...], precision=HI, preferred_element_type=F32) + b_ref[...]


def _mod_call(cvec, mod_w, mod_b):
    depth, _, n = mod_w.shape
    tn = D
    return pl.pallas_call(
        _mod_body,
        grid=(depth, n // tn),
        in_specs=[pl.BlockSpec((SUB, D), lambda l, j: (0, 0)),
                  pl.BlockSpec((None, D, tn), lambda l, j: (l, 0, j)),
                  pl.BlockSpec((None, 1, tn), lambda l, j: (l, 0, j))],
        out_specs=pl.BlockSpec((None, SUB, tn), lambda l, j: (l, 0, j)),
        out_shape=jax.ShapeDtypeStruct((depth, SUB, n), F32),
        compiler_params=_params("parallel", "parallel"),
        name="adaln_mod",
    )(cvec, mod_w, mod_b.reshape(depth, 1, n))


def _ffn_body(x_ref, mod_ref, ng_ref, w13_ref, w2_ref, *rest, sub, final):
    if final:
        fg_ref, o_ref = rest
    else:
        (o_ref,) = rest
    tm = x_ref.shape[0]
    for p in range(FFN_ROW_PARTS):
        rows = pl.ds(p * tm // FFN_ROW_PARTS, tm // FFN_ROW_PARTS)
        x = x_ref[rows, :]
        y = _modnorm(x, ng_ref[sub:sub + 1], mod_ref[3 * sub:3 * sub + 1], mod_ref[3 * sub + 1:3 * sub + 2])
        y = y.astype(BF16)
        a = jnp.dot(y, w13_ref[:, :D_FF], preferred_element_type=F32)
        b = jnp.dot(y, w13_ref[:, D_FF:], preferred_element_type=F32)
        out = x + 0.5 * mod_ref[3 * sub + 2:3 * sub + 3] * _mm(_silu(a) * b, w2_ref[...])
        if final:
            out = _rms(out, NORM_EPS) * fg_ref[...]
        o_ref[rows, :] = out


def _ffn_call(x, mod, ng, w13, w2, layer, sub, final_g=None):
    b, t, _ = x.shape
    tm = min(FFN_ROW_TILE, t)
    half = sub // 2
    in_specs = [pl.BlockSpec((None, tm, D), lambda bi, i: (bi, i, 0)),
                pl.BlockSpec((None, N_MOD, D), lambda bi, i: (bi, 0, 0)),
                pl.BlockSpec((3, D), lambda bi, i: (0, 0)),
                pl.BlockSpec((None, None, D, 2 * D_FF), lambda bi, i: (layer, half, 0, 0),
                             pipeline_mode=pl.Buffered(1)),
                pl.BlockSpec((None, None, D_FF, D), lambda bi, i: (layer, half, 0, 0),
                             pipeline_mode=pl.Buffered(1))]
    args = [x, mod, ng, w13, w2]
    if final_g is not None:
        in_specs.append(pl.BlockSpec((1, D), lambda bi, i: (0, 0)))
        args.append(final_g.reshape(1, D))
    return pl.pallas_call(
        functools.partial(_ffn_body, sub=sub, final=final_g is not None),
        grid=(b, t // tm),
        in_specs=in_specs,
        out_specs=pl.BlockSpec((None, tm, D), lambda bi, i: (bi, i, 0)),
        out_shape=jax.ShapeDtypeStruct((b, t, D), F32),
        compiler_params=_params("parallel", "parallel"),
        name="half_ffn",
    )(*args)


def _row_specs(t, tm):
    nb = tm // SUB
    last = t // SUB - 1
    return [pl.BlockSpec((None, tm, D), lambda bi, i: (bi, i, 0)),
            pl.BlockSpec((None, SUB, D), lambda bi, i: (bi, jnp.maximum(i * nb - 1, 0), 0)),
            pl.BlockSpec((None, SUB, D), lambda bi, i: (bi, jnp.minimum((i + 1) * nb, last), 0))]


def _full(shape):
    nd = len(shape)
    return pl.BlockSpec(shape, lambda bi, i: (0,) * nd)


def _tile(width, tm):
    return pl.BlockSpec((None, tm, width), lambda bi, i: (bi, i, 0))


def _tile2(width, tm):
    return pl.BlockSpec((2, None, tm, width), lambda bi, i: (0, bi, i, 0))


_MOD_SPEC = pl.BlockSpec((None, N_MOD, D), lambda bi, i: (bi, 0, 0))


def _group_ones():
    r = jnp.right_shift(_iota((LANE, LANE), 0), 6)
    c = jnp.right_shift(_iota((LANE, LANE), 1), 6)
    return (r == c).astype(BF16)


def _head_sum64(x, ones):
    parts = [_mm(x[:, t * LANE:(t + 1) * LANE], ones) for t in range(x.shape[1] // LANE)]
    return jnp.concatenate(parts, axis=-1)


def _rw_proj_body(x_ref, xp_ref, xn_ref, mod_ref, ng_ref, vec_ref, wrkv_ref, wd_ref, w2_ref, a2_ref, g2_ref,
                  r_ref, v_ref, g_ref, kk_ref, bv_ref, lw_ref, kd_ref, b_ref):
    i = pl.program_id(1)
    tm = x_ref.shape[0]
    g, shift, scale = ng_ref[1:2], mod_ref[3:4], mod_ref[4:5]
    h = _modnorm(x_ref[...], g, shift, scale)
    hp = jnp.where(i > 0, _modnorm(xp_ref[SUB - 1:SUB], g, shift, scale), 0.0)
    hn = jnp.where(i < pl.num_programs(1) - 1, _modnorm(xn_ref[0:1], g, shift, scale), 0.0)
    row = _iota((tm, 1), 0)
    h_prev = jnp.where(row == 0, hp, pltpu.roll(h, 1, axis=0))
    h_next = jnp.where(row == tm - 1, hn, pltpu.roll(h, tm - 1, axis=0))
    xx = 0.5 * (h_prev + h_next) - h

    def mix(n):
        return (h + xx * vec_ref[n:n + 1]).astype(BF16)

    wl = _mm(jnp.tanh(jnp.dot(mix(1), wd_ref[0], preferred_element_type=F32)), w2_ref[...])
    al = _mm(jnp.dot(mix(4), wd_ref[1], preferred_element_type=F32), a2_ref[...])
    k = jnp.dot(mix(2), wrkv_ref[1], preferred_element_type=F32)
    gg = _mm(_sigmoid(jnp.dot(mix(5), wd_ref[2], preferred_element_type=F32)), g2_ref[...])
    r = jnp.dot(mix(0), wrkv_ref[0], preferred_element_type=F32)
    v = jnp.dot(mix(3), wrkv_ref[2], preferred_element_type=F32)

    ones = _group_ones()
    kkr = k * vec_ref[10:11]
    n2 = _head_sum64(kkr * kkr, ones)
    kk = kkr / jnp.maximum(jnp.sqrt(n2), 1e-12)
    kd_sum = 0.0
    for d in range(2):
        lw_ref[d] = -math.exp(-0.5) * _sigmoid(vec_ref[6 + d:7 + d] + wl[:, d * D:(d + 1) * D])
        a = _sigmoid(vec_ref[8 + d:9 + d] + al[:, d * D:(d + 1) * D])
        kd = k * (1.0 + (a - 1.0) * vec_ref[11:12])
        kd_ref[d] = kd
        b_ref[d] = kk * a
        kd_sum = kd_sum + kd
    bv_ref[...] = (_head_sum64(r * kd_sum * vec_ref[12:13], ones) * v).astype(BF16)
    r_ref[...] = r
    v_ref[...] = v.astype(BF16)
    g_ref[...] = gg.astype(BF16)
    kk_ref[...] = kk


def _rw_proj_call(x, mod, ng, w):
    b, t, _ = x.shape
    tm = min(ROW_TILE, t)
    one = jax.ShapeDtypeStruct((b, t, D), F32)
    half = jax.ShapeDtypeStruct((b, t, D), BF16)
    two = jax.ShapeDtypeStruct((2, b, t, D), F32)
    return pl.pallas_call(
        _rw_proj_body,
        grid=(b, t // tm),
        in_specs=_row_specs(t, tm) + [_MOD_SPEC, _full((3, D)), _full((RW_VEC_ROWS, D)), _full((3, D, D)),
                                      _full((3, D, LANE)), _full((LANE, 2 * D)), _full((LANE, 2 * D)),
                                      _full((LANE, D))],
        out_specs=[_tile(D, tm)] * 5 + [_tile2(D, tm)] * 3,
        out_shape=[one, half, half, one, half] + [two] * 3,
        compiler_params=_params("parallel", "parallel"),
        name="rwkv_proj",
    )(x, x, x, mod, ng, w["vec"], w["wrkv"], w["wd"], w["w2"], w["a2"], w["g2"])


def _rw_scan_body(rf_ref, rb_ref, vf_ref, vb_ref, kkf_ref, kkb_ref, lwf_ref, lwb_ref, kdf_ref, kdb_ref,
                  bf_ref, bb_ref, s0_ref, yf_ref, yb_ref, s_ref):
    c = pl.program_id(1)
    n = RW_CHUNK
    subs = rf_ref.shape[0] // n
    refs = ((rf_ref, vf_ref, kkf_ref, lwf_ref, kdf_ref, bf_ref, yf_ref),
            (rb_ref, vb_ref, kkb_ref, lwb_ref, kdb_ref, bb_ref, yb_ref))

    @pl.when(c == 0)
    def _():
        s_ref[...] = s0_ref[...]

    gl = RW_GROUP * RW_HEAD
    shift = int(math.log2(RW_HEAD))
    tri = _iota((n, n), 0) - _iota((n, n), 1)
    rel = _iota((n, gl), 0) - jnp.bitwise_and(_iota((n, gl), 1), RW_HEAD - 1)
    before = [(tri >= 0).astype(BF16), (tri <= 0).astype(BF16)]
    strict = [rel > 0, rel < 0]
    incl = [rel >= 0, rel <= 0]
    eye = (rel == 0).astype(F32)
    row_head = jnp.right_shift(_iota((gl, 2 * gl), 0), shift)
    col_head = jnp.right_shift(jnp.bitwise_and(_iota((gl, 2 * gl), 1), gl - 1), shift)
    same2 = (row_head == col_head).astype(BF16)
    same = same2[:, :gl]
    same32 = same.astype(F32)
    low = _iota((1, LANE), 1) < RW_HEAD

    def bdiag(m):
        return jnp.concatenate([m] * RW_GROUP, axis=0) * same

    groups = range(D // gl)
    sls = [slice(q * gl, (q + 1) * gl) for q in groups]
    dirs = range(2)
    rows, ab, rb, vb, rhs, hk_t, keep = {}, {}, {}, {}, {}, {}, {}
    a_rb, akv, inv = {}, {}, {}

    def prepare(j):
        for d in dirs:
            r_ref, v_ref, kk_ref, lw_ref, kd_ref, b_ref, _ = refs[d]
            rows[d, j] = pl.ds((subs - 1 - j if d else j) * n, n)
            rw = rows[d, j]
            lw = lw_ref[rw, :]
            cum = _masked_cumsum(before[d], lw, 3)
            tot = jnp.sum(lw, axis=0, keepdims=True)
            kd, bb = kd_ref[rw, :], b_ref[rw, :]
            rt = r_ref[rw, :] * jnp.exp(cum)
            at = -kk_ref[rw, :] * jnp.exp(cum - lw)
            e_out = jnp.exp(-cum)
            kt, bt = kd * e_out, bb * e_out
            e_tail = jnp.exp(tot - cum)
            kh, bh = kd * e_tail, bb * e_tail
            v = v_ref[rw, :]
            etot = jnp.exp(tot)
            yield
            for q in groups:
                sl = sls[q]
                t = jnp.concatenate([bt[:, sl], kt[:, sl]], axis=0).T
                sw = pltpu.roll(t, RW_HEAD, axis=1)
                b2 = jnp.where(low, t, sw).astype(BF16)
                k2 = jnp.where(low, sw, t).astype(BF16)
                rhs[d, j, q] = jnp.concatenate([b2, b2, k2, k2], axis=1) * same2
                ab[d, j, q] = at[:, sl].astype(BF16)
                rb[d, j, q] = rt[:, sl].astype(BF16)
                vb[d, j, q] = v[:, sl].astype(BF16)
                hk_t[d, j, q] = jnp.concatenate([bh[:, sl], kh[:, sl]], axis=0).T.astype(BF16)
                keep[d, j, q] = jnp.concatenate([jnp.broadcast_to(etot[:, sl], (LANE, gl)).T] * 2, axis=1)
                if q % 2:
                    yield

    def invert(j):
        chains = [(d, j, q) for d in dirs for q in groups]
        prod = {k: jnp.dot(jnp.concatenate([ab[k], rb[k]], axis=0), rhs[k], preferred_element_type=F32)
                for k in chains}
        yield
        a_ab = {k: jnp.where(strict[k[0]], prod[k][:n, :gl], 0.0) for k in chains}
        for k in chains:
            a_rb[k] = jnp.where(incl[k[0]], prod[k][n:, :gl], 0.0).astype(BF16)
            a_k = jnp.concatenate([jnp.where(strict[k[0]], prod[k][:n, gl:], 0.0),
                                   jnp.where(incl[k[0]], prod[k][n:, gl:], 0.0)], axis=0)
            akv[k] = _mm(a_k, bdiag(vb[k]))
        acc = {k: eye + a_ab[k] for k in chains}
        pw = {k: a_ab[k].astype(BF16) for k in chains}
        pw = {k: jnp.dot(pw[k], bdiag(pw[k]), preferred_element_type=F32).astype(BF16) for k in chains}
        yield
        for _ in range(int(math.log2(n)) - 2):
            both = {k: jnp.dot(jnp.concatenate([pw[k], acc[k].astype(BF16)], axis=0), bdiag(pw[k]),
                               preferred_element_type=F32) for k in chains}
            yield
            acc = {k: acc[k] + both[k][n:] for k in chains}
            pw = {k: both[k][:n].astype(BF16) for k in chains}
        for k in chains:
            inv[k] = (acc[k] + _mm(acc[k], bdiag(pw[k]))).astype(BF16)
        yield

    heads = [(d, q) for d in dirs for q in groups]
    st = {k: s_ref[k[0], k[1]] for k in heads}

    def advance(j):
        at_j = lambda tab, k: tab[k[0], j, k[1]]
        ss = {k: _mm(jnp.concatenate([at_j(ab, k), at_j(rb, k)], axis=0), st[k].astype(BF16)) + at_j(akv, k)
              for k in heads}
        yield
        u = {k: jnp.dot(at_j(inv, k), bdiag(ss[k][:n].astype(BF16)), preferred_element_type=F32).astype(BF16)
             for k in heads}
        yield
        upd = {k: jnp.dot(at_j(hk_t, k), jnp.concatenate([u[k], at_j(vb, k)], axis=0), preferred_element_type=F32)
               for k in heads}
        yield
        y = {k: ss[k][n:] + jnp.dot(at_j(a_rb, k), bdiag(u[k]), preferred_element_type=F32) for k in heads}
        yield
        for k in heads:
            st[k] = st[k] * at_j(keep, k) + upd[k] * same32
        for d in dirs:
            refs[d][-1][rows[d, j], :] = jnp.concatenate([y[d, q] for q in groups], axis=-1).astype(BF16)
        yield

    chunks = range(subs)
    for phase in [[prepare(j)] for j in chunks] + [[invert(j) for j in chunks]] + [[advance(j)] for j in chunks]:
        while phase:
            phase = [g for g in phase if next(g, phase) is not phase]
    for k in heads:
        s_ref[k[0], k[1]] = st[k]


def _rw_scan_call(r, v, kk, lw, kd, bb, s0):
    b, t, _ = r.shape
    n = RW_CHUNK * RW_CHUNKS_PER_STEP
    nc = t // n
    fwd = pl.BlockSpec((None, n, D), lambda bi, c: (bi, c, 0))
    bwd = pl.BlockSpec((None, n, D), lambda bi, c: (bi, nc - 1 - c, 0))
    fwd2 = pl.BlockSpec((None, None, n, D), lambda bi, c: (0, bi, c, 0))
    bwd2 = pl.BlockSpec((None, None, n, D), lambda bi, c: (1, bi, nc - 1 - c, 0))
    st = pl.BlockSpec((2, None) + RW_STATE, lambda bi, c: (0, bi, 0, 0, 0))
    seq = jax.ShapeDtypeStruct((b, t, D), BF16)
    return pl.pallas_call(
        _rw_scan_body,
        grid=(b, nc),
        in_specs=[fwd, bwd, fwd, bwd, fwd, bwd, fwd2, bwd2, fwd2, bwd2, fwd2, bwd2, st],
        out_specs=[fwd, bwd, st],
        out_shape=[seq, seq, jax.ShapeDtypeStruct(s0.shape, F32)],
        compiler_params=_params("parallel", "arbitrary"),
        name="rwkv_scan",
    )(r, r, v, v, kk, kk, lw, lw, kd, kd, bb, bb, s0)


def _rw_out_body(yf_ref, yb_ref, bv_ref, g_ref, x_ref, mod_ref, vec_ref, wo_ref, o_ref):
    ones = _group_ones()
    y = yf_ref[...].astype(F32) + yb_ref[...].astype(F32)
    yc = y - _head_sum64(y, ones) * (1.0 / RW_HEAD)
    var = _head_sum64(yc * yc, ones) * (1.0 / RW_HEAD)
    yh = yc * lax.rsqrt(var + RW_LNX_EPS) * vec_ref[0:1] + vec_ref[1:2]
    o = _mm((yh + bv_ref[...]) * g_ref[...], wo_ref[...])
    o_ref[...] = x_ref[...] + mod_ref[5:6] * o


def _rw_out_call(yf, yb, bv, g, x, mod, w):
    b, t, _ = x.shape
    tm = min(ROW_TILE, t)
    return pl.pallas_call(
        _rw_out_body,
        grid=(b, t // tm),
        in_specs=[_tile(D, tm)] * 5 + [_MOD_SPEC, _full((SUB, D)), _full((D, D))],
        out_specs=_tile(D, tm),
        out_shape=jax.ShapeDtypeStruct((b, t, D), F32),
        compiler_params=_params("parallel", "parallel"),
        name="rwkv_readout",
    )(yf, yb, bv, g, x, mod, w["ovec"], w["wo"])


def _rwkv_mixer(xc, xl, mc, ml, ng, w, ctx_out):
    b = xl.shape[0]
    pc = _rw_proj_call(xc, mc, ng, w)
    pl_ = _rw_proj_call(xl, ml, ng, w)
    s0 = jnp.zeros((2, b) + RW_STATE, F32)

    def scan(p, s):
        r, v, _, kk, _, lw, kd, bb = p
        return _rw_scan_call(r, v, kk, lw, kd, bb, s)

    def out(yf, yb, p, x, mod):
        return _rw_out_call(yf, yb, p[4], p[2], x, mod, w)

    ycf, ycb, sc = scan(pc, s0)
    ylf, ylb, _ = scan(pl_, sc)
    return (out(ycf, ycb, pc, xc, mc) if ctx_out else None), out(ylf, ylb, pl_, xl, ml)


def _rwkv_weights(mu, w_rkv, w0, w1, w2, a0, a1, a2, g1, g2, k_k, k_a, r_k, lnx_g, lnx_b, w_o):
    zeros = jnp.zeros((RW_HEAD, D), F32)

    def two_dir(m):
        return jnp.concatenate([jnp.concatenate([m[0], zeros], axis=1),
                                jnp.concatenate([zeros, m[1]], axis=1)], axis=0).astype(BF16)

    rows = [mu, w0, a0, k_k[None], k_a[None], r_k.reshape(1, D)]
    vec = jnp.concatenate(rows + [jnp.zeros((RW_VEC_ROWS - 13, D), F32)], axis=0)
    ovec = jnp.concatenate([lnx_g[None], lnx_b[None], jnp.zeros((SUB - 2, D), F32)], axis=0)
    wd = jnp.stack([jnp.concatenate([w1[0], w1[1]], axis=1), jnp.concatenate([a1[0], a1[1]], axis=1), g1])
    return {"vec": vec, "ovec": ovec, "wrkv": w_rkv.astype(BF16), "wd": wd.astype(BF16), "w2": two_dir(w2),
            "a2": two_dir(a2), "g2": g2.astype(BF16), "wo": w_o.astype(BF16)}


def _rt_proj_body(x_ref, mod_ref, ng_ref, win_ref, *rest, rope):
    if rope:
        cos_ref, sin_ref, q_ref, k_ref, v_ref, g_ref = rest
    else:
        q_ref, k_ref, v_ref, g_ref = rest
    h = _modnorm(x_ref[...], ng_ref[1:2], mod_ref[3:4], mod_ref[4:5]).astype(BF16)
    q = jnp.dot(h, win_ref[:, 0:D], preferred_element_type=F32)
    k = jnp.dot(h, win_ref[:, D:2 * D], preferred_element_type=F32) * RT_DK ** -0.5
    if rope:
        cos = jnp.concatenate([cos_ref[...]] * RT_HEADS, axis=-1)
        sin = jnp.concatenate([sin_ref[...]] * RT_HEADS, axis=-1)
        low = jnp.bitwise_and(_iota((1, D), 1), LANE - 1) < LANE // 2

        def rot(t):
            partner = jnp.where(low, pltpu.roll(t, D - LANE // 2, axis=1), pltpu.roll(t, LANE // 2, axis=1))
            return t * cos + partner * sin

        q, k = rot(q), rot(k)
    q_ref[...] = q.astype(BF16)
    k_ref[...] = k.astype(BF16)
    v_ref[...] = jnp.dot(h, win_ref[:, 2 * D:4 * D], preferred_element_type=F32).astype(BF16)
    g_ref[...] = jnp.dot(h, win_ref[:, 4 * D:6 * D], preferred_element_type=F32).astype(BF16)


def _rt_proj_call(x, mod, ng, win, tables):
    b, t, _ = x.shape
    tm = min(ROW_TILE, t)
    in_specs = [_tile(D, tm), _MOD_SPEC, _full((3, D)), _full((D, 6 * D))]
    args = [x, mod, ng, win]
    if tables is not None:
        in_specs += [pl.BlockSpec((tm, RT_DK), lambda bi, i: (i, 0))] * 2
        args += list(tables)
    return pl.pallas_call(
        functools.partial(_rt_proj_body, rope=tables is not None),
        grid=(b, t // tm),
        in_specs=in_specs,
        out_specs=[_tile(D, tm), _tile(D, tm), _tile(2 * D, tm), _tile(2 * D, tm)],
        out_shape=[jax.ShapeDtypeStruct((b, t, D), BF16), jax.ShapeDtypeStruct((b, t, D), BF16),
                   jax.ShapeDtypeStruct((b, t, 2 * D), BF16), jax.ShapeDtypeStruct((b, t, 2 * D), BF16)],
        compiler_params=_params("parallel", "parallel"),
        name="retention_proj",
    )(*args)


def _rt_scan_body(lg_ref, qf_ref, qb_ref, kf_ref, kb_ref, vf_ref, vb_ref, s0_ref, of_ref, ob_ref, s_ref):
    c = pl.program_id(1)
    n = qf_ref.shape[0]
    refs = ((qf_ref, kf_ref, vf_ref, of_ref), (qb_ref, kb_ref, vb_ref, ob_ref))

    @pl.when(c == 0)
    def _():
        s_ref[...] = s0_ref[...]

    tri = _iota((n, n), 0) - _iota((n, n), 1)
    row = _iota((n, 1), 0)
    rel = [tri, -tri]
    pos = [row.astype(F32), (n - 1 - row).astype(F32)]
    heads = [(d, h) for d in range(2) for h in range(RT_HEADS)]
    lg = {k: -jnp.abs(lg_ref[k[0], k[1]]) for k in heads}
    qs = {k: refs[k[0]][0][:, k[1] * RT_DK:(k[1] + 1) * RT_DK] for k in heads}
    ks = {k: refs[k[0]][1][:, k[1] * RT_DK:(k[1] + 1) * RT_DK] for k in heads}
    vs = {k: refs[k[0]][2][:, k[1] * RT_DV:(k[1] + 1) * RT_DV] for k in heads}
    st = {k: s_ref[k[0], k[1]] for k in heads}
    qk = {k: lax.dot_general(qs[k], ks[k], NT, preferred_element_type=F32) for k in heads}
    qr = {k: _mm(qs[k], st[k].astype(BF16)) for k in heads}
    kv = {k: lax.dot_general(ks[k], (jnp.exp((n - 1.0 - pos[k[0]]) * lg[k]) * vs[k].astype(F32)).astype(BF16), TN,
                             preferred_element_type=F32) for k in heads}
    for k in heads:
        s_ref[k[0], k[1]] = jnp.exp(jnp.full((1, 1), float(n), F32) * lg[k]) * st[k] + kv[k]
    for k in heads:
        d, h = k
        dist = jnp.maximum(rel[d], 0).astype(F32)
        s = qk[k] * jnp.where(rel[d] >= 0, jnp.exp(dist * lg[k]), 0.0)
        o = _mm(s, vs[k]) + jnp.exp((pos[d] + 1.0) * lg[k]) * qr[k]
        refs[d][3][:, h * RT_DV:(h + 1) * RT_DV] = o.astype(BF16)


def _rt_scan_call(lg, q, k, v, s0):
    b, t, _ = q.shape
    n = RT_CHUNK
    nc = t // n

    def fwd(width):
        return pl.BlockSpec((None, n, width), lambda bi, c: (bi, c, 0))

    def bwd(width):
        return pl.BlockSpec((None, n, width), lambda bi, c: (bi, nc - 1 - c, 0))

    st = pl.BlockSpec((2, None, RT_HEADS, RT_DK, RT_DV), lambda bi, c: (0, bi, 0, 0, 0))
    seq = jax.ShapeDtypeStruct((b, t, 2 * D), BF16)
    return pl.pallas_call(
        _rt_scan_body,
        grid=(b, nc),
        in_specs=[pl.BlockSpec(memory_space=pltpu.SMEM), fwd(D), bwd(D), fwd(D), bwd(D), fwd(2 * D), bwd(2 * D), st],
        out_specs=[fwd(2 * D), bwd(2 * D), st],
        out_shape=[seq, seq, jax.ShapeDtypeStruct(s0.shape, F32)],
        compiler_params=_params("parallel", "arbitrary"),
        name="retention_scan",
    )(lg, q, q, k, k, v, v, s0)


def _rt_out_body(of_ref, ob_ref, g_ref, x_ref, mod_ref, wo_ref, out_ref):
    o = of_ref[...].astype(F32) + ob_ref[...].astype(F32)
    on = jnp.concatenate([_rms(o[:, h * RT_DV:(h + 1) * RT_DV], NORM_EPS) for h in range(RT_HEADS)], axis=-1)
    out_ref[...] = x_ref[...] + mod_ref[5:6] * _mm(_silu(g_ref[...].astype(F32)) * on, wo_ref[...])


def _rt_out_call(of, ob, g, x, mod, wo):
    b, t, _ = x.shape
    tm = min(ROW_TILE, t)
    return pl.pallas_call(
        _rt_out_body,
        grid=(b, t // tm),
        in_specs=[_tile(2 * D, tm)] * 3 + [_tile(D, tm), _MOD_SPEC, _full((2 * D, D))],
        out_specs=_tile(D, tm),
        out_shape=jax.ShapeDtypeStruct((b, t, D), F32),
        compiler_params=_params("parallel", "parallel"),
        name="retention_readout",
    )(of, ob, g, x, mod, wo)


def _rope_tables(t):
    pos = jnp.arange(t, dtype=jnp.int32)
    quarter = RT_DK // 4
    freqs = ROPE_BASE ** (-jnp.arange(quarter, dtype=F32) / quarter)
    ar = (pos // GRID_W).astype(F32)[:, None] * freqs
    ac = (pos % GRID_W).astype(F32)[:, None] * freqs
    cos = jnp.concatenate([jnp.cos(ar), jnp.cos(ar), jnp.cos(ac), jnp.cos(ac)], axis=-1)
    sin = jnp.concatenate([-jnp.sin(ar), jnp.sin(ar), -jnp.sin(ac), jnp.sin(ac)], axis=-1)
    return cos, sin


def _retention_mixer(xc, xl, mc, ml, ng, w_in, log_gamma, w_o, ctx_out):
    b, t, _ = xl.shape
    win, wo = w_in.astype(BF16), w_o.astype(BF16)
    qc, kc, vc, gc = _rt_proj_call(xc, mc, ng, win, None)
    ql, kl, vl, gl = _rt_proj_call(xl, ml, ng, win, _rope_tables(t))
    s0 = jnp.zeros((2, b, RT_HEADS, RT_DK, RT_DV), F32)
    ocf, ocb, sc = _rt_scan_call(log_gamma, qc, kc, vc, s0)
    olf, olb, _ = _rt_scan_call(log_gamma, ql, kl, vl, sc)
    return ((_rt_out_call(ocf, ocb, gc, xc, mc, wo) if ctx_out else None),
            _rt_out_call(olf, olb, gl, xl, ml, wo))


def _ml_proj_body(x_ref, xp_ref, xn_ref, mod_ref, ng_ref, win_ref, cv_ref, wqk_ref, wv_ref, wif_ref, bif_ref,
                  q_ref, k_ref, v_ref, xc_ref, z_ref, gt_ref, y_scr, xm_scr):
    i = pl.program_id(1)
    tm = x_ref.shape[0]
    g, shift, scale = ng_ref[1:2], mod_ref[3:4], mod_ref[4:5]
    y_scr[0:SUB] = _modnorm(xp_ref[...], g, shift, scale).astype(BF16)
    y_scr[SUB:SUB + tm] = _modnorm(x_ref[...], g, shift, scale).astype(BF16)
    y_scr[SUB + tm:] = _modnorm(xn_ref[...], g, shift, scale).astype(BF16)
    xm = jnp.dot(y_scr[...], win_ref[:, 0:ML_INNER], preferred_element_type=F32)
    row = _iota((tm + 2 * SUB, 1), 0)
    lo = jnp.where(i > 0, 0, SUB)
    hi = jnp.where(i < pl.num_programs(1) - 1, tm + 2 * SUB, tm + SUB)
    xm_scr[...] = jnp.where((row >= lo) & (row < hi), xm, 0.0)
    xm_mid = xm_scr[pl.ds(SUB, tm)]
    conv = (xm_scr[pl.ds(SUB - 1, tm)] * cv_ref[0:1] + xm_mid * cv_ref[1:2]
            + xm_scr[pl.ds(SUB + 1, tm)] * cv_ref[2:3] + cv_ref[3:4])
    xc = _silu(conv)
    xc_ref[...] = xc.astype(BF16)
    z_ref[...] = jnp.dot(y_scr[SUB:SUB + tm], win_ref[:, ML_INNER:], preferred_element_type=F32).astype(BF16)
    qs, ks, vs = [], [], []
    for t in range(ML_TILES):
        sl = slice(t * LANE, (t + 1) * LANE)
        qk = _mm(xc[:, sl], wqk_ref[t])
        qs.append(qk[:, :LANE])
        ks.append(qk[:, LANE:])
        vs.append(_mm(xm_mid[:, sl], wv_ref[t]))
    q = jnp.concatenate(qs, axis=-1)
    k = jnp.concatenate(ks, axis=-1)
    v = jnp.concatenate(vs, axis=-1)
    pre = (_mm(q, wif_ref[0:ML_INNER]) + _mm(k, wif_ref[ML_INNER:2 * ML_INNER])
           + _mm(v, wif_ref[2 * ML_INNER:]) + bif_ref[...])
    is_forget = jnp.bitwise_and(_iota((1, LANE), 1), 2 * ML_HEADS - 1) >= ML_HEADS
    log_sig = jnp.minimum(pre, 0.0) - jnp.log(1.0 + jnp.exp(-jnp.abs(pre)))
    gt_ref[...] = jnp.where(is_forget, log_sig, pre)
    q_ref[...] = q.astype(BF16)
    k_ref[...] = (k * ML_DH ** -0.5).astype(BF16)
    v_ref[...] = v.astype(BF16)


def _ml_proj_call(x, mod, ng, w):
    b, t, _ = x.shape
    tm = min(ROW_TILE, t)
    wide = jax.ShapeDtypeStruct((b, t, ML_INNER), BF16)
    return pl.pallas_call(
        _ml_proj_body,
        grid=(b, t // tm),
        in_specs=_row_specs(t, tm) + [_MOD_SPEC, _full((3, D)), _full((D, 2 * ML_INNER)), _full((SUB, ML_INNER)),
                                      _full((ML_TILES, LANE, 2 * LANE)), _full((ML_TILES, LANE, LANE)),
                                      _full((3 * ML_INNER, LANE)), _full((1, LANE))],
        out_specs=[_tile(ML_INNER, tm)] * 5 + [_tile(LANE, tm)],
        out_shape=[wide] * 5 + [jax.ShapeDtypeStruct((b, t, LANE), F32)],
        scratch_shapes=[pltpu.VMEM((tm + 2 * SUB, D), BF16), pltpu.VMEM((tm + 2 * SUB, ML_INNER), F32)],
        compiler_params=_params("parallel", "parallel"),
        name="mlstm_proj",
    )(x, x, x, mod, ng, w["win"], w["cv"], w["wqk"], w["wv"], w["wif"], w["bif"])


def _ml_scan_body(qf_ref, qb_ref, kf_ref, kb_ref, vf_ref, vb_ref, gf_ref, gb_ref, c0_ref, nm0_ref,
                  hf_ref, hb_ref, c_ref, nm_ref):
    c = pl.program_id(1)
    n = qf_ref.shape[0]
    refs = ((qf_ref, kf_ref, vf_ref, gf_ref, hf_ref), (qb_ref, kb_ref, vb_ref, gb_ref, hb_ref))

    @pl.when(c == 0)
    def _():
        c_ref[...] = c0_ref[...]
        nm_ref[...] = nm0_ref[...]

    tri = _iota((n, n), 0) - _iota((n, n), 1)
    valid = [tri >= 0, tri <= 0]
    dirs = range(2)
    gt, cum, tot, gt_t, cum_t = [], [], [], [], []
    for d in dirs:
        gt.append(refs[d][3][...])
        cum.append(_masked_cumsum(valid[d].astype(BF16), gt[d], 3))
        tot.append(jnp.sum(gt[d], axis=0, keepdims=True))
        gt_t.append(gt[d].T)
        cum_t.append(cum[d].T)
    heads = [(d, h) for d in dirs for h in range(ML_HEADS)]
    sls = [slice(h * ML_DH, (h + 1) * ML_DH) for h in range(ML_HEADS)]
    qs = {k: refs[k[0]][0][:, sls[k[1]]] for k in heads}
    ks = {k: refs[k[0]][1][:, sls[k[1]]] for k in heads}
    vs = {k: refs[k[0]][2][:, sls[k[1]]] for k in heads}
    ct = {k: c_ref[k[0], k[1]] for k in heads}
    m_prev = {k: nm_ref[k[0], ML_HEADS + k[1]:ML_HEADS + k[1] + 1, 0:1] for k in heads}
    n_prev = {k: nm_ref[k[0], k[1]:k[1] + 1, :] for k in heads}
    ig_lane = {k: 2 * ML_HEADS * k[0] + k[1] for k in heads}
    lf_lane = {k: 2 * ML_HEADS * k[0] + ML_HEADS + k[1] for k in heads}
    b_col = {k: cum[k[0]][:, lf_lane[k]:lf_lane[k] + 1] for k in heads}
    decay, m_t, inter = {}, {}, {}
    for k in heads:
        d = k[0]
        b_row = cum_t[d][lf_lane[k]:lf_lane[k] + 1]
        ig_row = gt_t[d][ig_lane[k]:ig_lane[k] + 1]
        logd = jnp.where(valid[d], b_col[k] - b_row + ig_row, -jnp.inf)
        m_inter = b_col[k] + m_prev[k]
        m_t[k] = jnp.maximum(m_inter, jnp.max(logd, axis=-1, keepdims=True))
        decay[k] = jnp.exp(logd - m_t[k])
        inter[k] = jnp.exp(m_inter - m_t[k])
    qk = {k: lax.dot_general(qs[k], ks[k], NT, preferred_element_type=F32) for k in heads}
    qc = {k: _mm(qs[k], ct[k].astype(BF16)) for k in heads}
    for k in heads:
        d, h = k
        b_tot = tot[d][:, lf_lane[k]:lf_lane[k] + 1]
        gj = b_tot - b_col[k] + gt[d][:, ig_lane[k]:ig_lane[k] + 1]
        m_end = b_tot + m_prev[k]
        m_new = jnp.maximum(m_end, jnp.max(gj, axis=0, keepdims=True))
        carry = jnp.exp(m_end - m_new)
        wk = jnp.exp(gj - m_new) * ks[k].astype(F32)
        c_ref[d, h] = carry * ct[k] + lax.dot_general(wk.astype(BF16), vs[k], TN, preferred_element_type=F32)
        nm_ref[d, h:h + 1, :] = carry * n_prev[k] + jnp.sum(wk, axis=0, keepdims=True)
        nm_ref[d, ML_HEADS + h:ML_HEADS + h + 1, :] = jnp.broadcast_to(m_new, (1, ML_DH))
    s = {k: qk[k] * decay[k] for k in heads}
    sv = {k: _mm(s[k], vs[k]) for k in heads}
    for k in heads:
        d, h = k
        num = sv[k] + inter[k] * qc[k]
        den = (jnp.sum(s[k], axis=-1, keepdims=True)
               + inter[k] * jnp.sum(qs[k].astype(F32) * n_prev[k], axis=-1, keepdims=True))
        refs[d][4][:, sls[h]] = (num / jnp.maximum(jnp.abs(den), jnp.exp(-m_t[k]))).astype(BF16)


def _ml_scan_call(q, k, v, gt, c0, nm0):
    b, t, _ = q.shape
    n = ML_CHUNK
    nc = t // n

    def fwd(width):
        return pl.BlockSpec((None, n, width), lambda bi, c: (bi, c, 0))

    def bwd(width):
        return pl.BlockSpec((None, n, width), lambda bi, c: (bi, nc - 1 - c, 0))

    cst = pl.BlockSpec((2, None, ML_HEADS, ML_DH, ML_DH), lambda bi, c: (0, bi, 0, 0, 0),
                       pipeline_mode=pl.Buffered(1))
    nst = pl.BlockSpec((2, None, 2 * ML_HEADS, ML_DH), lambda bi, c: (0, bi, 0, 0))
    seq = jax.ShapeDtypeStruct((b, t, ML_INNER), BF16)
    w = ML_INNER
    return pl.pallas_call(
        _ml_scan_body,
        grid=(b, nc),
        in_specs=[fwd(w), bwd(w), fwd(w), bwd(w), fwd(w), bwd(w), fwd(LANE), bwd(LANE), cst, nst],
        out_specs=[fwd(w), bwd(w), cst, nst],
        out_shape=[seq, seq, jax.ShapeDtypeStruct(c0.shape, F32), jax.ShapeDtypeStruct(nm0.shape, F32)],
        compiler_params=_params("parallel", "arbitrary"),
        name="mlstm_scan",
    )(q, q, k, k, v, v, gt, gt, c0, nm0)


def _ml_out_body(hf_ref, hb_ref, xc_ref, z_ref, x_ref, mod_ref, vec_ref, wo_ref, o_ref):
    y = hf_ref[...].astype(F32) + hb_ref[...].astype(F32)
    parts = []
    for h in range(ML_HEADS):
        yh = y[:, h * ML_DH:(h + 1) * ML_DH]
        parts.append(_rms(yh - jnp.mean(yh, axis=-1, keepdims=True), ML_NORM_EPS))
    yn = jnp.concatenate(parts, axis=-1) * vec_ref[0:1]
    o = _mm((yn + vec_ref[1:2] * xc_ref[...].astype(F32)) * _silu(z_ref[...].astype(F32)), wo_ref[...])
    o_ref[...] = x_ref[...] + mod_ref[5:6] * o


def _ml_out_call(hf, hb, xc, z, x, mod, w):
    b, t, _ = x.shape
    tm = min(ROW_TILE, t)
    return pl.pallas_call(
        _ml_out_body,
        grid=(b, t // tm),
        in_specs=[_tile(ML_INNER, tm)] * 4 + [_tile(D, tm), _MOD_SPEC, _full((SUB, ML_INNER)), _full((ML_INNER, D))],
        out_specs=_tile(D, tm),
        out_shape=jax.ShapeDtypeStruct((b, t, D), F32),
        compiler_params=_params("parallel", "parallel"),
        name="mlstm_readout",
    )(hf, hb, xc, z, x, mod, w["ovec"], w["wo"])


def _mlstm_weights(w_in, conv_w, conv_b, w_q, w_k, w_v, w_if, b_if, skip, norm_g, w_o):
    per_tile = LANE // ML_BLOCK
    eye = jnp.eye(per_tile, dtype=F32)

    def block_diag(w):
        wt = w.reshape(ML_TILES, per_tile, ML_BLOCK, ML_BLOCK)
        return jnp.einsum('tgio,gh->tgiho', wt, eye).reshape(ML_TILES, LANE, LANE)

    wqk = jnp.concatenate([block_diag(w_q), block_diag(w_k)], axis=-1).astype(BF16)
    n_gate = 2 * 2 * ML_HEADS
    wif = jnp.transpose(w_if, (1, 0, 2)).reshape(3 * ML_INNER, n_gate)
    wif = jnp.pad(wif, ((0, 0), (0, LANE - n_gate))).astype(BF16)
    bif = jnp.pad(b_if.reshape(1, n_gate), ((0, 0), (0, LANE - n_gate)))
    cv = jnp.concatenate([conv_w, conv_b[None], jnp.zeros((SUB - 4, ML_INNER), F32)], axis=0)
    ovec = jnp.concatenate([norm_g[None], skip[None], jnp.zeros((SUB - 2, ML_INNER), F32)], axis=0)
    return {"win": w_in.astype(BF16), "cv": cv, "wqk": wqk, "wv": block_diag(w_v).astype(BF16), "wif": wif,
            "bif": bif, "ovec": ovec, "wo": w_o.astype(BF16)}


def _mlstm_mixer(xc, xl, mc, ml, ng, w, ctx_out):
    b = xl.shape[0]
    pc = _ml_proj_call(xc, mc, ng, w)
    pl_ = _ml_proj_call(xl, ml, ng, w)
    c0 = jnp.zeros((2, b, ML_HEADS, ML_DH, ML_DH), F32)
    nm0 = jnp.zeros((2, b, 2 * ML_HEADS, ML_DH), F32)

    def scan(p, cs, nms):
        q, k, v, _, _, gt = p
        return _ml_scan_call(q, k, v, gt, cs, nms)

    def out(hf, hb, p, x, mod):
        return _ml_out_call(hf, hb, p[3], p[4], x, mod, w)

    hcf, hcb, cc, nmc = scan(pc, c0, nm0)
    hlf, hlb, _, _ = scan(pl_, cc, nmc)
    return (out(hcf, hcb, pc, xc, mc) if ctx_out else None), out(hlf, hlb, pl_, xl, ml)


def kernel(x, c, ctx, c_ctx, mod_w, mod_b, norm_g, ffn_w13, ffn_w2, final_g, rw_mu, rw_wrkv, rw_w0, rw_w1, rw_w2, rw_a0, rw_a1, rw_a2, rw_g1, rw_g2, rw_kk, rw_ka, rw_rk, rw_lnx_g, rw_lnx_b, rw_wo, rt_win, rt_log_gamma, rt_wo, ml_win, ml_conv_w, ml_conv_b, ml_wq, ml_wk, ml_wv, ml_wif, ml_bif, ml_skip, ml_norm_g, ml_wo):
    b = x.shape[0]
    depth = mod_w.shape[0]
    assert b < SUB
    cvec = jnp.concatenate([c, c_ctx[None], jnp.zeros((SUB - b - 1, D), F32)], axis=0)
    mods = _mod_call(cvec, mod_w, mod_b)
    w13 = ffn_w13.astype(BF16)
    w2 = ffn_w2.astype(BF16)
    xl, xc = x, ctx
    for i in range(depth):
        kind, s = i % 3, i // 3
        last = i == depth - 1
        ml = mods[i, :b].reshape(b, N_MOD, D)
        mc = jnp.broadcast_to(mods[i, b].reshape(1, N_MOD, D), (b, N_MOD, D))
        ng = norm_g[i]
        xl = _ffn_call(xl, ml, ng, w13, w2, i, 0)
        xc = _ffn_call(xc, mc, ng, w13, w2, i, 0)
        if kind == 0:
            w = _rwkv_weights(rw_mu[s], rw_wrkv[s], rw_w0[s], rw_w1[s], rw_w2[s], rw_a0[s], rw_a1[s], rw_a2[s],
                              rw_g1[s], rw_g2[s], rw_kk[s], rw_ka[s], rw_rk[s], rw_lnx_g[s], rw_lnx_b[s], rw_wo[s])
            xc, xl = _rwkv_mixer(xc, xl, mc, ml, ng, w, not last)
        elif kind == 1:
            xc, xl = _retention_mixer(xc, xl, mc, ml, ng, rt_win[s], rt_log_gamma[s], rt_wo[s], not last)
        else:
            w = _mlstm_weights(ml_win[s], ml_conv_w[s], ml_conv_b[s], ml_wq[s], ml_wk[s], ml_wv[s], ml_wif[s],
                               ml_bif[s], ml_skip[s], ml_norm_g[s], ml_wo[s])
            xc, xl = _mlstm_mixer(xc, xl, mc, ml, ng, w, not last)
        xl = _ffn_call(xl, ml, ng, w13, w2, i, 2, final_g if last else None)
        if not last:
            xc = _ffn_call(xc, mc, ng, w13, w2, i, 2)
    return xl
```
